```python
import math
import jax, jax.numpy as jnp
from jax import lax
import numpy as np

D_MODEL = 1024
BATCH = 16
SEQ = 4096
DEPTH = 2
DEC_BATCH = 16
DEC_SEQ = 16
PAST_LEN = 1024

CHUNK = 64
D_CONV = D_MODEL
CONV_W = 3
N_HEADS = 16
HEAD_DIM = D_MODEL // N_HEADS
N_KV_HEADS = 4
GROUP = N_HEADS // N_KV_HEADS
IDX_HEADS = 8
IDX_DIM = 64
TOPK_MAX = 256
NUM_BUCKETS = 32
MAX_DISTANCE = 128
D_FF = 2816
N_EXPERTS = 8
TOP_K_EXPERTS = 2
D_FF_EXPERT = 1408
Q_BLOCK = 128
N_DENSE = (DEPTH + 1) // 2
N_MOE = DEPTH // 2
ALPHA = (2 * DEPTH) ** 0.25
BETA = (8 * DEPTH) ** -0.25
LN_EPS = 1e-5
_SPLITS = (D_CONV, D_CONV, D_CONV, N_HEADS * HEAD_DIM, N_KV_HEADS * HEAD_DIM, N_KV_HEADS * HEAD_DIM,
           IDX_HEADS * IDX_DIM, IDX_DIM, IDX_HEADS, D_MODEL, D_MODEL)
N_IN = sum(_SPLITS)

kernel_name = 'hybrid_shortconv_dsa_stream_step'


def layer_norm(x, g, b):
    xf = x.astype(jnp.float32)
    mu = xf.mean(-1, keepdims=True)
    var = jnp.square(xf - mu).mean(-1, keepdims=True)
    return ((xf - mu) * lax.rsqrt(var + LN_EPS) * g.astype(jnp.float32) + b.astype(jnp.float32)).astype(x.dtype)


def split_proj(p):
    outs, start = [], 0
    for n in _SPLITS:
        outs.append(p[..., start:start + n])
        start += n
    return outs


def rel_bucket(rel):
    nb = NUM_BUCKETS // 2
    max_exact = nb // 2
    ret = jnp.where(rel > 0, nb, 0)
    n = jnp.abs(rel)
    nf = jnp.maximum(n, 1).astype(jnp.float32)
    large = max_exact + (jnp.log(nf / max_exact) / math.log(MAX_DISTANCE / max_exact) * (nb - max_exact)).astype(jnp.int32)
    large = jnp.minimum(large, nb - 1)
    return ret + jnp.where(n < max_exact, n, large)


def conv_branch(bg, cg, xin, prev):
    u = cg * xin
    up = jnp.concatenate([prev.astype(u.dtype), u], axis=1)
    return up, up[:, -(CONV_W - 1):]


def conv_apply(bg, up, conv_w):
    t = bg.shape[1]
    y = up[:, 0:t] * conv_w[0] + up[:, 1:t + 1] * conv_w[1] + up[:, 2:t + 2] * conv_w[2]
    return bg * y


def attend_block(q, qi, wi, q_pos, k, v, ki, k_pos, topk, bias_table):
    f32 = jnp.float32
    b, nq = q.shape[:2]
    s = jnp.einsum('bqhd,bsd->bqhs', qi.astype(f32), ki.astype(f32)) * (IDX_DIM ** -0.5)
    score = jnp.einsum('bqhs,bqh->bqs', jax.nn.relu(s), wi.astype(f32)) * (IDX_HEADS ** -0.5)
    admissible = (k_pos[None, :] // CHUNK) <= (q_pos[:, None] // CHUNK)
    score = jnp.where(admissible[None], score, -jnp.inf)
    _, idx = lax.top_k(score, topk)
    gather = jax.vmap(lambda arr, ix: arr[ix])
    k_sel = gather(k, idx).astype(f32)
    v_sel = gather(v, idx).astype(f32)
    sel_pos = k_pos[idx]
    valid = (sel_pos // CHUNK) <= (q_pos[None, :, None] // CHUNK)
    bias = bias_table.astype(f32)[rel_bucket(sel_pos - q_pos[None, :, None])]
    bias = bias.reshape(b, nq, topk, N_KV_HEADS, GROUP).transpose(0, 1, 3, 4, 2)
    qg = q.reshape(b, nq, N_KV_HEADS, GROUP, HEAD_DIM).astype(f32)
    logits = jnp.einsum('bqkgd,bqjkd->bqkgj', qg, k_sel) * (HEAD_DIM ** -0.5) + bias
    logits = jnp.where(valid[:, :, None, None, :], logits, -jnp.inf)
    p = jax.nn.softmax(logits, axis=-1)
    o = jnp.einsum('bqkgj,bqjkd->bqkgd', p, v_sel)
    return o.reshape(b, nq, N_HEADS * HEAD_DIM).astype(q.dtype)


def prompt_attention(q, qi, wi, k, v, ki, bias_table):
    b, s = q.shape[:2]
    nb = s // Q_BLOCK
    pos = jnp.arange(s, dtype=jnp.int32)
    topk = min(TOPK_MAX, s // 4)

    def blk(a):
        return jnp.moveaxis(a.reshape((b, nb, Q_BLOCK) + a.shape[2:]), 1, 0)

    def body(args):
        qb, qib, wib, pb = args
        return attend_block(qb, qib, wib, pb, k, v, ki, pos, topk, bias_table)

    out = lax.map(body, (blk(q), blk(qi), blk(wi), pos.reshape(nb, Q_BLOCK)))
    return jnp.moveaxis(out, 0, 1).reshape(b, s, N_HEADS * HEAD_DIM)


def sample_attention(q, qi, wi, k_new, v_new, ki_new, k_past, v_past, ki_past, bias_table):
    t = q.shape[1]
    past = k_past.shape[1]
    k = jnp.concatenate([k_past.astype(k_new.dtype), k_new], axis=1)
    v = jnp.concatenate([v_past.astype(v_new.dtype), v_new], axis=1)
    ki = jnp.concatenate([ki_past.astype(ki_new.dtype), ki_new], axis=1)
    k_pos = jnp.arange(past + t, dtype=jnp.int32)
    q_pos = past + jnp.arange(t, dtype=jnp.int32)
    topk = min(TOPK_MAX, (past + t) // 4)
    return attend_block(q, qi, wi, q_pos, k, v, ki, k_pos, topk, bias_table)


def token_mixer(x, w_in, conv_w, w_out, conv_prev, past, bias_table):
    b, t = x.shape[:2]
    bg, cg, xin, q, k, v, qi, ki, wi, ga, gb = split_proj(x @ w_in)
    up, conv_new = conv_branch(bg, cg, xin, conv_prev)
    y_a = conv_apply(bg, up, conv_w)
    q = q.reshape(b, t, N_HEADS, HEAD_DIM)
    k = k.reshape(b, t, N_KV_HEADS, HEAD_DIM)
    v = v.reshape(b, t, N_KV_HEADS, HEAD_DIM)
    qi = qi.reshape(b, t, IDX_HEADS, IDX_DIM)
    if past is None:
        y_b = prompt_attention(q, qi, wi, k, v, ki, bias_table)
    else:
        y_b = sample_attention(q, qi, wi, k, v, ki, past[0], past[1], past[2], bias_table)
    merged = jax.nn.sigmoid(ga) * y_a + jax.nn.sigmoid(gb) * y_b
    return merged @ w_out, k, v, ki, conv_new


def swiglu(x, wg, wu, wd):
    return (jax.nn.silu(x @ wg) * (x @ wu)) @ wd


def moe_ffn(x, router_w, router_b, wg, wu, wd):
    f32 = jnp.float32
    logits = x.astype(f32) @ router_w.astype(f32) + router_b.astype(f32)
    top_val, top_idx = lax.top_k(logits, TOP_K_EXPERTS)
    gates = jax.nn.softmax(top_val, axis=-1)
    combine = jnp.sum(jax.nn.one_hot(top_idx, N_EXPERTS, dtype=f32) * gates[..., None], axis=-2).astype(x.dtype)
    out = jnp.zeros_like(x)
    for e in range(N_EXPERTS):
        out = out + combine[..., e:e + 1] * swiglu(x, wg[e], wu[e], wd[e])
    return out


def trunk_layer(x, l, conv_prev, past, w_in, conv_w, w_out, rel_bias, ln1_g, ln1_b, ln2_g, ln2_b,
                ffn_w_gate, ffn_w_up, ffn_w_down, router_w, router_b, moe_w_gate, moe_w_up, moe_w_down):
    m, k, v, ki, conv_new = token_mixer(x, w_in[l], conv_w[l], w_out[l], conv_prev, past, rel_bias)
    x = layer_norm(ALPHA * x + m, ln1_g[l], ln1_b[l])
    j = l // 2
    if l % 2 == 0:
        f = swiglu(x, ffn_w_gate[j], ffn_w_up[j], ffn_w_down[j])
    else:
        f = moe_ffn(x, router_w[j], router_b[j], moe_w_gate[j], moe_w_up[j], moe_w_down[j])
    x = layer_norm(ALPHA * x + f, ln2_g[l], ln2_b[l])
    return x, k, v, ki, conv_new


def setup_inputs(seed: int = 0) -> dict:
    key = jax.random.key(seed)
    ks = jax.random.split(key, 24)
    nrm = lambda k, shape, s: jax.random.normal(k, shape, jnp.float32) * s
    return {
        'x_prompt': nrm(ks[0], (BATCH, SEQ, D_MODEL), 1.0),
        'x_sample': nrm(ks[1], (DEC_BATCH, DEC_SEQ, D_MODEL), 1.0),
        'cache_k': nrm(ks[2], (DEPTH, DEC_BATCH, PAST_LEN, N_KV_HEADS, HEAD_DIM), 1.0),
        'cache_v': nrm(ks[3], (DEPTH, DEC_BATCH, PAST_LEN, N_KV_HEADS, HEAD_DIM), 1.0),
        'cache_kidx': nrm(ks[4], (DEPTH, DEC_BATCH, PAST_LEN, IDX_DIM), 1.0),
        'state_conv': nrm(ks[5], (DEPTH, DEC_BATCH, CONV_W - 1, D_CONV), 1.0),
        'w_in': nrm(ks[6], (DEPTH, D_MODEL, N_IN), D_MODEL ** -0.5),
        'conv_w': nrm(ks[7], (DEPTH, CONV_W, D_CONV), 0.5),
        'w_out': nrm(ks[8], (DEPTH, D_MODEL, D_MODEL), BETA * D_MODEL ** -0.5),
        'rel_bias': nrm(ks[9], (NUM_BUCKETS, N_HEADS), 0.5),
        'ln1_g': 1.0 + nrm(ks[10], (DEPTH, D_MODEL), 0.02),
        'ln1_b': nrm(ks[11], (DEPTH, D_MODEL), 0.02),
        'ln2_g': 1.0 + nrm(ks[12], (DEPTH, D_MODEL), 0.02),
        'ln2_b': nrm(ks[13], (DEPTH, D_MODEL), 0.02),
        'ffn_w_gate': nrm(ks[14], (N_DENSE, D_MODEL, D_FF), D_MODEL ** -0.5),
        'ffn_w_up': nrm(ks[15], (N_DENSE, D_MODEL, D_FF), D_MODEL ** -0.5),
        'ffn_w_down': nrm(ks[16], (N_DENSE, D_FF, D_MODEL), BETA * D_FF ** -0.5),
        'router_w': nrm(ks[17], (N_MOE, D_MODEL, N_EXPERTS), D_MODEL ** -0.5),
        'router_b': nrm(ks[18], (N_MOE, N_EXPERTS), 0.01),
        'moe_w_gate': nrm(ks[19], (N_MOE, N_EXPERTS, D_MODEL, D_FF_EXPERT), D_MODEL ** -0.5),
        'moe_w_up': nrm(ks[20], (N_MOE, N_EXPERTS, D_MODEL, D_FF_EXPERT), D_MODEL ** -0.5),
        'moe_w_down': nrm(ks[21], (N_MOE, N_EXPERTS, D_FF_EXPERT, D_MODEL), BETA * D_FF_EXPERT ** -0.5),
    }


def reference(x_prompt, x_sample, cache_k, cache_v, cache_kidx, state_conv, w_in, conv_w, w_out, rel_bias,
              ln1_g, ln1_b, ln2_g, ln2_b, ffn_w_gate, ffn_w_up, ffn_w_down, router_w, router_b,
              moe_w_gate, moe_w_up, moe_w_down):
    weights = (w_in, conv_w, w_out, rel_bias, ln1_g, ln1_b, ln2_g, ln2_b, ffn_w_gate, ffn_w_up, ffn_w_down,
               router_w, router_b, moe_w_gate, moe_w_up, moe_w_down)
    xp, xs = x_prompt, x_sample
    kp, vp, kip, cp = [], [], [], []
    kss, vss, kis, css = [], [], [], []
    zero_conv = jnp.zeros((xp.shape[0], CONV_W - 1, D_CONV), xp.dtype)
    for l in range(DEPTH):
        xp, k1, v1, ki1, c1 = trunk_layer(xp, l, zero_conv, None, *weights)
        kp.append(k1); vp.append(v1); kip.append(ki1); cp.append(c1)
        past = (cache_k[l], cache_v[l], cache_kidx[l])
        xs, k2, v2, ki2, c2 = trunk_layer(xs, l, state_conv[l], past, *weights)
        kss.append(k2); vss.append(v2); kis.append(ki2); css.append(c2)
    return (xp, xs, jnp.stack(kp), jnp.stack(vp), jnp.stack(kip), jnp.stack(cp),
            jnp.stack(kss), jnp.stack(vss), jnp.stack(kis), jnp.stack(css))
```

```python
import functools
import math

import jax
import jax.numpy as jnp
from jax import lax
from jax.experimental import pallas as pl
from jax.experimental.pallas import tpu as pltpu

D_MODEL = 1024
CHUNK = 64
CHUNK_SHIFT = 6
CONV_W = 3
N_HEADS = 16
HEAD_DIM = 64
N_KV_HEADS = 4
GROUP = N_HEADS // N_KV_HEADS
IDX_HEADS = 8
IDX_DIM = 64
TOPK_MAX = 256
NUM_BUCKETS = 32
MAX_DISTANCE = 128
D_FF = 2816
N_EXPERTS = 8
D_FF_EXPERT = 1408
LN_EPS = 1e-5

LANES = 128
SUBLANES = 8
N_PAIRS = N_HEADS // 2
NEG = -1e30
INT_MIN = -2 ** 31
VMEM_LIMIT = 56 * 1024 * 1024

F32 = jnp.float32
BF16 = jnp.bfloat16
I32 = jnp.int32

assert CHUNK == 1 << CHUNK_SHIFT and 2 * HEAD_DIM == LANES and 2 * IDX_DIM == LANES


def _dot(a, b):
    return jnp.dot(a, b, preferred_element_type=F32)


def _dot_nt(a, b):
    return lax.dot_general(a, b, (((1,), (1,)), ((), ())), preferred_element_type=F32)


def _layer_norm(y, g, b):
    mu = jnp.mean(y, axis=-1, keepdims=True)
    d = y - mu
    var = jnp.mean(d * d, axis=-1, keepdims=True)
    return d * lax.rsqrt(var + LN_EPS) * g + b


def _inproj_kernel(x_ref, prev_ref, wa_ref, wq_ref, wgb_ref, wkk_ref, wvv_ref, wqi_ref, wki_ref, cw_ref,
                   conv_ref, q_ref, sgb_ref, kkf_ref, kkb_ref, vvf_ref, vvb_ref, qi_ref, kiki_ref, kiwi_ref,
                   cstate_ref, uext_ref, *, tm):
    @pl.when(pl.program_id(1) == 0)
    def _():
        uext_ref[0:SUBLANES, :] = prev_ref[0]

    xb = x_ref[0].astype(BF16)
    cc = 256
    for c in range(0, D_MODEL, cc):
        bg = _dot(xb, wa_ref[:, c:c + cc])
        cg = _dot(xb, wa_ref[:, D_MODEL + c:D_MODEL + c + cc])
        xin = _dot(xb, wa_ref[:, 2 * D_MODEL + c:2 * D_MODEL + c + cc])
        ga = _dot(xb, wa_ref[:, 3 * D_MODEL + c:3 * D_MODEL + c + cc])
        u = cg * xin
        uext_ref[SUBLANES:SUBLANES + tm, c:c + cc] = u
        um1 = uext_ref[SUBLANES - 1:SUBLANES - 1 + tm, c:c + cc]
        um2 = uext_ref[SUBLANES - 2:SUBLANES - 2 + tm, c:c + cc]
        y = um2 * cw_ref[0:1, c:c + cc] + um1 * cw_ref[1:2, c:c + cc] + u * cw_ref[2:3, c:c + cc]
        conv_ref[0, :, c:c + cc] = jax.nn.sigmoid(ga) * (bg * y)
    tail = uext_ref[tm:tm + SUBLANES, :]
    cstate_ref[0] = tail
    uext_ref[0:SUBLANES, :] = tail

    q_ref[0] = (_dot(xb, wq_ref[...]) * (HEAD_DIM ** -0.5)).astype(BF16)
    sgb_ref[0] = jax.nn.sigmoid(_dot(xb, wgb_ref[...]))
    kk = _dot(xb, wkk_ref[...])
    kkf_ref[0] = kk
    kkb_ref[0] = kk.astype(BF16)
    vv = _dot(xb, wvv_ref[...])
    vvf_ref[0] = vv
    vvb_ref[0] = vv.astype(BF16)
    qi_ref[0] = _dot(xb, wqi_ref[...]).astype(BF16)
    kw = _dot(xb, wki_ref[...])
    kiki_ref[0] = kw[:, 0:LANES].astype(BF16)
    kiwi_ref[0] = kw[:, LANES:2 * LANES]


def _inproj(x, prev8, w, conv_w, tm):
    b, t, _ = x.shape
    grid = (b, t // tm)
    row = lambda n: pl.BlockSpec((1, tm, n), lambda i, j: (i, j, 0))
    full = lambda a: pl.BlockSpec(a.shape, lambda i, j: (0,) * a.ndim, pipeline_mode=pl.Buffered(1))
    out_shapes = (
        jax.ShapeDtypeStruct((b, t, D_MODEL), F32),
        jax.ShapeDtypeStruct((b, t, D_MODEL), BF16),
        jax.ShapeDtypeStruct((b, t, D_MODEL), F32),
        jax.ShapeDtypeStruct((b, t, 4 * LANES), F32),
        jax.ShapeDtypeStruct((b, t, 4 * LANES), BF16),
        jax.ShapeDtypeStruct((b, t, 4 * LANES), F32),
        jax.ShapeDtypeStruct((b, t, 4 * LANES), BF16),
        jax.ShapeDtypeStruct((b, t, IDX_HEADS * IDX_DIM), BF16),
        jax.ShapeDtypeStruct((b, t, LANES), BF16),
        jax.ShapeDtypeStruct((b, t, LANES), F32),
        jax.ShapeDtypeStruct((b, SUBLANES, D_MODEL), F32),
    )
    out_specs = (row(D_MODEL), row(D_MODEL), row(D_MODEL), row(4 * LANES), row(4 * LANES), row(4 * LANES),
                 row(4 * LANES), row(IDX_HEADS * IDX_DIM), row(LANES), row(LANES),
                 pl.BlockSpec((1, SUBLANES, D_MODEL), lambda i, j: (i, 0, 0)))
    in_specs = [row(D_MODEL), pl.BlockSpec((1, SUBLANES, D_MODEL), lambda i, j: (i, 0, 0)),
                full(w['a']), full(w['q']), full(w['gb']), full(w['kk']), full(w['vv']), full(w['qi']),
                full(w['ki']), full(conv_w)]
    return pl.pallas_call(
        functools.partial(_inproj_kernel, tm=tm),
        grid=grid, in_specs=in_specs, out_specs=out_specs, out_shape=out_shapes,
        scratch_shapes=[pltpu.VMEM((tm + SUBLANES, D_MODEL), F32)],
        compiler_params=pltpu.CompilerParams(dimension_semantics=("arbitrary", "arbitrary"),
                                             vmem_limit_bytes=VMEM_LIMIT),
        name="inproj_conv",
    )(x, prev8, w['a'], w['q'], w['gb'], w['kk'], w['vv'], w['qi'], w['ki'], conv_w)


def _bias_tiles_kernel(tbl_ref, out_ref, *, nq):
    d = pl.program_id(0) - 2
    row = lax.broadcasted_iota(I32, (nq, LANES), 0)
    lane = lax.broadcasted_iota(I32, (nq, LANES), 1)
    rel = d * LANES + lane - row
    n = jnp.abs(rel)
    nb = NUM_BUCKETS // 2
    max_exact = nb // 2
    n2 = n * n
    large = jnp.full((nq, LANES), max_exact, I32)
    for j in range(1, nb - max_exact):
        large = large + (n2 >= (max_exact * max_exact) * (1 << j)).astype(I32)
    bucket = jnp.where(n < max_exact, n, large) + jnp.where(rel > 0, nb, 0)
    far = pl.program_id(0) == 0
    for h in range(N_HEADS):
        base = tbl_ref[nb - 1, h]
        acc = jnp.zeros((nq, LANES), F32)
        for bk in range(NUM_BUCKETS):
            acc = jnp.where(bucket == bk, tbl_ref[bk, h] - base, acc)
        out_ref[0, h] = jnp.where(far, 0.0, acc)


def _bias_tiles(rel_bias, nq):
    assert MAX_DISTANCE == 128 and NUM_BUCKETS == 32
    return pl.pallas_call(
        functools.partial(_bias_tiles_kernel, nq=nq),
        grid=(3,),
        in_specs=[pl.BlockSpec(memory_space=pltpu.SMEM)],
        out_specs=pl.BlockSpec((1, N_HEADS, nq, LANES), lambda i: (i, 0, 0, 0)),
        out_shape=jax.ShapeDtypeStruct((3, N_HEADS, nq, LANES), F32),
        name="bias_tiles",
    )(rel_bias)


def _attn_kernel(q_ref, qi_ref, kiwi_ref, conv_ref, sgb_ref, kk_ref, vv_ref, kiki_ref, bt_ref, out_ref,
                 key_ref, mask_ref, qis_ref, wis_ref, cut_ref, *, nq, q_off, kv_len, topk):
    q0 = q_off + pl.program_id(1) * nq
    nkb = (q0 + nq + LANES - 1) // LANES
    jd = q0 // LANES
    row = lax.broadcasted_iota(I32, (nq, LANES), 0)
    lane = lax.broadcasted_iota(I32, (nq, LANES), 1)
    low = lane < HEAD_DIM
    qchunk = lax.shift_right_arithmetic(q0 + row, CHUNK_SHIFT)

    kiwi = kiwi_ref[0] * ((IDX_DIM ** -0.5) * (IDX_HEADS ** -0.5))
    for h in range(IDX_HEADS):
        pair = qi_ref[0, :, (h // 2) * LANES:(h // 2 + 1) * LANES].astype(F32)
        keep = low if h % 2 == 0 else jnp.logical_not(low)
        qis_ref[h] = jnp.where(keep, pair, 0.0).astype(BF16)
        wi_h = jnp.sum(jnp.where(lane == IDX_DIM + h, kiwi, 0.0), axis=1, keepdims=True)
        wis_ref[h] = jnp.broadcast_to(wi_h, (nq, LANES))

    def score_block(j, carry):
        ks = pl.multiple_of(j * LANES, LANES)
        ki = kiki_ref[0, pl.ds(ks, LANES), :]
        sc = jnp.zeros((nq, LANES), F32)
        for h in range(IDX_HEADS):
            sc = sc + jnp.maximum(_dot_nt(qis_ref[h], ki), 0.0) * wis_ref[h]
        bits = lax.bitcast_convert_type(sc, I32)
        key = jnp.where(bits < 0, bits ^ 0x7FFFFFFF, bits)
        kpos = j * LANES + lane
        adm = (lax.shift_right_arithmetic(kpos, CHUNK_SHIFT) <= qchunk) & (kpos < kv_len)
        key_ref[:, pl.ds(ks, LANES)] = jnp.where(adm, key, INT_MIN)
        return carry

    lax.fori_loop(0, nkb, score_block, 0)

    def count(pred):
        def body(j, acc):
            ks = pl.multiple_of(j * LANES, LANES)
            return acc + pred(key_ref[:, pl.ds(ks, LANES)], j).astype(I32)
        acc = lax.fori_loop(0, nkb, body, jnp.zeros((nq, LANES), I32))
        return jnp.sum(acc.astype(F32), axis=1, keepdims=True).astype(I32)

    def thr_step(i, t):
        cand = t + lax.shift_left(jnp.int32(1), 31 - i)
        cand_b = jnp.broadcast_to(cand, (nq, LANES))
        cnt = count(lambda blk, j: blk >= cand_b)
        return jnp.where(cnt >= topk, cand, t)

    thr = lax.fori_loop(0, 32, thr_step, jnp.full((nq, 1), INT_MIN, I32))
    thr_b = jnp.broadcast_to(thr, (nq, LANES))

    n_gt = count(lambda blk, j: blk > thr_b)
    n_eq = count(lambda blk, j: blk == thr_b)
    want = topk - n_gt
    cut_ref[...] = jnp.full((nq, 1), 2 ** 30, I32)
    need = (n_eq > want) & (thr > INT_MIN)

    @pl.when(jnp.max(need.astype(F32)) > 0.0)
    def _():
        nbits = max(int(kv_len + LANES).bit_length(), 1)

        def cut_step(i, c):
            cand = c + lax.shift_left(jnp.int32(1), nbits - 1 - i)
            cand_b = jnp.broadcast_to(cand, (nq, LANES))
            cnt = count(lambda blk, j: (blk == thr_b) & (j * LANES + lane < cand_b))
            return jnp.where(cnt <= want - 1, cand, c)

        cut_ref[...] = lax.fori_loop(0, nbits, cut_step, jnp.zeros((nq, 1), I32))

    cut_b = jnp.broadcast_to(cut_ref[...], (nq, LANES))

    def mask_block(j, carry):
        ks = pl.multiple_of(j * LANES, LANES)
        key = key_ref[:, pl.ds(ks, LANES)]
        tie = (key == thr_b) & (thr_b > INT_MIN) & (j * LANES + lane <= cut_b)
        mask_ref[:, pl.ds(ks, LANES)] = jnp.where((key > thr_b) | tie, 0.0, NEG)
        return carry

    lax.fori_loop(0, nkb, mask_block, 0)

    for p in range(N_PAIRS):
        g = (2 * p) // GROUP
        qp = q_ref[0, :, p * LANES:(p + 1) * LANES].astype(F32)
        qa = jnp.where(low, qp, 0.0).astype(BF16)
        qb = jnp.where(low, 0.0, qp).astype(BF16)

        def attn_block(j, carry):
            ma, la, mb, lb, acc = carry
            ks = pl.multiple_of(j * LANES, LANES)
            kblk = kk_ref[0, pl.ds(ks, LANES), g * LANES:(g + 1) * LANES]
            vblk = vv_ref[0, pl.ds(ks, LANES), g * LANES:(g + 1) * LANES].astype(F32)
            lane_k = lax.broadcasted_iota(I32, (LANES, LANES), 1)
            va = jnp.where(lane_k < HEAD_DIM, vblk, 0.0).astype(BF16)
            vb = jnp.where(lane_k < HEAD_DIM, 0.0, vblk).astype(BF16)
            msk = mask_ref[:, pl.ds(ks, LANES)]
            tile = jnp.where(j == jd, 2, jnp.where(j == jd - 1, 1, 0))
            sa = _dot_nt(qa, kblk) + msk + bt_ref[tile, 2 * p]
            sb = _dot_nt(qb, kblk) + msk + bt_ref[tile, 2 * p + 1]
            ma_new = jnp.maximum(ma, jnp.max(sa, axis=1, keepdims=True))
            mb_new = jnp.maximum(mb, jnp.max(sb, axis=1, keepdims=True))
            aa = jnp.exp(ma - ma_new)
            ab = jnp.exp(mb - mb_new)
            pa = jnp.exp(sa - ma_new)
            pb = jnp.exp(sb - mb_new)
            la = aa * la + jnp.sum(pa, axis=1, keepdims=True)
            lb = ab * lb + jnp.sum(pb, axis=1, keepdims=True)
            acc = jnp.where(low, aa, ab) * acc + _dot(pa.astype(BF16), va) + _dot(pb.astype(BF16), vb)
            return ma_new, la, mb_new, lb, acc

        init = (jnp.full((nq, 1), NEG, F32), jnp.zeros((nq, 1), F32),
                jnp.full((nq, 1), NEG, F32), jnp.zeros((nq, 1), F32), jnp.zeros((nq, LANES), F32))
        _, la, _, lb, acc = lax.fori_loop(0, nkb, attn_block, init)
        o = acc / jnp.where(low, la, lb)
        sl = slice(p * LANES, (p + 1) * LANES)
        out_ref[0, :, sl] = (conv_ref[0, :, sl] + sgb_ref[0, :, sl] * o).astype(BF16)


def _attention(q, qi, kiwi, conv, sgb, kk, vv, kiki, bias_tiles, *, nq, q_off, kv_len, topk):
    b, t, _ = q.shape
    lpad = kk.shape[1]
    assert q_off % LANES == 0 and nq <= LANES and (t == nq or nq == LANES)
    assert lpad % LANES == 0 and lpad >= q_off + t
    grid = (b, t // nq)
    row = lambda n: pl.BlockSpec((1, nq, n), lambda i, j: (i, j, 0))
    keys = lambda n: pl.BlockSpec((1, lpad, n), lambda i, j: (i, 0, 0))
    return pl.pallas_call(
        functools.partial(_attn_kernel, nq=nq, q_off=q_off, kv_len=kv_len, topk=topk),
        grid=grid,
        in_specs=[row(D_MODEL), row(IDX_HEADS * IDX_DIM), row(LANES), row(D_MODEL), row(D_MODEL),
                  keys(4 * LANES), keys(4 * LANES), keys(LANES),
                  pl.BlockSpec((3, N_HEADS, nq, LANES), lambda i, j: (0, 0, 0, 0))],
        out_specs=row(D_MODEL),
        out_shape=jax.ShapeDtypeStruct((b, t, D_MODEL), BF16),
        scratch_shapes=[pltpu.VMEM((nq, lpad), I32), pltpu.VMEM((nq, lpad), F32),
                        pltpu.VMEM((IDX_HEADS, nq, LANES), BF16), pltpu.VMEM((IDX_HEADS, nq, LANES), F32),
                        pltpu.VMEM((nq, 1), I32)],
        compiler_params=pltpu.CompilerParams(dimension_semantics=("arbitrary", "arbitrary"),
                                             vmem_limit_bytes=VMEM_LIMIT),
        name="dsa_attention",
    )(q, qi, kiwi, conv, sgb, kk, vv, kiki, bias_tiles)


def _outproj_kernel(m_ref, x_ref, w_ref, g_ref, b_ref, out_ref, *, alpha):
    y = alpha * x_ref[...] + _dot(m_ref[...], w_ref[...])
    out_ref[...] = _layer_norm(y, g_ref[...], b_ref[...])


def _outproj_ln(merged, x, w_out, g, b, alpha, tm):
    n = x.shape[0]
    row = pl.BlockSpec((tm, D_MODEL), lambda i: (i, 0))
    vec = pl.BlockSpec((1, D_MODEL), lambda i: (0, 0))
    return pl.pallas_call(
        functools.partial(_outproj_kernel, alpha=alpha),
        grid=(n // tm,),
        in_specs=[row, row, pl.BlockSpec((D_MODEL, D_MODEL), lambda i: (0, 0)), vec, vec],
        out_specs=row,
        out_shape=jax.ShapeDtypeStruct((n, D_MODEL), F32),
        compiler_params=pltpu.CompilerParams(dimension_semantics=("arbitrary",), vmem_limit_bytes=VMEM_LIMIT),
        name="outproj_ln",
    )(merged, x, w_out, g, b)


def _ffn_kernel(x_ref, wg_ref, wu_ref, wd_ref, g_ref, b_ref, out_ref, *, alpha):
    x = x_ref[...]
    xb = x.astype(BF16)
    hg = _dot(xb, wg_ref[...])
    h = (hg * jax.nn.sigmoid(hg)) * _dot(xb, wu_ref[...])
    f = _dot(h.astype(BF16), wd_ref[...])
    out_ref[...] = _layer_norm(alpha * x + f, g_ref[...], b_ref[...])


def _ffn_ln(x, wg, wu, wd, g, b, alpha, tm):
    n = x.shape[0]
    row = pl.BlockSpec((tm, D_MODEL), lambda i: (i, 0))
    vec = pl.BlockSpec((1, D_MODEL), lambda i: (0, 0))
    full = lambda a: pl.BlockSpec(a.shape, lambda i: (0, 0), pipeline_mode=pl.Buffered(1))
    return pl.pallas_call(
        functools.partial(_ffn_kernel, alpha=alpha),
        grid=(n // tm,),
        in_specs=[row, full(wg), full(wu), full(wd), vec, vec],
        out_specs=row,
        out_shape=jax.ShapeDtypeStruct((n, D_MODEL), F32),
        compiler_params=pltpu.CompilerParams(dimension_semantics=("arbitrary",), vmem_limit_bytes=VMEM_LIMIT),
        name="ffn_ln",
    )(x, wg, wu, wd, g, b)


def _moe_kernel(x_ref, rw_ref, rb_ref, wg_ref, wu_ref, wd_ref, g_ref, b_ref, out_ref, comb_ref, acc_ref,
                *, alpha, tm):
    e = pl.program_id(1)
    lane = lax.broadcasted_iota(I32, (tm, LANES), 1)

    @pl.when(e == 0)
    def _():
        logits = jnp.dot(x_ref[...], rw_ref[...], preferred_element_type=F32,
                         precision=lax.Precision.HIGHEST) + rb_ref[...]
        logits = jnp.where(lane < N_EXPERTS, logits, -jnp.inf)
        v1 = jnp.max(logits, axis=1, keepdims=True)
        lanef = lane.astype(F32)
        i1 = jnp.min(jnp.where(logits == v1, lanef, float(LANES)), axis=1, keepdims=True)
        rest = jnp.where(lanef == i1, -jnp.inf, logits)
        v2 = jnp.max(rest, axis=1, keepdims=True)
        i2 = jnp.min(jnp.where(rest == v2, lanef, float(LANES)), axis=1, keepdims=True)
        e2 = jnp.exp(v2 - v1)
        den = 1.0 + e2
        comb_ref[...] = jnp.where(lanef == i1, 1.0 / den, 0.0) + jnp.where(lanef == i2, e2 / den, 0.0)
        acc_ref[...] = jnp.zeros((tm, D_MODEL), F32)

    xb = x_ref[...].astype(BF16)
    hg = _dot(xb, wg_ref[...])
    h = (hg * jax.nn.sigmoid(hg)) * _dot(xb, wu_ref[...])
    f = _dot(h.astype(BF16), wd_ref[...])
    ce = jnp.sum(jnp.where(lane == e, comb_ref[...], 0.0), axis=1, keepdims=True)
    acc_ref[...] += ce * f

    @pl.when(e == N_EXPERTS - 1)
    def _():
        out_ref[...] = _layer_norm(alpha * x_ref[...] + acc_ref[...], g_ref[...], b_ref[...])


def _moe_ln(x, rw, rb, wg, wu, wd, g, b, alpha, tm):
    n = x.shape[0]
    row = pl.BlockSpec((tm, D_MODEL), lambda i, e: (i, 0))
    vec = pl.BlockSpec((1, D_MODEL), lambda i, e: (0, 0))
    return pl.pallas_call(
        functools.partial(_moe_kernel, alpha=alpha, tm=tm),
        grid=(n // tm, N_EXPERTS),
        in_specs=[row, pl.BlockSpec((D_MODEL, LANES), lambda i, e: (0, 0)),
                  pl.BlockSpec((1, LANES), lambda i, e: (0, 0)),
                  pl.BlockSpec((None, D_MODEL, D_FF_EXPERT), lambda i, e: (e, 0, 0)),
                  pl.BlockSpec((None, D_MODEL, D_FF_EXPERT), lambda i, e: (e, 0, 0)),
                  pl.BlockSpec((None, D_FF_EXPERT, D_MODEL), lambda i, e: (e, 0, 0)),
                  vec, vec],
        out_specs=row,
        out_shape=jax.ShapeDtypeStruct((n, D_MODEL), F32),
        scratch_shapes=[pltpu.VMEM((tm, LANES), F32), pltpu.VMEM((tm, D_MODEL), F32)],
        compiler_params=pltpu.CompilerParams(dimension_semantics=("arbitrary", "arbitrary"),
                                             vmem_limit_bytes=VMEM_LIMIT),
        name="moe_ln",
    )(x, rw, rb, wg, wu, wd, g, b)


def _split_w_in(w):
    sizes = (D_MODEL, D_MODEL, D_MODEL, N_HEADS * HEAD_DIM, N_KV_HEADS * HEAD_DIM, N_KV_HEADS * HEAD_DIM,
             IDX_HEADS * IDX_DIM, IDX_DIM, IDX_HEADS, D_MODEL, D_MODEL)
    parts, start = [], 0
    for n in sizes:
        parts.append(w[:, start:start + n])
        start += n
    bg, cg, xin, q, k, v, qi, ki, wi, ga, gb = parts

    def dup(a):
        a = a.reshape(a.shape[0], -1, HEAD_DIM)
        return jnp.concatenate([a, a], axis=-1).reshape(a.shape[0], -1)

    kiwi = jnp.concatenate([ki, wi, jnp.zeros((w.shape[0], LANES - IDX_DIM - IDX_HEADS), w.dtype)], axis=1)
    return {
        'a': jnp.concatenate([bg, cg, xin, ga], axis=1).astype(BF16),
        'q': q.astype(BF16), 'gb': gb.astype(BF16), 'kk': dup(k).astype(BF16), 'vv': dup(v).astype(BF16),
        'qi': qi.astype(BF16), 'ki': jnp.concatenate([ki, ki, kiwi], axis=1).astype(BF16),
    }


def _undup(a):
    b, t, _ = a.shape
    return a.reshape(b, t, N_KV_HEADS, LANES)[..., :HEAD_DIM]


def _dup_cache(a):
    a = a.astype(BF16)
    return jnp.concatenate([a, a], axis=-1).reshape(a.shape[0], a.shape[1], -1)


def _pad_keys(a, lpad):
    return jnp.pad(a, ((0, 0), (0, lpad - a.shape[1]), (0, 0)))


def _row_tile(n, cap):
    tm = min(n, cap)
    assert n % tm == 0
    return tm


def kernel(x_prompt, x_sample, cache_k, cache_v, cache_kidx, state_conv, w_in, conv_w, w_out, rel_bias,
           ln1_g, ln1_b, ln2_g, ln2_b, ffn_w_gate, ffn_w_up, ffn_w_down, router_w, router_b,
           moe_w_gate, moe_w_up, moe_w_down):
    depth = w_in.shape[0]
    alpha = (2 * depth) ** 0.25
    bp, tp, _ = x_prompt.shape
    bs, ts, _ = x_sample.shape
    past = cache_k.shape[2]
    nq_p = LANES
    assert tp % nq_p == 0 and past % LANES == 0 and ts <= LANES and ts % SUBLANES == 0
    topk_p = min(TOPK_MAX, tp // 4)
    topk_s = min(TOPK_MAX, (past + ts) // 4)
    lpad_s = -(-(past + ts) // LANES) * LANES

    bt_p = _bias_tiles(rel_bias, nq_p)
    bt_s = _bias_tiles(rel_bias, ts)
    vec = lambda a: a.reshape(1, -1)

    def layer(l, x, prev8, past_kv):
        b, t, _ = x.shape
        w = _split_w_in(w_in[l])
        (conv, q, sgb, kkf, kkb, vvf, vvb, qi, kiki, kiwi, cstate) = _inproj(
            x, prev8, w, conv_w[l], _row_tile(t, 512))
        if past_kv is None:
            merged = _attention(q, qi, kiwi, conv, sgb, kkb, vvb, kiki, bt_p,
                                nq=nq_p, q_off=0, kv_len=t, topk=topk_p)
        else:
            ck, cv, cki = past_kv
            kk_all = _pad_keys(jnp.concatenate([_dup_cache(ck), kkb], axis=1), lpad_s)
            vv_all = _pad_keys(jnp.concatenate([_dup_cache(cv), vvb], axis=1), lpad_s)
            cki = cki.astype(BF16)
            kiki_all = _pad_keys(jnp.concatenate([jnp.concatenate([cki, cki], axis=-1), kiki], axis=1), lpad_s)
            merged = _attention(q, qi, kiwi, conv, sgb, kk_all, vv_all, kiki_all, bt_s,
                                nq=t, q_off=past, kv_len=past + t, topk=topk_s)
        n = b * t
        x1 = _outproj_ln(merged.reshape(n, D_MODEL), x.reshape(n, D_MODEL), w_out[l].astype(BF16),
                         vec(ln1_g[l]), vec(ln1_b[l]), alpha, _row_tile(n, 512))
        j = l // 2
        if l % 2 == 0:
            x2 = _ffn_ln(x1, ffn_w_gate[j].astype(BF16), ffn_w_up[j].astype(BF16), ffn_w_down[j].astype(BF16),
                         vec(ln2_g[l]), vec(ln2_b[l]), alpha, _row_tile(n, 512))
        else:
            rw = jnp.pad(router_w[j], ((0, 0), (0, LANES - N_EXPERTS)))
            rb = jnp.pad(router_b[j], (0, LANES - N_EXPERTS)).reshape(1, LANES)
            x2 = _moe_ln(x1, rw, rb, moe_w_gate[j].astype(BF16), moe_w_up[j].astype(BF16),
                         moe_w_down[j].astype(BF16), vec(ln2_g[l]), vec(ln2_b[l]), alpha, _row_tile(n, 512))
        return (x2.reshape(b, t, D_MODEL), _undup(kkf), _undup(vvf), kiwi[..., :IDX_DIM],
                cstate[:, SUBLANES - (CONV_W - 1):, :])

    xp, xs = x_prompt, x_sample
    outs_p, outs_s = [], []
    zero_prev = jnp.zeros((bp, SUBLANES, D_MODEL), F32)
    for l in range(depth):
        xp, *rest = layer(l, xp, zero_prev, None)
        outs_p.append(rest)
        prev8 = jnp.pad(state_conv[l], ((0, 0), (SUBLANES - (CONV_W - 1), 0), (0, 0)))
        xs, *rest = layer(l, xs, prev8, (cache_k[l], cache_v[l], cache_kidx[l]))
        outs_s.append(rest)
    stack = lambda outs, i: jnp.stack([o[i] for o in outs])
    return (xp, xs,
            stack(outs_p, 0), stack(outs_p, 1), stack(outs_p, 2), stack(outs_p, 3),
            stack(outs_s, 0), stack(outs_s, 1), stack(outs_s, 2), stack(outs_s, 3))
```

```python
import functools

import jax
import jax.numpy as jnp
from jax import lax
from jax.experimental import pallas as pl
from jax.experimental.pallas import tpu as pltpu

D_MODEL = 1024
CHUNK = 64
CHUNK_SHIFT = 6
CONV_W = 3
N_HEADS = 16
HEAD_DIM = 64
N_KV_HEADS = 4
GROUP = N_HEADS // N_KV_HEADS
IDX_HEADS = 8
IDX_DIM = 64
TOPK_MAX = 256
NUM_BUCKETS = 32
MAX_DISTANCE = 128
D_FF = 2816
N_EXPERTS = 8
D_FF_EXPERT = 1408
LN_EPS = 1e-5

LANES = 128
SUBLANES = 8
N_PAIRS = N_HEADS // 2
KB = 512
TILES = KB // LANES
N_BIAS_TILES = 4
PAIRS_PER_LOOP = 8
NEG = -1e30
LOG2E = 1.4426950408889634
INT_MIN = -2 ** 31
VMEM_LIMIT = 56 * 1024 * 1024

F32 = jnp.float32
BF16 = jnp.bfloat16
I32 = jnp.int32

assert CHUNK == 1 << CHUNK_SHIFT and 2 * HEAD_DIM == LANES and 2 * IDX_DIM == LANES


def _dot(a, b):
    return jnp.dot(a, b, preferred_element_type=F32)


def _dot_nt(a, b):
    return lax.dot_general(a, b, (((1,), (1,)), ((), ())), preferred_element_type=F32)


def _layer_norm(y, g, b):
    mu = jnp.mean(y, axis=-1, keepdims=True)
    d = y - mu
    var = jnp.mean(d * d, axis=-1, keepdims=True)
    return d * lax.rsqrt(var + LN_EPS) * g + b


def _tree(op, xs):
    xs = list(xs)
    while len(xs) > 1:
        xs = [op(xs[i], xs[i + 1]) for i in range(0, len(xs) - 1, 2)] + ([xs[-1]] if len(xs) % 2 else [])
    return xs[0]


def _fold8(x):
    return _tree(jnp.add, [x[i:i + SUBLANES] for i in range(0, x.shape[0], SUBLANES)])


def _inproj_kernel(x_ref, prev_ref, wa_ref, wq_ref, wgb_ref, wkk_ref, wvv_ref, wqi_ref, wki_ref, cw_ref,
                   conv_ref, q_ref, sgb_ref, kkf_ref, kkb_ref, vvf_ref, vvb_ref, qi_ref, kiki_ref, kiwi_ref,
                   cstate_ref, uext_ref, *, tm, transpose_v):
    @pl.when(pl.program_id(1) == 0)
    def _():
        uext_ref[0:SUBLANES, :] = prev_ref[0]

    xb = x_ref[0].astype(BF16)
    cc = 256
    for c in range(0, D_MODEL, cc):
        bg = _dot(xb, wa_ref[:, c:c + cc])
        cg = _dot(xb, wa_ref[:, D_MODEL + c:D_MODEL + c + cc])
        xin = _dot(xb, wa_ref[:, 2 * D_MODEL + c:2 * D_MODEL + c + cc])
        ga = _dot(xb, wa_ref[:, 3 * D_MODEL + c:3 * D_MODEL + c + cc])
        u = cg * xin
        uext_ref[SUBLANES:SUBLANES + tm, c:c + cc] = u
        um1 = uext_ref[SUBLANES - 1:SUBLANES - 1 + tm, c:c + cc]
        um2 = uext_ref[SUBLANES - 2:SUBLANES - 2 + tm, c:c + cc]
        y = um2 * cw_ref[0:1, c:c + cc] + um1 * cw_ref[1:2, c:c + cc] + u * cw_ref[2:3, c:c + cc]
        conv_ref[0, :, c:c + cc] = jax.nn.sigmoid(ga) * (bg * y)
    tail = uext_ref[tm:tm + SUBLANES, :]
    cstate_ref[0] = tail
    uext_ref[0:SUBLANES, :] = tail

    q_ref[0] = (_dot(xb, wq_ref[...]) * (HEAD_DIM ** -0.5 * LOG2E)).astype(BF16)
    sgb_ref[0] = jax.nn.sigmoid(_dot(xb, wgb_ref[...]))
    kk = _dot(xb, wkk_ref[...])
    kkf_ref[0] = kk
    kkb_ref[0] = kk.astype(BF16)
    vv = _dot(xb, wvv_ref[...])
    vvf_ref[0] = vv
    if transpose_v:
        vvb_ref[0, 0] = vv.T.astype(BF16)
    else:
        vvb_ref[0] = vv.astype(BF16)
    qi_ref[0] = _dot(xb, wqi_ref[...]).astype(BF16)
    kw = _dot(xb, wki_ref[...])
    kiki_ref[0] = kw[:, 0:LANES].astype(BF16)
    kiwi_ref[0] = kw[:, LANES:2 * LANES]


def _inproj(x, prev8, w, conv_w, tm, transpose_v):
    b, t, _ = x.shape
    grid = (b, t // tm)
    row = lambda n: pl.BlockSpec((1, tm, n), lambda i, j: (i, j, 0))
    full = lambda a: pl.BlockSpec(a.shape, lambda i, j: (0,) * a.ndim, pipeline_mode=pl.Buffered(1))
    if transpose_v:
        assert tm == KB
        vvb_shape = jax.ShapeDtypeStruct((b, t // KB, 4 * LANES, KB), BF16)
        vvb_spec = pl.BlockSpec((1, 1, 4 * LANES, KB), lambda i, j: (i, j, 0, 0))
    else:
        vvb_shape = jax.ShapeDtypeStruct((b, t, 4 * LANES), BF16)
        vvb_spec = row(4 * LANES)
    out_shapes = (
        jax.ShapeDtypeStruct((b, t, D_MODEL), F32),
        jax.ShapeDtypeStruct((b, t, D_MODEL), BF16),
        jax.ShapeDtypeStruct((b, t, D_MODEL), F32),
        jax.ShapeDtypeStruct((b, t, 4 * LANES), F32),
        jax.ShapeDtypeStruct((b, t, 4 * LANES), BF16),
        jax.ShapeDtypeStruct((b, t, 4 * LANES), F32),
        vvb_shape,
        jax.ShapeDtypeStruct((b, t, IDX_HEADS * IDX_DIM), BF16),
        jax.ShapeDtypeStruct((b, t, LANES), BF16),
        jax.ShapeDtypeStruct((b, t, LANES), F32),
        jax.ShapeDtypeStruct((b, SUBLANES, D_MODEL), F32),
    )
    out_specs = (row(D_MODEL), row(D_MODEL), row(D_MODEL), row(4 * LANES), row(4 * LANES), row(4 * LANES),
                 vvb_spec, row(IDX_HEADS * IDX_DIM), row(LANES), row(LANES),
                 pl.BlockSpec((1, SUBLANES, D_MODEL), lambda i, j: (i, 0, 0)))
    in_specs = [row(D_MODEL), pl.BlockSpec((1, SUBLANES, D_MODEL), lambda i, j: (i, 0, 0)),
                full(w['a']), full(w['q']), full(w['gb']), full(w['kk']), full(w['vv']), full(w['qi']),
                full(w['ki']), full(conv_w)]
    return pl.pallas_call(
        functools.partial(_inproj_kernel, tm=tm, transpose_v=transpose_v),
        grid=grid, in_specs=in_specs, out_specs=out_specs, out_shape=out_shapes,
        scratch_shapes=[pltpu.VMEM((tm + SUBLANES, D_MODEL), F32)],
        compiler_params=pltpu.CompilerParams(dimension_semantics=("arbitrary", "arbitrary"),
                                             vmem_limit_bytes=VMEM_LIMIT),
        name="inproj_conv",
    )(x, prev8, w['a'], w['q'], w['gb'], w['kk'], w['vv'], w['qi'], w['ki'], conv_w)


def _bias_tiles_kernel(tbl_ref, out_ref, *, nq):
    d = pl.program_id(0) - 2
    krow = lax.broadcasted_iota(I32, (LANES, nq), 0)
    qcol = lax.broadcasted_iota(I32, (LANES, nq), 1)
    rel = d * LANES + krow - qcol
    n = jnp.abs(rel)
    nb = NUM_BUCKETS // 2
    max_exact = nb // 2
    n2 = n * n
    large = jnp.full((LANES, nq), max_exact, I32)
    for j in range(1, nb - max_exact):
        large = large + (n2 >= (max_exact * max_exact) * (1 << j)).astype(I32)
    bucket = jnp.where(n < max_exact, n, large) + jnp.where(rel > 0, nb, 0)
    far = pl.program_id(0) == 0
    for h in range(N_HEADS):
        base = tbl_ref[nb - 1, h]
        acc = jnp.zeros((LANES, nq), F32)
        for bk in range(NUM_BUCKETS):
            acc = jnp.where(bucket == bk, (tbl_ref[bk, h] - base) * LOG2E, acc)
        out_ref[0, h] = jnp.where(far, 0.0, acc)


def _bias_tiles(rel_bias, nq):
    assert MAX_DISTANCE == 128 and NUM_BUCKETS == 32
    return pl.pallas_call(
        functools.partial(_bias_tiles_kernel, nq=nq),
        grid=(N_BIAS_TILES,),
        in_specs=[pl.BlockSpec(memory_space=pltpu.SMEM)],
        out_specs=pl.BlockSpec((1, N_HEADS, LANES, nq), lambda i: (i, 0, 0, 0)),
        out_shape=jax.ShapeDtypeStruct((N_BIAS_TILES, N_HEADS, LANES, nq), F32),
        name="bias_tiles",
    )(rel_bias)


def _attn_kernel(q_ref, qi_ref, kiwi_ref, conv_ref, sgb_ref, kk_ref, vvt_ref, kiki_ref, bt_ref, out_ref,
                 key_ref, mask_ref, qis_ref, tri_ref, qs_ref, *, nq, q_off, kv_len, topk):
    q0 = q_off + pl.program_id(1) * nq
    nkb = (q0 + nq + KB - 1) // KB
    jd = q0 // LANES
    lane = lax.broadcasted_iota(I32, (nq, LANES), 1)
    low = lane < HEAD_DIM
    krow = lax.broadcasted_iota(I32, (LANES, nq), 0)
    rowlow = krow < HEAD_DIM
    qchunk = lax.shift_right_arithmetic(q0 + lax.broadcasted_iota(I32, (LANES, nq), 1), CHUNK_SHIFT)

    wit = kiwi_ref[0].T * ((IDX_DIM ** -0.5) * (IDX_HEADS ** -0.5))
    wis = [wit[IDX_DIM + h:IDX_DIM + h + 1, :] for h in range(IDX_HEADS)]
    for h in range(IDX_HEADS):
        pair = qi_ref[0, :, (h // 2) * LANES:(h // 2 + 1) * LANES].astype(F32)
        keep = low if h % 2 == 0 else jnp.logical_not(low)
        qis_ref[h] = jnp.where(keep, pair, 0.0).astype(BF16)

    def score_block(jb, carry):
        ks = pl.multiple_of(jb * KB, KB)
        ki = kiki_ref[0, pl.ds(ks, KB), :]
        sc = [jnp.zeros((LANES, nq), F32) for _ in range(TILES)]
        for h in range(IDX_HEADS):
            s = _dot_nt(ki, qis_ref[h])
            for c in range(TILES):
                sc[c] = sc[c] + jnp.maximum(s[c * LANES:(c + 1) * LANES], 0.0) * wis[h]
        for c in range(TILES):
            bits = lax.bitcast_convert_type(sc[c], I32)
            key = jnp.where(bits < 0, bits ^ 0x7FFFFFFF, bits)
            kpos = ks + c * LANES + krow
            adm = (lax.shift_right_arithmetic(kpos, CHUNK_SHIFT) <= qchunk) & (kpos < kv_len)
            key_ref[pl.ds(ks + c * LANES, LANES), :] = jnp.where(adm, key, INT_MIN)
        return carry

    lax.fori_loop(0, nkb, score_block, 0)

    def count(pred):
        def body(jb, acc):
            ks = pl.multiple_of(jb * KB, KB)
            parts = []
            for c in range(TILES):
                blk = key_ref[pl.ds(ks + c * LANES, LANES), :]
                parts.append(_fold8(pred(blk, ks + c * LANES).astype(I32)))
            return acc + _tree(jnp.add, parts)
        acc = lax.fori_loop(0, nkb, body, jnp.zeros((SUBLANES, nq), I32))
        return jnp.sum(acc.astype(F32), axis=0, keepdims=True).astype(I32)

    def thr_step(i, t):
        cand = t + lax.shift_left(jnp.int32(1), 31 - i)
        cnt = count(lambda blk, base: blk >= cand)
        return jnp.where(cnt >= topk, cand, t)

    thr = lax.fori_loop(0, 32, thr_step, jnp.full((1, nq), INT_MIN, I32))

    want = (topk - count(lambda blk, base: blk > thr)).astype(F32)
    live = thr > INT_MIN
    tri_ref[...] = (lax.broadcasted_iota(I32, (KB, KB), 0) > lax.broadcasted_iota(I32, (KB, KB), 1)
                    ).astype(F32).astype(BF16)

    def mask_block(jb, seen):
        ks = pl.multiple_of(jb * KB, KB)
        keys = [key_ref[pl.ds(ks + c * LANES, LANES), :] for c in range(TILES)]
        ties = [((k == thr) & live).astype(F32) for k in keys]
        before = _dot(tri_ref[...], jnp.concatenate(ties, axis=0).astype(BF16)) + seen
        for c in range(TILES):
            keep = (keys[c] > thr) | ((ties[c] > 0.0) & (before[c * LANES:(c + 1) * LANES] < want))
            mask_ref[pl.ds(ks + c * LANES, LANES), :] = jnp.where(keep, 0.0, NEG)
        return seen + jnp.sum(_fold8(_tree(jnp.add, ties)), axis=0, keepdims=True)

    lax.fori_loop(0, nkb, mask_block, jnp.zeros((1, nq), F32))

    def head_step(s, msks, tiles, head, m, l):
        ss = [s[c * LANES:(c + 1) * LANES] + msks[c] + bt_ref[tiles[c], head] for c in range(TILES)]
        m_new = jnp.maximum(m, jnp.max(_tree(jnp.maximum, ss), axis=0, keepdims=True))
        a = jnp.exp2(m - m_new)
        ps = [jnp.exp2(x - m_new) for x in ss]
        l_new = a * l + jnp.sum(_tree(jnp.add, ps), axis=0, keepdims=True)
        return m_new, l_new, a, jnp.concatenate(ps, axis=0).astype(BF16)

    for p0 in range(0, N_PAIRS, PAIRS_PER_LOOP):
        pairs = tuple(range(p0, p0 + PAIRS_PER_LOOP))
        for p in pairs:
            qp = q_ref[0, :, p * LANES:(p + 1) * LANES].astype(F32)
            qs_ref[2 * p] = jnp.where(low, qp, 0.0).astype(BF16)
            qs_ref[2 * p + 1] = jnp.where(low, 0.0, qp).astype(BF16)

        def attn_block(jb, carry):
            ks = pl.multiple_of(jb * KB, KB)
            msks = [mask_ref[pl.ds(ks + c * LANES, LANES), :] for c in range(TILES)]
            tiles = [jnp.clip(jb * TILES + c - jd + 2, 0, N_BIAS_TILES - 1) for c in range(TILES)]
            out = []
            for i, p in enumerate(pairs):
                g = (2 * p) // GROUP
                kblk = kk_ref[0, pl.ds(ks, KB), g * LANES:(g + 1) * LANES]
                vt = vvt_ref[0, jb, g * LANES:(g + 1) * LANES, :]
                ma, la, mb, lb, acc = carry[i]
                ma, la, aa, pa = head_step(_dot_nt(kblk, qs_ref[2 * p]), msks, tiles, 2 * p, ma, la)
                mb, lb, ab, pb = head_step(_dot_nt(kblk, qs_ref[2 * p + 1]), msks, tiles, 2 * p + 1, mb, lb)
                acc = jnp.where(rowlow, aa, ab) * acc + jnp.where(rowlow, _dot(vt, pa), _dot(vt, pb))
                out.append((ma, la, mb, lb, acc))
            return tuple(out)

        init = (jnp.full((1, nq), NEG, F32), jnp.zeros((1, nq), F32),
                jnp.full((1, nq), NEG, F32), jnp.zeros((1, nq), F32), jnp.zeros((LANES, nq), F32))
        res = lax.fori_loop(0, nkb, attn_block, (init,) * len(pairs))
        for i, p in enumerate(pairs):
            _, la, _, lb, acc = res[i]
            o = (acc / jnp.where(rowlow, la, lb)).T
            sl = slice(p * LANES, (p + 1) * LANES)
            out_ref[0, :, sl] = (conv_ref[0, :, sl] + sgb_ref[0, :, sl] * o).astype(BF16)


def _attention(q, qi, kiwi, conv, sgb, kk, vvt, kiki, bias_tiles, *, nq, q_off, kv_len, topk):
    b, t, _ = q.shape
    lpad = kk.shape[1]
    assert q_off % LANES == 0 and nq % LANES == 0 and t % nq == 0
    assert lpad % KB == 0 and lpad >= q_off + t and vvt.shape == (b, lpad // KB, 4 * LANES, KB)
    grid = (b, t // nq)
    row = lambda n: pl.BlockSpec((1, nq, n), lambda i, j: (i, j, 0))
    keys = lambda n: pl.BlockSpec((1, lpad, n), lambda i, j: (i, 0, 0))
    return pl.pallas_call(
        functools.partial(_attn_kernel, nq=nq, q_off=q_off, kv_len=kv_len, topk=topk),
        grid=grid,
        in_specs=[row(D_MODEL), row(IDX_HEADS * IDX_DIM), row(LANES), row(D_MODEL), row(D_MODEL),
                  keys(4 * LANES),
                  pl.BlockSpec((1, lpad // KB, 4 * LANES, KB), lambda i, j: (i, 0, 0, 0)),
                  keys(LANES),
                  pl.BlockSpec((N_BIAS_TILES, N_HEADS, LANES, nq), lambda i, j: (0, 0, 0, 0),
                               pipeline_mode=pl.Buffered(1))],
        out_specs=row(D_MODEL),
        out_shape=jax.ShapeDtypeStruct((b, t, D_MODEL), BF16),
        scratch_shapes=[pltpu.VMEM((lpad, nq), I32), pltpu.VMEM((lpad, nq), F32),
                        pltpu.VMEM((IDX_HEADS, nq, LANES), BF16), pltpu.VMEM((KB, KB), BF16),
                        pltpu.VMEM((N_HEADS, nq, LANES), BF16)],
        compiler_params=pltpu.CompilerParams(dimension_semantics=("arbitrary", "arbitrary"),
                                             vmem_limit_bytes=VMEM_LIMIT),
        name="dsa_attention",
    )(q, qi, kiwi, conv, sgb, kk, vvt, kiki, bias_tiles)


def _outproj_kernel(m_ref, x_ref, w_ref, g_ref, b_ref, out_ref, *, alpha):
    y = alpha * x_ref[...] + _dot(m_ref[...], w_ref[...])
    out_ref[...] = _layer_norm(y, g_ref[...], b_ref[...])


def _outproj_ln(merged, x, w_out, g, b, alpha, tm):
    n = x.shape[0]
    row = pl.BlockSpec((tm, D_MODEL), lambda i: (i, 0))
    vec = pl.BlockSpec((1, D_MODEL), lambda i: (0, 0))
    return pl.pallas_call(
        functools.partial(_outproj_kernel, alpha=alpha),
        grid=(n // tm,),
        in_specs=[row, row, pl.BlockSpec((D_MODEL, D_MODEL), lambda i: (0, 0)), vec, vec],
        out_specs=row,
        out_shape=jax.ShapeDtypeStruct((n, D_MODEL), F32),
        compiler_params=pltpu.CompilerParams(dimension_semantics=("arbitrary",), vmem_limit_bytes=VMEM_LIMIT),
        name="outproj_ln",
    )(merged, x, w_out, g, b)


def _ffn_kernel(x_ref, wg_ref, wu_ref, wd_ref, g_ref, b_ref, out_ref, *, alpha):
    x = x_ref[...]
    xb = x.astype(BF16)
    hg = _dot(xb, wg_ref[...])
    h = (hg * jax.nn.sigmoid(hg)) * _dot(xb, wu_ref[...])
    f = _dot(h.astype(BF16), wd_ref[...])
    out_ref[...] = _layer_norm(alpha * x + f, g_ref[...], b_ref[...])


def _ffn_ln(x, wg, wu, wd, g, b, alpha, tm):
    n = x.shape[0]
    row = pl.BlockSpec((tm, D_MODEL), lambda i: (i, 0))
    vec = pl.BlockSpec((1, D_MODEL), lambda i: (0, 0))
    full = lambda a: pl.BlockSpec(a.shape, lambda i: (0, 0), pipeline_mode=pl.Buffered(1))
    return pl.pallas_call(
        functools.partial(_ffn_kernel, alpha=alpha),
        grid=(n // tm,),
        in_specs=[row, full(wg), full(wu), full(wd), vec, vec],
        out_specs=row,
        out_shape=jax.ShapeDtypeStruct((n, D_MODEL), F32),
        compiler_params=pltpu.CompilerParams(dimension_semantics=("arbitrary",), vmem_limit_bytes=VMEM_LIMIT),
        name="ffn_ln",
    )(x, wg, wu, wd, g, b)


def _moe_kernel(x_ref, rw_ref, rb_ref, wg_ref, wu_ref, wd_ref, g_ref, b_ref, out_ref, comb_ref, acc_ref,
                *, alpha, tm):
    e = pl.program_id(1)
    lane = lax.broadcasted_iota(I32, (tm, LANES), 1)

    @pl.when(e == 0)
    def _():
        logits = jnp.dot(x_ref[...], rw_ref[...], preferred_element_type=F32,
                         precision=lax.Precision.HIGHEST) + rb_ref[...]
        logits = jnp.where(lane < N_EXPERTS, logits, -jnp.inf)
        lanef = lane.astype(F32)
        v1 = jnp.max(logits, axis=1, keepdims=True)
        i1 = jnp.min(jnp.where(logits == v1, lanef, float(LANES)), axis=1, keepdims=True)
        rest = jnp.where(lanef == i1, -jnp.inf, logits)
        v2 = jnp.max(rest, axis=1, keepdims=True)
        i2 = jnp.min(jnp.where(rest == v2, lanef, float(LANES)), axis=1, keepdims=True)
        e2 = jnp.exp(v2 - v1)
        den = 1.0 + e2
        comb_ref[...] = jnp.where(lanef == i1, 1.0 / den, 0.0) + jnp.where(lanef == i2, e2 / den, 0.0)
        acc_ref[...] = jnp.zeros((tm, D_MODEL), F32)

    xb = x_ref[...].astype(BF16)
    hg = _dot(xb, wg_ref[...])
    h = (hg * jax.nn.sigmoid(hg)) * _dot(xb, wu_ref[...])
    f = _dot(h.astype(BF16), wd_ref[...])
    ce = jnp.sum(jnp.where(lane == e, comb_ref[...], 0.0), axis=1, keepdims=True)
    acc_ref[...] += ce * f

    @pl.when(e == N_EXPERTS - 1)
    def _():
        out_ref[...] = _layer_norm(alpha * x_ref[...] + acc_ref[...], g_ref[...], b_ref[...])


def _moe_ln(x, rw, rb, wg, wu, wd, g, b, alpha, tm):
    n = x.shape[0]
    row = pl.BlockSpec((tm, D_MODEL), lambda i, e: (i, 0))
    vec = pl.BlockSpec((1, D_MODEL), lambda i, e: (0, 0))
    return pl.pallas_call(
        functools.partial(_moe_kernel, alpha=alpha, tm=tm),
        grid=(n // tm, N_EXPERTS),
        in_specs=[row, pl.BlockSpec((D_MODEL, LANES), lambda i, e: (0, 0)),
                  pl.BlockSpec((1, LANES), lambda i, e: (0, 0)),
                  pl.BlockSpec((None, D_MODEL, D_FF_EXPERT), lambda i, e: (e, 0, 0)),
                  pl.BlockSpec((None, D_MODEL, D_FF_EXPERT), lambda i, e: (e, 0, 0)),
                  pl.BlockSpec((None, D_FF_EXPERT, D_MODEL), lambda i, e: (e, 0, 0)),
                  vec, vec],
        out_specs=row,
        out_shape=jax.ShapeDtypeStruct((n, D_MODEL), F32),
        scratch_shapes=[pltpu.VMEM((tm, LANES), F32), pltpu.VMEM((tm, D_MODEL), F32)],
        compiler_params=pltpu.CompilerParams(dimension_semantics=("arbitrary", "arbitrary"),
                                             vmem_limit_bytes=VMEM_LIMIT),
        name="moe_ln",
    )(x, rw, rb, wg, wu, wd, g, b)


def _split_w_in(w):
    sizes = (D_MODEL, D_MODEL, D_MODEL, N_HEADS * HEAD_DIM, N_KV_HEADS * HEAD_DIM, N_KV_HEADS * HEAD_DIM,
             IDX_HEADS * IDX_DIM, IDX_DIM, IDX_HEADS, D_MODEL, D_MODEL)
    parts, start = [], 0
    for n in sizes:
        parts.append(w[:, start:start + n])
        start += n
    bg, cg, xin, q, k, v, qi, ki, wi, ga, gb = parts

    def dup(a):
        a = a.reshape(a.shape[0], -1, HEAD_DIM)
        return jnp.concatenate([a, a], axis=-1).reshape(a.shape[0], -1)

    kiwi = jnp.concatenate([ki, wi, jnp.zeros((w.shape[0], LANES - IDX_DIM - IDX_HEADS), w.dtype)], axis=1)
    return {
        'a': jnp.concatenate([bg, cg, xin, ga], axis=1).astype(BF16),
        'q': q.astype(BF16), 'gb': gb.astype(BF16), 'kk': dup(k).astype(BF16), 'vv': dup(v).astype(BF16),
        'qi': qi.astype(BF16), 'ki': jnp.concatenate([ki, ki, kiwi], axis=1).astype(BF16),
    }


def _undup(a):
    b, t, _ = a.shape
    return a.reshape(b, t, N_KV_HEADS, LANES)[..., :HEAD_DIM]


def _dup_cache(a):
    a = a.astype(BF16)
    return jnp.concatenate([a, a], axis=-1).reshape(a.shape[0], a.shape[1], -1)


def _pad_rows(a, n):
    return jnp.pad(a, ((0, 0), (0, n - a.shape[1]), (0, 0)))


def _row_tile(n, cap):
    tm = min(n, cap)
    assert n % tm == 0
    return tm


def kernel(x_prompt, x_sample, cache_k, cache_v, cache_kidx, state_conv, w_in, conv_w, w_out, rel_bias,
           ln1_g, ln1_b, ln2_g, ln2_b, ffn_w_gate, ffn_w_up, ffn_w_down, router_w, router_b,
           moe_w_gate, moe_w_up, moe_w_down):
    depth = w_in.shape[0]
    alpha = (2 * depth) ** 0.25
    bp, tp, _ = x_prompt.shape
    bs, ts, _ = x_sample.shape
    past = cache_k.shape[2]
    nq_p = 256 if tp % 256 == 0 else LANES
    nq_s = LANES
    assert tp % KB == 0 and past % LANES == 0 and ts <= nq_s and ts % SUBLANES == 0
    topk_p = min(TOPK_MAX, tp // 4)
    topk_s = min(TOPK_MAX, (past + ts) // 4)
    lpad_s = -(-(past + nq_s) // KB) * KB

    bt_p = _bias_tiles(rel_bias, nq_p)
    bt_s = _bias_tiles(rel_bias, nq_s)
    vec = lambda a: a.reshape(1, -1)

    def layer(l, x, prev8, past_kv):
        b, t, _ = x.shape
        w = _split_w_in(w_in[l])
        prompt = past_kv is None
        (conv, q, sgb, kkf, kkb, vvf, vvb, qi, kiki, kiwi, cstate) = _inproj(
            x, prev8, w, conv_w[l], _row_tile(t, KB), transpose_v=prompt)
        if prompt:
            merged = _attention(q, qi, kiwi, conv, sgb, kkb, vvb, kiki, bt_p,
                                nq=nq_p, q_off=0, kv_len=t, topk=topk_p)
        else:
            ck, cv, cki = past_kv
            kk_all = _pad_rows(jnp.concatenate([_dup_cache(ck), kkb], axis=1), lpad_s)
            vv_all = _pad_rows(jnp.concatenate([_dup_cache(cv), vvb], axis=1), lpad_s)
            vvt_all = jnp.swapaxes(vv_all.reshape(b, lpad_s // KB, KB, 4 * LANES), 2, 3)
            cki = cki.astype(BF16)
            kiki_all = _pad_rows(jnp.concatenate([jnp.concatenate([cki, cki], axis=-1), kiki], axis=1), lpad_s)
            pad = lambda a: _pad_rows(a, nq_s)
            merged = _attention(pad(q), pad(qi), pad(kiwi), pad(conv), pad(sgb), kk_all, vvt_all, kiki_all, bt_s,
                                nq=nq_s, q_off=past, kv_len=past + t, topk=topk_s)[:, :t]
        n = b * t
        x1 = _outproj_ln(merged.reshape(n, D_MODEL), x.reshape(n, D_MODEL), w_out[l].astype(BF16),
                         vec(ln1_g[l]), vec(ln1_b[l]), alpha, _row_tile(n, 512))
        j = l // 2
        if l % 2 == 0:
            x2 = _ffn_ln(x1, ffn_w_gate[j].astype(BF16), ffn_w_up[j].astype(BF16), ffn_w_down[j].astype(BF16),
                         vec(ln2_g[l]), vec(ln2_b[l]), alpha, _row_tile(n, 512))
        else:
            rw = jnp.pad(router_w[j], ((0, 0), (0, LANES - N_EXPERTS)))
            rb = jnp.pad(router_b[j], (0, LANES - N_EXPERTS)).reshape(1, LANES)
            x2 = _moe_ln(x1, rw, rb, moe_w_gate[j].astype(BF16), moe_w_up[j].astype(BF16),
                         moe_w_down[j].astype(BF16), vec(ln2_g[l]), vec(ln2_b[l]), alpha, _row_tile(n, 512))
        return (x2.reshape(b, t, D_MODEL), _undup(kkf), _undup(vvf), kiwi[..., :IDX_DIM],
                cstate[:, SUBLANES - (CONV_W - 1):, :])

    xp, xs = x_prompt, x_sample
    outs_p, outs_s = [], []
    zero_prev = jnp.zeros((bp, SUBLANES, D_MODEL), F32)
    for l in range(depth):
        xp, *rest = layer(l, xp, zero_prev, None)
        outs_p.append(rest)
        prev8 = jnp.pad(state_conv[l], ((0, 0), (SUBLANES - (CONV_W - 1), 0), (0, 0)))
        xs, *rest = layer(l, xs, prev8, (cache_k[l], cache_v[l], cache_kidx[l]))
        outs_s.append(rest)
    stack = lambda outs, i: jnp.stack([o[i] for o in outs])
    return (xp, xs,
            stack(outs_p, 0), stack(outs_p, 1), stack(outs_p, 2), stack(outs_p, 3),
            stack(outs_s, 0), stack(outs_s, 1), stack(outs_s, 2), stack(outs_s, 3))
```

```python
import functools

import jax
import jax.numpy as jnp
from jax import lax
from jax.experimental import pallas as pl
from jax.experimental.pallas import tpu as pltpu

D_MODEL = 1024
CHUNK = 64
CHUNK_SHIFT = 6
CONV_W = 3
N_HEADS = 16
HEAD_DIM = 64
N_KV_HEADS = 4
GROUP = N_HEADS // N_KV_HEADS
IDX_HEADS = 8
IDX_DIM = 64
TOPK_MAX = 256
NUM_BUCKETS = 32
MAX_DISTANCE = 128
D_FF = 2816
N_EXPERTS = 8
D_FF_EXPERT = 1408
LN_EPS = 1e-5

LANES = 128
SUBLANES = 8
N_PAIRS = N_HEADS // 2
KB = 512
TILES = KB // LANES
N_BIAS_TILES = 4
PAIRS_PER_LOOP = 8
LOOKAHEAD = 2
NEG = -1e30
LOG2E = 1.4426950408889634
INT_MIN = -2 ** 31
VMEM_LIMIT = 56 * 1024 * 1024

F32 = jnp.float32
BF16 = jnp.bfloat16
I32 = jnp.int32

assert CHUNK == 1 << CHUNK_SHIFT and 2 * HEAD_DIM == LANES and 2 * IDX_DIM == LANES


def _dot(a, b):
    return jnp.dot(a, b, preferred_element_type=F32)


def _dot_nt(a, b):
    return lax.dot_general(a, b, (((1,), (1,)), ((), ())), preferred_element_type=F32)


def _layer_norm(y, g, b):
    mu = jnp.mean(y, axis=-1, keepdims=True)
    d = y - mu
    var = jnp.mean(d * d, axis=-1, keepdims=True)
    return d * lax.rsqrt(var + LN_EPS) * g + b


def _tree(op, xs):
    xs = list(xs)
    while len(xs) > 1:
        xs = [op(xs[i], xs[i + 1]) for i in range(0, len(xs) - 1, 2)] + ([xs[-1]] if len(xs) % 2 else [])
    return xs[0]


def _fold8(x):
    return _tree(jnp.add, [x[i:i + SUBLANES] for i in range(0, x.shape[0], SUBLANES)])


def _inproj_kernel(x_ref, prev_ref, wa_ref, wq_ref, wgb_ref, wkk_ref, wvv_ref, wqi_ref, wki_ref, cw_ref,
                   conv_ref, q_ref, sgb_ref, kkf_ref, kkb_ref, vvf_ref, vvb_ref, qi_ref, kiki_ref, kiwi_ref,
                   cstate_ref, uext_ref, *, tm, transpose_v):
    @pl.when(pl.program_id(1) == 0)
    def _():
        uext_ref[0:SUBLANES, :] = prev_ref[0]

    xb = x_ref[0].astype(BF16)
    cc = 256
    for c in range(0, D_MODEL, cc):
        bg = _dot(xb, wa_ref[:, c:c + cc])
        cg = _dot(xb, wa_ref[:, D_MODEL + c:D_MODEL + c + cc])
        xin = _dot(xb, wa_ref[:, 2 * D_MODEL + c:2 * D_MODEL + c + cc])
        ga = _dot(xb, wa_ref[:, 3 * D_MODEL + c:3 * D_MODEL + c + cc])
        u = cg * xin
        uext_ref[SUBLANES:SUBLANES + tm, c:c + cc] = u
        um1 = uext_ref[SUBLANES - 1:SUBLANES - 1 + tm, c:c + cc]
        um2 = uext_ref[SUBLANES - 2:SUBLANES - 2 + tm, c:c + cc]
        y = um2 * cw_ref[0:1, c:c + cc] + um1 * cw_ref[1:2, c:c + cc] + u * cw_ref[2:3, c:c + cc]
        conv_ref[0, :, c:c + cc] = jax.nn.sigmoid(ga) * (bg * y)
    tail = uext_ref[tm:tm + SUBLANES, :]
    cstate_ref[0] = tail
    uext_ref[0:SUBLANES, :] = tail

    q_ref[0] = (_dot(xb, wq_ref[...]) * (HEAD_DIM ** -0.5 * LOG2E)).astype(BF16)
    sgb_ref[0] = jax.nn.sigmoid(_dot(xb, wgb_ref[...]))
    kk = _dot(xb, wkk_ref[...])
    kkf_ref[0] = kk
    kkb_ref[0] = kk.astype(BF16)
    vv = _dot(xb, wvv_ref[...])
    vvf_ref[0] = vv
    if transpose_v:
        vvb_ref[0, 0] = vv.T.astype(BF16)
    else:
        vvb_ref[0] = vv.astype(BF16)
    qi_ref[0] = _dot(xb, wqi_ref[...]).astype(BF16)
    kw = _dot(xb, wki_ref[...])
    kiki_ref[0] = kw[:, 0:LANES].astype(BF16)
    kiwi_ref[0] = kw[:, LANES:2 * LANES]


def _inproj(x, prev8, w, conv_w, tm, transpose_v):
    b, t, _ = x.shape
    grid = (b, t // tm)
    row = lambda n: pl.BlockSpec((1, tm, n), lambda i, j: (i, j, 0))
    full = lambda a: pl.BlockSpec(a.shape, lambda i, j: (0,) * a.ndim, pipeline_mode=pl.Buffered(1))
    if transpose_v:
        assert tm == KB
        vvb_shape = jax.ShapeDtypeStruct((b, t // KB, 4 * LANES, KB), BF16)
        vvb_spec = pl.BlockSpec((1, 1, 4 * LANES, KB), lambda i, j: (i, j, 0, 0))
    else:
        vvb_shape = jax.ShapeDtypeStruct((b, t, 4 * LANES), BF16)
        vvb_spec = row(4 * LANES)
    out_shapes = (
        jax.ShapeDtypeStruct((b, t, D_MODEL), F32),
        jax.ShapeDtypeStruct((b, t, D_MODEL), BF16),
        jax.ShapeDtypeStruct((b, t, D_MODEL), F32),
        jax.ShapeDtypeStruct((b, t, 4 * LANES), F32),
        jax.ShapeDtypeStruct((b, t, 4 * LANES), BF16),
        jax.ShapeDtypeStruct((b, t, 4 * LANES), F32),
        vvb_shape,
        jax.ShapeDtypeStruct((b, t, IDX_HEADS * IDX_DIM), BF16),
        jax.ShapeDtypeStruct((b, t, LANES), BF16),
        jax.ShapeDtypeStruct((b, t, LANES), F32),
        jax.ShapeDtypeStruct((b, SUBLANES, D_MODEL), F32),
    )
    out_specs = (row(D_MODEL), row(D_MODEL), row(D_MODEL), row(4 * LANES), row(4 * LANES), row(4 * LANES),
                 vvb_spec, row(IDX_HEADS * IDX_DIM), row(LANES), row(LANES),
                 pl.BlockSpec((1, SUBLANES, D_MODEL), lambda i, j: (i, 0, 0)))
    in_specs = [row(D_MODEL), pl.BlockSpec((1, SUBLANES, D_MODEL), lambda i, j: (i, 0, 0)),
                full(w['a']), full(w['q']), full(w['gb']), full(w['kk']), full(w['vv']), full(w['qi']),
                full(w['ki']), full(conv_w)]
    return pl.pallas_call(
        functools.partial(_inproj_kernel, tm=tm, transpose_v=transpose_v),
        grid=grid, in_specs=in_specs, out_specs=out_specs, out_shape=out_shapes,
        scratch_shapes=[pltpu.VMEM((tm + SUBLANES, D_MODEL), F32)],
        compiler_params=pltpu.CompilerParams(dimension_semantics=("arbitrary", "arbitrary"),
                                             vmem_limit_bytes=VMEM_LIMIT),
        name="inproj_conv",
    )(x, prev8, w['a'], w['q'], w['gb'], w['kk'], w['vv'], w['qi'], w['ki'], conv_w)


def _bias_tiles_kernel(tbl_ref, out_ref, *, nq):
    d = pl.program_id(0) - 2
    krow = lax.broadcasted_iota(I32, (LANES, nq), 0)
    qcol = lax.broadcasted_iota(I32, (LANES, nq), 1)
    rel = d * LANES + krow - qcol
    n = jnp.abs(rel)
    nb = NUM_BUCKETS // 2
    max_exact = nb // 2
    n2 = n * n
    large = jnp.full((LANES, nq), max_exact, I32)
    for j in range(1, nb - max_exact):
        large = large + (n2 >= (max_exact * max_exact) * (1 << j)).astype(I32)
    bucket = jnp.where(n < max_exact, n, large) + jnp.where(rel > 0, nb, 0)
    far = pl.program_id(0) == 0
    for h in range(N_HEADS):
        base = tbl_ref[nb - 1, h]
        acc = jnp.zeros((LANES, nq), F32)
        for bk in range(NUM_BUCKETS):
            acc = jnp.where(bucket == bk, (tbl_ref[bk, h] - base) * LOG2E, acc)
        out_ref[0, h] = jnp.where(far, 0.0, acc)


def _bias_tiles(rel_bias, nq):
    assert MAX_DISTANCE == 128 and NUM_BUCKETS == 32
    return pl.pallas_call(
        functools.partial(_bias_tiles_kernel, nq=nq),
        grid=(N_BIAS_TILES,),
        in_specs=[pl.BlockSpec(memory_space=pltpu.SMEM)],
        out_specs=pl.BlockSpec((1, N_HEADS, LANES, nq), lambda i: (i, 0, 0, 0)),
        out_shape=jax.ShapeDtypeStruct((N_BIAS_TILES, N_HEADS, LANES, nq), F32),
        name="bias_tiles",
    )(rel_bias)


def _attn_kernel(q_ref, qi_ref, kiwi_ref, conv_ref, sgb_ref, kk_ref, vvt_ref, kiki_ref, bt_ref, out_ref,
                 key_ref, mask_ref, qis_ref, tri_ref, qs_ref, *, nq, q_off, kv_len, topk):
    q0 = q_off + pl.program_id(1) * nq
    nkb = (q0 + nq + KB - 1) // KB
    jd = q0 // LANES
    lane = lax.broadcasted_iota(I32, (nq, LANES), 1)
    low = lane < HEAD_DIM
    krow = lax.broadcasted_iota(I32, (LANES, nq), 0)
    rowlow = krow < HEAD_DIM
    qchunk = lax.shift_right_arithmetic(q0 + lax.broadcasted_iota(I32, (LANES, nq), 1), CHUNK_SHIFT)

    wit = kiwi_ref[0].T * ((IDX_DIM ** -0.5) * (IDX_HEADS ** -0.5))
    wis = [wit[IDX_DIM + h:IDX_DIM + h + 1, :] for h in range(IDX_HEADS)]
    for h in range(IDX_HEADS):
        pair = qi_ref[0, :, (h // 2) * LANES:(h // 2 + 1) * LANES].astype(F32)
        keep = low if h % 2 == 0 else jnp.logical_not(low)
        qis_ref[h] = jnp.where(keep, pair, 0.0).astype(BF16)

    def score_block(jb, carry):
        ks = pl.multiple_of(jb * KB, KB)
        ki = kiki_ref[0, pl.ds(ks, KB), :]
        sc = [jnp.zeros((LANES, nq), F32) for _ in range(TILES)]
        for h in range(IDX_HEADS):
            s = _dot_nt(ki, qis_ref[h])
            for c in range(TILES):
                sc[c] = sc[c] + jnp.maximum(s[c * LANES:(c + 1) * LANES], 0.0) * wis[h]
        for c in range(TILES):
            bits = lax.bitcast_convert_type(sc[c], I32)
            key = jnp.where(bits < 0, bits ^ 0x7FFFFFFF, bits)
            kpos = ks + c * LANES + krow
            adm = (lax.shift_right_arithmetic(kpos, CHUNK_SHIFT) <= qchunk) & (kpos < kv_len)
            key_ref[pl.ds(ks + c * LANES, LANES), :] = jnp.where(adm, key, INT_MIN)
        return carry

    lax.fori_loop(0, nkb, score_block, 0)

    def count(pred):
        def body(jb, acc):
            ks = pl.multiple_of(jb * KB, KB)
            parts = []
            for c in range(TILES):
                blk = key_ref[pl.ds(ks + c * LANES, LANES), :]
                parts.append(_fold8(pred(blk, ks + c * LANES).astype(I32)))
            return acc + _tree(jnp.add, parts)
        acc = lax.fori_loop(0, nkb, body, jnp.zeros((SUBLANES, nq), I32))
        return jnp.sum(acc.astype(F32), axis=0, keepdims=True).astype(I32)

    def thr_step(i, t):
        cand = t + lax.shift_left(jnp.int32(1), 31 - i)
        cnt = count(lambda blk, base: blk >= cand)
        return jnp.where(cnt >= topk, cand, t)

    thr = lax.fori_loop(0, 32, thr_step, jnp.full((1, nq), INT_MIN, I32))

    want = (topk - count(lambda blk, base: blk > thr)).astype(F32)
    live = thr > INT_MIN
    tri_ref[...] = (lax.broadcasted_iota(I32, (KB, KB), 0) > lax.broadcasted_iota(I32, (KB, KB), 1)
                    ).astype(F32).astype(BF16)

    def mask_block(jb, seen):
        ks = pl.multiple_of(jb * KB, KB)
        keys = [key_ref[pl.ds(ks + c * LANES, LANES), :] for c in range(TILES)]
        ties = [((k == thr) & live).astype(F32) for k in keys]
        before = _dot(tri_ref[...], jnp.concatenate(ties, axis=0).astype(BF16)) + seen
        for c in range(TILES):
            keep = (keys[c] > thr) | ((ties[c] > 0.0) & (before[c * LANES:(c + 1) * LANES] < want))
            mask_ref[pl.ds(ks + c * LANES, LANES), :] = jnp.where(keep, 0.0, NEG)
        return seen + jnp.sum(_fold8(_tree(jnp.add, ties)), axis=0, keepdims=True)

    lax.fori_loop(0, nkb, mask_block, jnp.zeros((1, nq), F32))

    def head_step(s, msks, biases, m):
        ss = [s[c * LANES:(c + 1) * LANES] + msks[c] for c in range(TILES)]
        if biases is not None:
            ss = [x + b for x, b in zip(ss, biases)]
        m_new = jnp.maximum(m, jnp.max(_tree(jnp.maximum, ss), axis=0, keepdims=True))
        return m_new, jnp.exp2(m - m_new), jnp.concatenate([jnp.exp2(x - m_new) for x in ss], axis=0).astype(BF16)

    vrow_low = lax.broadcasted_iota(I32, (LANES, KB), 0) < HEAD_DIM

    for p0 in range(0, N_PAIRS, PAIRS_PER_LOOP):
        pairs = tuple(range(p0, p0 + PAIRS_PER_LOOP))
        for p in pairs:
            qp = q_ref[0, :, p * LANES:(p + 1) * LANES].astype(F32)
            qs_ref[2 * p] = jnp.where(low, qp, 0.0).astype(BF16)
            qs_ref[2 * p + 1] = jnp.where(low, 0.0, qp).astype(BF16)

        def attn_block(jb, carry, with_bias):
            ks = pl.multiple_of(jb * KB, KB)
            msks = [mask_ref[pl.ds(ks + c * LANES, LANES), :] for c in range(TILES)]
            tiles = [jnp.clip(jb * TILES + c - jd + 2, 0, N_BIAS_TILES - 1) for c in range(TILES)]

            def logits(head):
                g = head // GROUP
                kblk = kk_ref[0, pl.ds(ks, KB), g * LANES:(g + 1) * LANES]
                return _dot_nt(kblk, qs_ref[head])

            heads = [2 * p + r for p in pairs for r in range(2)]
            pending = [logits(h) for h in heads[:LOOKAHEAD]]
            vts = {}
            out = []
            for i, p in enumerate(pairs):
                g = (2 * p) // GROUP
                if g not in vts:
                    vt = vvt_ref[0, jb, g * LANES:(g + 1) * LANES, :]
                    one = jnp.ones_like(vt)
                    vts[g] = (jnp.where(vrow_low, vt, one), jnp.where(vrow_low, one, vt))
                ma, la, mb, lb, acc = carry[i]
                stats = []
                for r, m in enumerate((ma, mb)):
                    nxt = 2 * i + r + LOOKAHEAD
                    if nxt < len(heads):
                        pending.append(logits(heads[nxt]))
                    biases = [bt_ref[tiles[c], 2 * p + r] for c in range(TILES)] if with_bias else None
                    stats.append(head_step(pending.pop(0), msks, biases, m))
                (ma, aa, pa), (mb, ab, pb) = stats
                pva = _dot(vts[g][0], pa)
                pvb = _dot(vts[g][1], pb)
                la = aa * la + pva[HEAD_DIM:HEAD_DIM + 1]
                lb = ab * lb + pvb[0:1]
                acc = jnp.where(rowlow, aa, ab) * acc + jnp.where(rowlow, pva, pvb)
                out.append((ma, la, mb, lb, acc))
            return tuple(out)

        init = (jnp.full((1, nq), NEG, F32), jnp.zeros((1, nq), F32),
                jnp.full((1, nq), NEG, F32), jnp.zeros((1, nq), F32), jnp.zeros((LANES, nq), F32))
        nfar = jnp.clip((jd - 1) // TILES, 0, nkb)
        res = lax.fori_loop(0, nfar, functools.partial(attn_block, with_bias=False), (init,) * len(pairs))
        res = lax.fori_loop(nfar, nkb, functools.partial(attn_block, with_bias=True), res)
        for i, p in enumerate(pairs):
            _, la, _, lb, acc = res[i]
            o = (acc / jnp.where(rowlow, la, lb)).T
            sl = slice(p * LANES, (p + 1) * LANES)
            out_ref[0, :, sl] = (conv_ref[0, :, sl] + sgb_ref[0, :, sl] * o).astype(BF16)


def _attention(q, qi, kiwi, conv, sgb, kk, vvt, kiki, bias_tiles, *, nq, q_off, kv_len, topk):
    b, t, _ = q.shape
    lpad = kk.shape[1]
    assert q_off % LANES == 0 and nq % LANES == 0 and t % nq == 0
    assert lpad % KB == 0 and lpad >= q_off + t and vvt.shape == (b, lpad // KB, 4 * LANES, KB)
    grid = (b, t // nq)
    row = lambda n: pl.BlockSpec((1, nq, n), lambda i, j: (i, j, 0))
    keys = lambda n: pl.BlockSpec((1, lpad, n), lambda i, j: (i, 0, 0))
    return pl.pallas_call(
        functools.partial(_attn_kernel, nq=nq, q_off=q_off, kv_len=kv_len, topk=topk),
        grid=grid,
        in_specs=[row(D_MODEL), row(IDX_HEADS * IDX_DIM), row(LANES), row(D_MODEL), row(D_MODEL),
                  keys(4 * LANES),
                  pl.BlockSpec((1, lpad // KB, 4 * LANES, KB), lambda i, j: (i, 0, 0, 0)),
                  keys(LANES),
                  pl.BlockSpec((N_BIAS_TILES, N_HEADS, LANES, nq), lambda i, j: (0, 0, 0, 0),
                               pipeline_mode=pl.Buffered(1))],
        out_specs=row(D_MODEL),
        out_shape=jax.ShapeDtypeStruct((b, t, D_MODEL), BF16),
        scratch_shapes=[pltpu.VMEM((lpad, nq), I32), pltpu.VMEM((lpad, nq), F32),
                        pltpu.VMEM((IDX_HEADS, nq, LANES), BF16), pltpu.VMEM((KB, KB), BF16),
                        pltpu.VMEM((N_HEADS, nq, LANES), BF16)],
        compiler_params=pltpu.CompilerParams(dimension_semantics=("arbitrary", "arbitrary"),
                                             vmem_limit_bytes=VMEM_LIMIT),
        name="dsa_attention",
    )(q, qi, kiwi, conv, sgb, kk, vvt, kiki, bias_tiles)


def _outproj_kernel(m_ref, x_ref, w_ref, g_ref, b_ref, out_ref, *, alpha):
    y = alpha * x_ref[...] + _dot(m_ref[...], w_ref[...])
    out_ref[...] = _layer_norm(y, g_ref[...], b_ref[...])


def _outproj_ln(merged, x, w_out, g, b, alpha, tm):
    n = x.shape[0]
    row = pl.BlockSpec((tm, D_MODEL), lambda i: (i, 0))
    vec = pl.BlockSpec((1, D_MODEL), lambda i: (0, 0))
    return pl.pallas_call(
        functools.partial(_outproj_kernel, alpha=alpha),
        grid=(n // tm,),
        in_specs=[row, row, pl.BlockSpec((D_MODEL, D_MODEL), lambda i: (0, 0)), vec, vec],
        out_specs=row,
        out_shape=jax.ShapeDtypeStruct((n, D_MODEL), F32),
        compiler_params=pltpu.CompilerParams(dimension_semantics=("arbitrary",), vmem_limit_bytes=VMEM_LIMIT),
        name="outproj_ln",
    )(merged, x, w_out, g, b)


def _ffn_kernel(x_ref, wg_ref, wu_ref, wd_ref, g_ref, b_ref, out_ref, *, alpha):
    x = x_ref[...]
    xb = x.astype(BF16)
    hg = _dot(xb, wg_ref[...])
    h = (hg * jax.nn.sigmoid(hg)) * _dot(xb, wu_ref[...])
    f = _dot(h.astype(BF16), wd_ref[...])
    out_ref[...] = _layer_norm(alpha * x + f, g_ref[...], b_ref[...])


def _ffn_ln(x, wg, wu, wd, g, b, alpha, tm):
    n = x.shape[0]
    row = pl.BlockSpec((tm, D_MODEL), lambda i: (i, 0))
    vec = pl.BlockSpec((1, D_MODEL), lambda i: (0, 0))
    full = lambda a: pl.BlockSpec(a.shape, lambda i: (0, 0), pipeline_mode=pl.Buffered(1))
    return pl.pallas_call(
        functools.partial(_ffn_kernel, alpha=alpha),
        grid=(n // tm,),
        in_specs=[row, full(wg), full(wu), full(wd), vec, vec],
        out_specs=row,
        out_shape=jax.ShapeDtypeStruct((n, D_MODEL), F32),
        compiler_params=pltpu.CompilerParams(dimension_semantics=("arbitrary",), vmem_limit_bytes=VMEM_LIMIT),
        name="ffn_ln",
    )(x, wg, wu, wd, g, b)


def _moe_kernel(x_ref, rw_ref, rb_ref, wg_ref, wu_ref, wd_ref, g_ref, b_ref, out_ref, comb_ref, acc_ref,
                *, alpha, tm):
    e = pl.program_id(1)
    lane = lax.broadcasted_iota(I32, (tm, LANES), 1)

    @pl.when(e == 0)
    def _():
        logits = jnp.dot(x_ref[...], rw_ref[...], preferred_element_type=F32,
                         precision=lax.Precision.HIGHEST) + rb_ref[...]
        logits = jnp.where(lane < N_EXPERTS, logits, -jnp.inf)
        lanef = lane.astype(F32)
        v1 = jnp.max(logits, axis=1, keepdims=True)
        i1 = jnp.min(jnp.where(logits == v1, lanef, float(LANES)), axis=1, keepdims=True)
        rest = jnp.where(lanef == i1, -jnp.inf, logits)
        v2 = jnp.max(rest, axis=1, keepdims=True)
        i2 = jnp.min(jnp.where(rest == v2, lanef, float(LANES)), axis=1, keepdims=True)
        e2 = jnp.exp(v2 - v1)
        den = 1.0 + e2
        comb_ref[...] = jnp.where(lanef == i1, 1.0 / den, 0.0) + jnp.where(lanef == i2, e2 / den, 0.0)
        acc_ref[...] = jnp.zeros((tm, D_MODEL), F32)

    xb = x_ref[...].astype(BF16)
    hg = _dot(xb, wg_ref[...])
    h = (hg * jax.nn.sigmoid(hg)) * _dot(xb, wu_ref[...])
    f = _dot(h.astype(BF16), wd_ref[...])
    ce = jnp.sum(jnp.where(lane == e, comb_ref[...], 0.0), axis=1, keepdims=True)
    acc_ref[...] += ce * f

    @pl.when(e == N_EXPERTS - 1)
    def _():
        out_ref[...] = _layer_norm(alpha * x_ref[...] + acc_ref[...], g_ref[...], b_ref[...])


def _moe_ln(x, rw, rb, wg, wu, wd, g, b, alpha, tm):
    n = x.shape[0]
    row = pl.BlockSpec((tm, D_MODEL), lambda i, e: (i, 0))
    vec = pl.BlockSpec((1, D_MODEL), lambda i, e: (0, 0))
    return pl.pallas_call(
        functools.partial(_moe_kernel, alpha=alpha, tm=tm),
        grid=(n // tm, N_EXPERTS),
        in_specs=[row, pl.BlockSpec((D_MODEL, LANES), lambda i, e: (0, 0)),
                  pl.BlockSpec((1, LANES), lambda i, e: (0, 0)),
                  pl.BlockSpec((None, D_MODEL, D_FF_EXPERT), lambda i, e: (e, 0, 0)),
                  pl.BlockSpec((None, D_MODEL, D_FF_EXPERT), lambda i, e: (e, 0, 0)),
                  pl.BlockSpec((None, D_FF_EXPERT, D_MODEL), lambda i, e: (e, 0, 0)),
                  vec, vec],
        out_specs=row,
        out_shape=jax.ShapeDtypeStruct((n, D_MODEL), F32),
        scratch_shapes=[pltpu.VMEM((tm, LANES), F32), pltpu.VMEM((tm, D_MODEL), F32)],
        compiler_params=pltpu.CompilerParams(dimension_semantics=("arbitrary", "arbitrary"),
                                             vmem_limit_bytes=VMEM_LIMIT),
        name="moe_ln",
    )(x, rw, rb, wg, wu, wd, g, b)


def _split_w_in(w):
    sizes = (D_MODEL, D_MODEL, D_MODEL, N_HEADS * HEAD_DIM, N_KV_HEADS * HEAD_DIM, N_KV_HEADS * HEAD_DIM,
             IDX_HEADS * IDX_DIM, IDX_DIM, IDX_HEADS, D_MODEL, D_MODEL)
    parts, start = [], 0
    for n in sizes:
        parts.append(w[:, start:start + n])
        start += n
    bg, cg, xin, q, k, v, qi, ki, wi, ga, gb = parts

    def dup(a):
        a = a.reshape(a.shape[0], -1, HEAD_DIM)
        return jnp.concatenate([a, a], axis=-1).reshape(a.shape[0], -1)

    kiwi = jnp.concatenate([ki, wi, jnp.zeros((w.shape[0], LANES - IDX_DIM - IDX_HEADS), w.dtype)], axis=1)
    return {
        'a': jnp.concatenate([bg, cg, xin, ga], axis=1).astype(BF16),
        'q': q.astype(BF16), 'gb': gb.astype(BF16), 'kk': dup(k).astype(BF16), 'vv': dup(v).astype(BF16),
        'qi': qi.astype(BF16), 'ki': jnp.concatenate([ki, ki, kiwi], axis=1).astype(BF16),
    }


def _undup(a):
    b, t, _ = a.shape
    return a.reshape(b, t, N_KV_HEADS, LANES)[..., :HEAD_DIM]


def _dup_cache(a):
    a = a.astype(BF16)
    return jnp.concatenate([a, a], axis=-1).reshape(a.shape[0], a.shape[1], -1)


def _pad_rows(a, n):
    return jnp.pad(a, ((0, 0), (0, n - a.shape[1]), (0, 0)))


def _row_tile(n, cap):
    tm = min(n, cap)
    assert n % tm == 0
    return tm


def kernel(x_prompt, x_sample, cache_k, cache_v, cache_kidx, state_conv, w_in, conv_w, w_out, rel_bias,
           ln1_g, ln1_b, ln2_g, ln2_b, ffn_w_gate, ffn_w_up, ffn_w_down, router_w, router_b,
           moe_w_gate, moe_w_up, moe_w_down):
    depth = w_in.shape[0]
    alpha = (2 * depth) ** 0.25
    bp, tp, _ = x_prompt.shape
    bs, ts, _ = x_sample.shape
    past = cache_k.shape[2]
    nq_p = 256 if tp % 256 == 0 else LANES
    nq_s = LANES
    assert tp % KB == 0 and past % LANES == 0 and ts <= nq_s and ts % SUBLANES == 0
    topk_p = min(TOPK_MAX, tp // 4)
    topk_s = min(TOPK_MAX, (past + ts) // 4)
    lpad_s = -(-(past + nq_s) // KB) * KB

    bt_p = _bias_tiles(rel_bias, nq_p)
    bt_s = _bias_tiles(rel_bias, nq_s)
    vec = lambda a: a.reshape(1, -1)

    def layer(l, x, prev8, past_kv):
        b, t, _ = x.shape
        w = _split_w_in(w_in[l])
        prompt = past_kv is None
        (conv, q, sgb, kkf, kkb, vvf, vvb, qi, kiki, kiwi, cstate) = _inproj(
            x, prev8, w, conv_w[l], _row_tile(t, KB), transpose_v=prompt)
        if prompt:
            merged = _attention(q, qi, kiwi, conv, sgb, kkb, vvb, kiki, bt_p,
                                nq=nq_p, q_off=0, kv_len=t, topk=topk_p)
        else:
            ck, cv, cki = past_kv
            kk_all = _pad_rows(jnp.concatenate([_dup_cache(ck), kkb], axis=1), lpad_s)
            vv_all = _pad_rows(jnp.concatenate([_dup_cache(cv), vvb], axis=1), lpad_s)
            vvt_all = jnp.swapaxes(vv_all.reshape(b, lpad_s // KB, KB, 4 * LANES), 2, 3)
            cki = cki.astype(BF16)
            kiki_all = _pad_rows(jnp.concatenate([jnp.concatenate([cki, cki], axis=-1), kiki], axis=1), lpad_s)
            pad = lambda a: _pad_rows(a, nq_s)
            merged = _attention(pad(q), pad(qi), pad(kiwi), pad(conv), pad(sgb), kk_all, vvt_all, kiki_all, bt_s,
                                nq=nq_s, q_off=past, kv_len=past + t, topk=topk_s)[:, :t]
        n = b * t
        x1 = _outproj_ln(merged.reshape(n, D_MODEL), x.reshape(n, D_MODEL), w_out[l].astype(BF16),
                         vec(ln1_g[l]), vec(ln1_b[l]), alpha, _row_tile(n, 512))
        j = l // 2
        if l % 2 == 0:
            x2 = _ffn_ln(x1, ffn_w_gate[j].astype(BF16), ffn_w_up[j].astype(BF16), ffn_w_down[j].astype(BF16),
                         vec(ln2_g[l]), vec(ln2_b[l]), alpha, _row_tile(n, 512))
        else:
            rw = jnp.pad(router_w[j], ((0, 0), (0, LANES - N_EXPERTS)))
            rb = jnp.pad(router_b[j], (0, LANES - N_EXPERTS)).reshape(1, LANES)
            x2 = _moe_ln(x1, rw, rb, moe_w_gate[j].astype(BF16), moe_w_up[j].astype(BF16),
                         moe_w_down[j].astype(BF16), vec(ln2_g[l]), vec(ln2_b[l]), alpha, _row_tile(n, 512))
        return (x2.reshape(b, t, D_MODEL), _undup(kkf), _undup(vvf), kiwi[..., :IDX_DIM],
                cstate[:, SUBLANES - (CONV_W - 1):, :])

    xp, xs = x_prompt, x_sample
    outs_p, outs_s = [], []
    zero_prev = jnp.zeros((bp, SUBLANES, D_MODEL), F32)
    for l in range(depth):
        xp, *rest = layer(l, xp, zero_prev, None)
        outs_p.append(rest)
        prev8 = jnp.pad(state_conv[l], ((0, 0), (SUBLANES - (CONV_W - 1), 0), (0, 0)))
        xs, *rest = layer(l, xs, prev8, (cache_k[l], cache_v[l], cache_kidx[l]))
        outs_s.append(rest)
    stack = lambda outs, i: jnp.stack([o[i] for o in outs])
    return (xp, xs,
            stack(outs_p, 0), stack(outs_p, 1), stack(outs_p, 2), stack(outs_p, 3),
            stack(outs_s, 0), stack(outs_s, 1), stack(outs_s, 2), stack(outs_s, 3))
```

```python
import functools

import jax
import jax.numpy as jnp
from jax import lax
from jax.experimental import pallas as pl
from jax.experimental.pallas import tpu as pltpu

D_MODEL = 1024
CHUNK = 64
CHUNK_SHIFT = 6
CONV_W = 3
N_HEADS = 16
HEAD_DIM = 64
N_KV_HEADS = 4
GROUP = N_HEADS // N_KV_HEADS
IDX_HEADS = 8
IDX_DIM = 64
TOPK_MAX = 256
NUM_BUCKETS = 32
MAX_DISTANCE = 128
D_FF = 2816
N_EXPERTS = 8
D_FF_EXPERT = 1408
LN_EPS = 1e-5

LANES = 128
SUBLANES = 8
N_PAIRS = N_HEADS // 2
KB = 512
TILES = KB // LANES
N_BIAS_TILES = 4
PAIRS_PER_LOOP = 8
LOOKAHEAD = 3
NEG = -1e30
LOG2E = 1.4426950408889634
INT_MIN = -2 ** 31
VMEM_LIMIT = 56 * 1024 * 1024

F32 = jnp.float32
BF16 = jnp.bfloat16
I32 = jnp.int32

assert CHUNK == 1 << CHUNK_SHIFT and 2 * HEAD_DIM == LANES and 2 * IDX_DIM == LANES


def _dot(a, b):
    return jnp.dot(a, b, preferred_element_type=F32)


def _dot_nt(a, b):
    return lax.dot_general(a, b, (((1,), (1,)), ((), ())), preferred_element_type=F32)


def _layer_norm(y, g, b):
    mu = jnp.mean(y, axis=-1, keepdims=True)
    d = y - mu
    var = jnp.mean(d * d, axis=-1, keepdims=True)
    return d * lax.rsqrt(var + LN_EPS) * g + b


def _tree(op, xs):
    xs = list(xs)
    while len(xs) > 1:
        xs = [op(xs[i], xs[i + 1]) for i in range(0, len(xs) - 1, 2)] + ([xs[-1]] if len(xs) % 2 else [])
    return xs[0]


def _fold8(x):
    return _tree(jnp.add, [x[i:i + SUBLANES] for i in range(0, x.shape[0], SUBLANES)])


def _inproj_kernel(x_ref, prev_ref, wa_ref, wq_ref, wgb_ref, wkk_ref, wvv_ref, wqi_ref, wki_ref, cw_ref,
                   conv_ref, q_ref, sgb_ref, kkf_ref, kkb_ref, vvf_ref, vvb_ref, qi_ref, kiki_ref, kiwi_ref,
                   cstate_ref, uext_ref, *, tm, transpose_v):
    @pl.when(pl.program_id(1) == 0)
    def _():
        uext_ref[0:SUBLANES, :] = prev_ref[0]

    xb = x_ref[0].astype(BF16)
    cc = 256
    for c in range(0, D_MODEL, cc):
        bg = _dot(xb, wa_ref[:, c:c + cc])
        cg = _dot(xb, wa_ref[:, D_MODEL + c:D_MODEL + c + cc])
        xin = _dot(xb, wa_ref[:, 2 * D_MODEL + c:2 * D_MODEL + c + cc])
        ga = _dot(xb, wa_ref[:, 3 * D_MODEL + c:3 * D_MODEL + c + cc])
        u = cg * xin
        uext_ref[SUBLANES:SUBLANES + tm, c:c + cc] = u
        um1 = uext_ref[SUBLANES - 1:SUBLANES - 1 + tm, c:c + cc]
        um2 = uext_ref[SUBLANES - 2:SUBLANES - 2 + tm, c:c + cc]
        y = um2 * cw_ref[0:1, c:c + cc] + um1 * cw_ref[1:2, c:c + cc] + u * cw_ref[2:3, c:c + cc]
        conv_ref[0, :, c:c + cc] = jax.nn.sigmoid(ga) * (bg * y)
    tail = uext_ref[tm:tm + SUBLANES, :]
    cstate_ref[0] = tail
    uext_ref[0:SUBLANES, :] = tail

    q_ref[0] = (_dot(xb, wq_ref[...]) * (HEAD_DIM ** -0.5 * LOG2E)).astype(BF16)
    sgb_ref[0] = jax.nn.sigmoid(_dot(xb, wgb_ref[...]))
    def undup(a):
        low = lax.broadcasted_iota(I32, (tm, LANES), 1) < HEAD_DIM
        return jnp.concatenate(
            [jnp.where(low, a[:, 2 * j * LANES:(2 * j + 1) * LANES], a[:, (2 * j + 1) * LANES:(2 * j + 2) * LANES])
             for j in range(N_KV_HEADS // 2)], axis=1)

    kk = _dot(xb, wkk_ref[...])
    kkf_ref[0] = undup(kk)
    kkb_ref[0] = kk.astype(BF16)
    vv = _dot(xb, wvv_ref[...])
    vvf_ref[0] = undup(vv)
    if transpose_v:
        vvb_ref[0, 0] = vv.T.astype(BF16)
    else:
        vvb_ref[0] = vv.astype(BF16)
    qi_ref[0] = _dot(xb, wqi_ref[...]).astype(BF16)
    kw = _dot(xb, wki_ref[...])
    kiki_ref[0] = kw[:, 0:LANES].astype(BF16)
    kiwi_ref[0] = kw[:, LANES:2 * LANES]


def _inproj(x, prev8, w, conv_w, tm, transpose_v):
    b, t, _ = x.shape
    grid = (b, t // tm)
    row = lambda n: pl.BlockSpec((1, tm, n), lambda i, j: (i, j, 0))
    full = lambda a: pl.BlockSpec(a.shape, lambda i, j: (0,) * a.ndim, pipeline_mode=pl.Buffered(1))
    if transpose_v:
        assert tm == KB
        vvb_shape = jax.ShapeDtypeStruct((b, t // KB, 4 * LANES, KB), BF16)
        vvb_spec = pl.BlockSpec((1, 1, 4 * LANES, KB), lambda i, j: (i, j, 0, 0))
    else:
        vvb_shape = jax.ShapeDtypeStruct((b, t, 4 * LANES), BF16)
        vvb_spec = row(4 * LANES)
    out_shapes = (
        jax.ShapeDtypeStruct((b, t, D_MODEL), F32),
        jax.ShapeDtypeStruct((b, t, D_MODEL), BF16),
        jax.ShapeDtypeStruct((b, t, D_MODEL), F32),
        jax.ShapeDtypeStruct((b, t, N_KV_HEADS * HEAD_DIM), F32),
        jax.ShapeDtypeStruct((b, t, 4 * LANES), BF16),
        jax.ShapeDtypeStruct((b, t, N_KV_HEADS * HEAD_DIM), F32),
        vvb_shape,
        jax.ShapeDtypeStruct((b, t, IDX_HEADS * IDX_DIM), BF16),
        jax.ShapeDtypeStruct((b, t, LANES), BF16),
        jax.ShapeDtypeStruct((b, t, LANES), F32),
        jax.ShapeDtypeStruct((b, SUBLANES, D_MODEL), F32),
    )
    out_specs = (row(D_MODEL), row(D_MODEL), row(D_MODEL), row(N_KV_HEADS * HEAD_DIM), row(4 * LANES),
                 row(N_KV_HEADS * HEAD_DIM), vvb_spec, row(IDX_HEADS * IDX_DIM), row(LANES), row(LANES),
                 pl.BlockSpec((1, SUBLANES, D_MODEL), lambda i, j: (i, 0, 0)))
    in_specs = [row(D_MODEL), pl.BlockSpec((1, SUBLANES, D_MODEL), lambda i, j: (i, 0, 0)),
                full(w['a']), full(w['q']), full(w['gb']), full(w['kk']), full(w['vv']), full(w['qi']),
                full(w['ki']), full(conv_w)]
    return pl.pallas_call(
        functools.partial(_inproj_kernel, tm=tm, transpose_v=transpose_v),
        grid=grid, in_specs=in_specs, out_specs=out_specs, out_shape=out_shapes,
        scratch_shapes=[pltpu.VMEM((tm + SUBLANES, D_MODEL), F32)],
        compiler_params=pltpu.CompilerParams(dimension_semantics=("arbitrary", "arbitrary"),
                                             vmem_limit_bytes=VMEM_LIMIT),
        name="inproj_conv",
    )(x, prev8, w['a'], w['q'], w['gb'], w['kk'], w['vv'], w['qi'], w['ki'], conv_w)


def _bias_tiles_kernel(tbl_ref, out_ref, *, nq):
    d = pl.program_id(0) - 2
    krow = lax.broadcasted_iota(I32, (LANES, nq), 0)
    qcol = lax.broadcasted_iota(I32, (LANES, nq), 1)
    rel = d * LANES + krow - qcol
    n = jnp.abs(rel)
    nb = NUM_BUCKETS // 2
    max_exact = nb // 2
    n2 = n * n
    large = jnp.full((LANES, nq), max_exact, I32)
    for j in range(1, nb - max_exact):
        large = large + (n2 >= (max_exact * max_exact) * (1 << j)).astype(I32)
    bucket = jnp.where(n < max_exact, n, large) + jnp.where(rel > 0, nb, 0)
    far = pl.program_id(0) == 0
    for h in range(N_HEADS):
        base = tbl_ref[nb - 1, h]
        acc = jnp.zeros((LANES, nq), F32)
        for bk in range(NUM_BUCKETS):
            acc = jnp.where(bucket == bk, (tbl_ref[bk, h] - base) * LOG2E, acc)
        out_ref[0, h] = jnp.where(far, 0.0, acc)


def _bias_tiles(rel_bias, nq):
    assert MAX_DISTANCE == 128 and NUM_BUCKETS == 32
    return pl.pallas_call(
        functools.partial(_bias_tiles_kernel, nq=nq),
        grid=(N_BIAS_TILES,),
        in_specs=[pl.BlockSpec(memory_space=pltpu.SMEM)],
        out_specs=pl.BlockSpec((1, N_HEADS, LANES, nq), lambda i: (i, 0, 0, 0)),
        out_shape=jax.ShapeDtypeStruct((N_BIAS_TILES, N_HEADS, LANES, nq), F32),
        name="bias_tiles",
    )(rel_bias)


def _attn_kernel(q_ref, qi_ref, kiwi_ref, conv_ref, sgb_ref, kk_ref, vvt_ref, kiki_ref, bt_ref, out_ref,
                 key_ref, mask_ref, qis_ref, tri_ref, qs_ref, acc_ref, *, nq, q_off, kv_len, topk):
    q0 = q_off + pl.program_id(1) * nq
    nkb = (q0 + nq + KB - 1) // KB
    jd = q0 // LANES
    lane = lax.broadcasted_iota(I32, (nq, LANES), 1)
    low = lane < HEAD_DIM
    krow = lax.broadcasted_iota(I32, (LANES, nq), 0)
    rowlow = krow < HEAD_DIM
    qchunk = lax.shift_right_arithmetic(q0 + lax.broadcasted_iota(I32, (LANES, nq), 1), CHUNK_SHIFT)

    wit = kiwi_ref[0].T * ((IDX_DIM ** -0.5) * (IDX_HEADS ** -0.5))
    wis = [wit[IDX_DIM + h:IDX_DIM + h + 1, :] for h in range(IDX_HEADS)]
    for h in range(IDX_HEADS):
        pair = qi_ref[0, :, (h // 2) * LANES:(h // 2 + 1) * LANES].astype(F32)
        keep = low if h % 2 == 0 else jnp.logical_not(low)
        qis_ref[h] = jnp.where(keep, pair, 0.0).astype(BF16)

    def score_block(jb, carry):
        ks = pl.multiple_of(jb * KB, KB)
        ki = kiki_ref[0, pl.ds(ks, KB), :]
        sc = [jnp.zeros((LANES, nq), F32) for _ in range(TILES)]
        for h in range(IDX_HEADS):
            s = _dot_nt(ki, qis_ref[h])
            for c in range(TILES):
                sc[c] = sc[c] + jnp.maximum(s[c * LANES:(c + 1) * LANES], 0.0) * wis[h]
        for c in range(TILES):
            bits = lax.bitcast_convert_type(sc[c], I32)
            key = jnp.where(bits < 0, bits ^ 0x7FFFFFFF, bits)
            kpos = ks + c * LANES + krow
            adm = (lax.shift_right_arithmetic(kpos, CHUNK_SHIFT) <= qchunk) & (kpos < kv_len)
            key_ref[pl.ds(ks + c * LANES, LANES), :] = jnp.where(adm, key, INT_MIN)
        return carry

    lax.fori_loop(0, nkb, score_block, 0)

    def count(pred):
        def body(jb, acc):
            ks = pl.multiple_of(jb * KB, KB)
            parts = []
            for c in range(TILES):
                blk = key_ref[pl.ds(ks + c * LANES, LANES), :]
                parts.append(_fold8(pred(blk, ks + c * LANES).astype(I32)))
            return acc + _tree(jnp.add, parts)
        acc = lax.fori_loop(0, nkb, body, jnp.zeros((SUBLANES, nq), I32))
        return jnp.sum(acc.astype(F32), axis=0, keepdims=True).astype(I32)

    def thr_step(i, t):
        cand = t + lax.shift_left(jnp.int32(1), 31 - i)
        cnt = count(lambda blk, base: blk >= cand)
        return jnp.where(cnt >= topk, cand, t)

    thr = lax.fori_loop(0, 32, thr_step, jnp.full((1, nq), INT_MIN, I32))

    want = (topk - count(lambda blk, base: blk > thr)).astype(F32)
    live = thr > INT_MIN
    tri_ref[...] = (lax.broadcasted_iota(I32, (KB, KB), 0) > lax.broadcasted_iota(I32, (KB, KB), 1)
                    ).astype(F32).astype(BF16)

    def mask_block(jb, seen):
        ks = pl.multiple_of(jb * KB, KB)
        keys = [key_ref[pl.ds(ks + c * LANES, LANES), :] for c in range(TILES)]
        ties = [((k == thr) & live).astype(F32) for k in keys]
        before = _dot(tri_ref[...], jnp.concatenate(ties, axis=0).astype(BF16)) + seen
        for c in range(TILES):
            keep = (keys[c] > thr) | ((ties[c] > 0.0) & (before[c * LANES:(c + 1) * LANES] < want))
            mask_ref[pl.ds(ks + c * LANES, LANES), :] = jnp.where(keep, 0.0, NEG)
        return seen + jnp.sum(_fold8(_tree(jnp.add, ties)), axis=0, keepdims=True)

    lax.fori_loop(0, nkb, mask_block, jnp.zeros((1, nq), F32))

    def head_step(s, msks, biases, m):
        ss = [s[c * LANES:(c + 1) * LANES] + msks[c] for c in range(TILES)]
        if biases is not None:
            ss = [x + b for x, b in zip(ss, biases)]
        m_new = jnp.maximum(m, jnp.max(_tree(jnp.maximum, ss), axis=0, keepdims=True))
        return m_new, jnp.exp2(m - m_new), jnp.concatenate([jnp.exp2(x - m_new) for x in ss], axis=0).astype(BF16)

    vrow_low = lax.broadcasted_iota(I32, (LANES, KB), 0) < HEAD_DIM

    for p0 in range(0, N_PAIRS, PAIRS_PER_LOOP):
        pairs = tuple(range(p0, p0 + PAIRS_PER_LOOP))
        for p in pairs:
            qp = q_ref[0, :, p * LANES:(p + 1) * LANES].astype(F32)
            qs_ref[2 * p] = jnp.where(low, qp, 0.0).astype(BF16)
            qs_ref[2 * p + 1] = jnp.where(low, 0.0, qp).astype(BF16)

        def attn_block(jb, carry, with_bias):
            ks = pl.multiple_of(jb * KB, KB)
            msks = [mask_ref[pl.ds(ks + c * LANES, LANES), :] for c in range(TILES)]
            tiles = [jnp.clip(jb * TILES + c - jd + 2, 0, N_BIAS_TILES - 1) for c in range(TILES)]

            def logits(head):
                g = head // GROUP
                kblk = kk_ref[0, pl.ds(ks, KB), g * LANES:(g + 1) * LANES]
                return _dot_nt(kblk, qs_ref[head])

            heads = [2 * p + r for p in pairs for r in range(2)]
            pending = [logits(h) for h in heads[:LOOKAHEAD]]
            vts = {}
            out = []
            for i, p in enumerate(pairs):
                g = (2 * p) // GROUP
                if g not in vts:
                    vt = vvt_ref[0, jb, g * LANES:(g + 1) * LANES, :]
                    one = jnp.ones_like(vt)
                    vts[g] = (jnp.where(vrow_low, vt, one), jnp.where(vrow_low, one, vt))
                ma, la, mb, lb = carry[i]
                stats = []
                for r, m in enumerate((ma, mb)):
                    nxt = 2 * i + r + LOOKAHEAD
                    if nxt < len(heads):
                        pending.append(logits(heads[nxt]))
                    biases = [bt_ref[tiles[c], 2 * p + r] for c in range(TILES)] if with_bias else None
                    stats.append(head_step(pending.pop(0), msks, biases, m))
                (ma, aa, pa), (mb, ab, pb) = stats
                pva = _dot(vts[g][0], pa)
                pvb = _dot(vts[g][1], pb)
                la = aa * la + pva[HEAD_DIM:HEAD_DIM + 1]
                lb = ab * lb + pvb[0:1]
                acc_ref[p] = jnp.where(rowlow, aa, ab) * acc_ref[p] + jnp.where(rowlow, pva, pvb)
                out.append((ma, la, mb, lb))
            return tuple(out)

        for p in pairs:
            acc_ref[p] = jnp.zeros((LANES, nq), F32)
        init = (jnp.full((1, nq), NEG, F32), jnp.zeros((1, nq), F32),
                jnp.full((1, nq), NEG, F32), jnp.zeros((1, nq), F32))
        nfar = jnp.clip((jd - 1) // TILES, 0, nkb)
        res = lax.fori_loop(0, nfar, functools.partial(attn_block, with_bias=False), (init,) * len(pairs))
        res = lax.fori_loop(nfar, nkb, functools.partial(attn_block, with_bias=True), res)
        for i, p in enumerate(pairs):
            _, la, _, lb = res[i]
            o = (acc_ref[p] / jnp.where(rowlow, la, lb)).T
            sl = slice(p * LANES, (p + 1) * LANES)
            out_ref[0, :, sl] = (conv_ref[0, :, sl] + sgb_ref[0, :, sl] * o).astype(BF16)


def _attention(q, qi, kiwi, conv, sgb, kk, vvt, kiki, bias_tiles, *, nq, q_off, kv_len, topk):
    b, t, _ = q.shape
    lpad = kk.shape[1]
    assert q_off % LANES == 0 and nq % LANES == 0 and t % nq == 0
    assert lpad % KB == 0 and lpad >= q_off + t and vvt.shape == (b, lpad // KB, 4 * LANES, KB)
    grid = (b, t // nq)
    row = lambda n: pl.BlockSpec((1, nq, n), lambda i, j: (i, j, 0))
    keys = lambda n: pl.BlockSpec((1, lpad, n), lambda i, j: (i, 0, 0))
    return pl.pallas_call(
        functools.partial(_attn_kernel, nq=nq, q_off=q_off, kv_len=kv_len, topk=topk),
        grid=grid,
        in_specs=[row(D_MODEL), row(IDX_HEADS * IDX_DIM), row(LANES), row(D_MODEL), row(D_MODEL),
                  keys(4 * LANES),
                  pl.BlockSpec((1, lpad // KB, 4 * LANES, KB), lambda i, j: (i, 0, 0, 0)),
                  keys(LANES),
                  pl.BlockSpec((N_BIAS_TILES, N_HEADS, LANES, nq), lambda i, j: (0, 0, 0, 0),
                               pipeline_mode=pl.Buffered(1))],
        out_specs=row(D_MODEL),
        out_shape=jax.ShapeDtypeStruct((b, t, D_MODEL), BF16),
        scratch_shapes=[pltpu.VMEM((lpad, nq), I32), pltpu.VMEM((lpad, nq), F32),
                        pltpu.VMEM((IDX_HEADS, nq, LANES), BF16), pltpu.VMEM((KB, KB), BF16),
                        pltpu.VMEM((N_HEADS, nq, LANES), BF16), pltpu.VMEM((N_PAIRS, LANES, nq), F32)],
        compiler_params=pltpu.CompilerParams(dimension_semantics=("arbitrary", "arbitrary"),
                                             vmem_limit_bytes=VMEM_LIMIT),
        name="dsa_attention",
    )(q, qi, kiwi, conv, sgb, kk, vvt, kiki, bias_tiles)


def _outproj_kernel(m_ref, x_ref, w_ref, g_ref, b_ref, out_ref, *, alpha):
    y = alpha * x_ref[...] + _dot(m_ref[...], w_ref[...])
    out_ref[...] = _layer_norm(y, g_ref[...], b_ref[...])


def _outproj_ln(merged, x, w_out, g, b, alpha, tm):
    n = x.shape[0]
    row = pl.BlockSpec((tm, D_MODEL), lambda i: (i, 0))
    vec = pl.BlockSpec((1, D_MODEL), lambda i: (0, 0))
    return pl.pallas_call(
        functools.partial(_outproj_kernel, alpha=alpha),
        grid=(n // tm,),
        in_specs=[row, row, pl.BlockSpec((D_MODEL, D_MODEL), lambda i: (0, 0)), vec, vec],
        out_specs=row,
        out_shape=jax.ShapeDtypeStruct((n, D_MODEL), F32),
        compiler_params=pltpu.CompilerParams(dimension_semantics=("arbitrary",), vmem_limit_bytes=VMEM_LIMIT),
        name="outproj_ln",
    )(merged, x, w_out, g, b)


def _ffn_kernel(x_ref, wg_ref, wu_ref, wd_ref, g_ref, b_ref, out_ref, *, alpha):
    x = x_ref[...]
    xb = x.astype(BF16)
    hg = _dot(xb, wg_ref[...])
    h = (hg * jax.nn.sigmoid(hg)) * _dot(xb, wu_ref[...])
    f = _dot(h.astype(BF16), wd_ref[...])
    out_ref[...] = _layer_norm(alpha * x + f, g_ref[...], b_ref[...])


def _ffn_ln(x, wg, wu, wd, g, b, alpha, tm):
    n = x.shape[0]
    row = pl.BlockSpec((tm, D_MODEL), lambda i: (i, 0))
    vec = pl.BlockSpec((1, D_MODEL), lambda i: (0, 0))
    full = lambda a: pl.BlockSpec(a.shape, lambda i: (0, 0), pipeline_mode=pl.Buffered(1))
    return pl.pallas_call(
        functools.partial(_ffn_kernel, alpha=alpha),
        grid=(n // tm,),
        in_specs=[row, full(wg), full(wu), full(wd), vec, vec],
        out_specs=row,
        out_shape=jax.ShapeDtypeStruct((n, D_MODEL), F32),
        compiler_params=pltpu.CompilerParams(dimension_semantics=("arbitrary",), vmem_limit_bytes=VMEM_LIMIT),
        name="ffn_ln",
    )(x, wg, wu, wd, g, b)


def _moe_kernel(x_ref, rw_ref, rb_ref, wgu_ref, wd_ref, g_ref, b_ref, out_ref, comb_ref, acc_ref,
                *, alpha, tm):
    e = pl.program_id(1)
    lane = lax.broadcasted_iota(I32, (tm, LANES), 1)

    @pl.when(e == 0)
    def _():
        logits = jnp.dot(x_ref[...], rw_ref[...], preferred_element_type=F32,
                         precision=lax.Precision.HIGHEST) + rb_ref[...]
        logits = jnp.where(lane < N_EXPERTS, logits, -jnp.inf)
        lanef = lane.astype(F32)
        v1 = jnp.max(logits, axis=1, keepdims=True)
        i1 = jnp.min(jnp.where(logits == v1, lanef, float(LANES)), axis=1, keepdims=True)
        rest = jnp.where(lanef == i1, -jnp.inf, logits)
        v2 = jnp.max(rest, axis=1, keepdims=True)
        i2 = jnp.min(jnp.where(rest == v2, lanef, float(LANES)), axis=1, keepdims=True)
        e2 = jnp.exp(v2 - v1)
        den = 1.0 + e2
        comb_ref[...] = jnp.where(lanef == i1, 1.0 / den, 0.0) + jnp.where(lanef == i2, e2 / den, 0.0)
        acc_ref[...] = jnp.zeros((tm, D_MODEL), F32)

    xb = x_ref[...].astype(BF16)
    hgu = _dot(xb, wgu_ref[...])
    hg = hgu[:, :D_FF_EXPERT]
    h = (hg * jax.nn.sigmoid(hg)) * hgu[:, D_FF_EXPERT:]
    f = _dot(h.astype(BF16), wd_ref[...])
    ce = jnp.sum(jnp.where(lane == e, comb_ref[...], 0.0), axis=1, keepdims=True)
    acc_ref[...] += ce * f

    @pl.when(e == N_EXPERTS - 1)
    def _():
        out_ref[...] = _layer_norm(alpha * x_ref[...] + acc_ref[...], g_ref[...], b_ref[...])


def _moe_ln(x, rw, rb, wgu, wd, g, b, alpha, tm):
    n = x.shape[0]
    row = pl.BlockSpec((tm, D_MODEL), lambda i, e: (i, 0))
    vec = pl.BlockSpec((1, D_MODEL), lambda i, e: (0, 0))
    return pl.pallas_call(
        functools.partial(_moe_kernel, alpha=alpha, tm=tm),
        grid=(n // tm, N_EXPERTS),
        in_specs=[row, pl.BlockSpec((D_MODEL, LANES), lambda i, e: (0, 0)),
                  pl.BlockSpec((1, LANES), lambda i, e: (0, 0)),
                  pl.BlockSpec((None, D_MODEL, 2 * D_FF_EXPERT), lambda i, e: (e, 0, 0)),
                  pl.BlockSpec((None, D_FF_EXPERT, D_MODEL), lambda i, e: (e, 0, 0)),
                  vec, vec],
        out_specs=row,
        out_shape=jax.ShapeDtypeStruct((n, D_MODEL), F32),
        scratch_shapes=[pltpu.VMEM((tm, LANES), F32), pltpu.VMEM((tm, D_MODEL), F32)],
        compiler_params=pltpu.CompilerParams(dimension_semantics=("arbitrary", "arbitrary"),
                                             vmem_limit_bytes=VMEM_LIMIT),
        name="moe_ln",
    )(x, rw, rb, wgu, wd, g, b)


def _split_w_in(w):
    sizes = (D_MODEL, D_MODEL, D_MODEL, N_HEADS * HEAD_DIM, N_KV_HEADS * HEAD_DIM, N_KV_HEADS * HEAD_DIM,
             IDX_HEADS * IDX_DIM, IDX_DIM, IDX_HEADS, D_MODEL, D_MODEL)
    parts, start = [], 0
    for n in sizes:
        parts.append(w[:, start:start + n])
        start += n
    bg, cg, xin, q, k, v, qi, ki, wi, ga, gb = parts

    def dup(a):
        a = a.reshape(a.shape[0], -1, HEAD_DIM)
        return jnp.concatenate([a, a], axis=-1).reshape(a.shape[0], -1)

    kiwi = jnp.concatenate([ki, wi, jnp.zeros((w.shape[0], LANES - IDX_DIM - IDX_HEADS), w.dtype)], axis=1)
    return {
        'a': jnp.concatenate([bg, cg, xin, ga], axis=1).astype(BF16),
        'q': q.astype(BF16), 'gb': gb.astype(BF16), 'kk': dup(k).astype(BF16), 'vv': dup(v).astype(BF16),
        'qi': qi.astype(BF16), 'ki': jnp.concatenate([ki, ki, kiwi], axis=1).astype(BF16),
    }


def _dup_cache(a):
    a = a.astype(BF16)
    return jnp.concatenate([a, a], axis=-1).reshape(a.shape[0], a.shape[1], -1)


def _pad_rows(a, n):
    return jnp.pad(a, ((0, 0), (0, n - a.shape[1]), (0, 0)))


def _row_tile(n, cap):
    tm = min(n, cap)
    assert n % tm == 0
    return tm


def kernel(x_prompt, x_sample, cache_k, cache_v, cache_kidx, state_conv, w_in, conv_w, w_out, rel_bias,
           ln1_g, ln1_b, ln2_g, ln2_b, ffn_w_gate, ffn_w_up, ffn_w_down, router_w, router_b,
           moe_w_gate, moe_w_up, moe_w_down):
    depth = w_in.shape[0]
    alpha = (2 * depth) ** 0.25
    bp, tp, _ = x_prompt.shape
    bs, ts, _ = x_sample.shape
    past = cache_k.shape[2]
    nq_p = 256 if tp % 256 == 0 else LANES
    nq_s = LANES
    assert tp % KB == 0 and past % LANES == 0 and ts <= nq_s and ts % SUBLANES == 0
    topk_p = min(TOPK_MAX, tp // 4)
    topk_s = min(TOPK_MAX, (past + ts) // 4)
    lpad_s = -(-(past + nq_s) // KB) * KB

    bt_p = _bias_tiles(rel_bias, nq_p)
    bt_s = _bias_tiles(rel_bias, nq_s)
    vec = lambda a: a.reshape(1, -1)

    def layer(l, x, prev8, past_kv):
        b, t, _ = x.shape
        w = _split_w_in(w_in[l])
        prompt = past_kv is None
        (conv, q, sgb, kkf, kkb, vvf, vvb, qi, kiki, kiwi, cstate) = _inproj(
            x, prev8, w, conv_w[l], _row_tile(t, KB), transpose_v=prompt)
        if prompt:
            merged = _attention(q, qi, kiwi, conv, sgb, kkb, vvb, kiki, bt_p,
                                nq=nq_p, q_off=0, kv_len=t, topk=topk_p)
        else:
            ck, cv, cki = past_kv
            kk_all = _pad_rows(jnp.concatenate([_dup_cache(ck), kkb], axis=1), lpad_s)
            vv_all = _pad_rows(jnp.concatenate([_dup_cache(cv), vvb], axis=1), lpad_s)
            vvt_all = jnp.swapaxes(vv_all.reshape(b, lpad_s // KB, KB, 4 * LANES), 2, 3)
            cki = cki.astype(BF16)
            kiki_all = _pad_rows(jnp.concatenate([jnp.concatenate([cki, cki], axis=-1), kiki], axis=1), lpad_s)
            pad = lambda a: _pad_rows(a, nq_s)
            merged = _attention(pad(q), pad(qi), pad(kiwi), pad(conv), pad(sgb), kk_all, vvt_all, kiki_all, bt_s,
                                nq=nq_s, q_off=past, kv_len=past + t, topk=topk_s)[:, :t]
        n = b * t
        x1 = _outproj_ln(merged.reshape(n, D_MODEL), x.reshape(n, D_MODEL), w_out[l].astype(BF16),
                         vec(ln1_g[l]), vec(ln1_b[l]), alpha, _row_tile(n, 512))
        j = l // 2
        if l % 2 == 0:
            x2 = _ffn_ln(x1, ffn_w_gate[j].astype(BF16), ffn_w_up[j].astype(BF16), ffn_w_down[j].astype(BF16),
                         vec(ln2_g[l]), vec(ln2_b[l]), alpha, _row_tile(n, 512))
        else:
            rw = jnp.pad(router_w[j], ((0, 0), (0, LANES - N_EXPERTS)))
            rb = jnp.pad(router_b[j], (0, LANES - N_EXPERTS)).reshape(1, LANES)
            wgu = jnp.concatenate([moe_w_gate[j], moe_w_up[j]], axis=-1).astype(BF16)
            x2 = _moe_ln(x1, rw, rb, wgu, moe_w_down[j].astype(BF16), vec(ln2_g[l]), vec(ln2_b[l]), alpha,
                         _row_tile(n, 512))
        heads = lambda a: a.reshape(b, t, N_KV_HEADS, HEAD_DIM)
        return (x2.reshape(b, t, D_MODEL), heads(kkf), heads(vvf), kiwi[..., :IDX_DIM],
                cstate[:, SUBLANES - (CONV_W - 1):, :])

    xp, xs = x_prompt, x_sample
    outs_p, outs_s = [], []
    zero_prev = jnp.zeros((bp, SUBLANES, D_MODEL), F32)
    for l in range(depth):
        xp, *rest = layer(l, xp, zero_prev, None)
        outs_p.append(rest)
        prev8 = jnp.pad(state_conv[l], ((0, 0), (SUBLANES - (CONV_W - 1), 0), (0, 0)))
        xs, *rest = layer(l, xs, prev8, (cache_k[l], cache_v[l], cache_kidx[l]))
        outs_s.append(rest)
    stack = lambda outs, i: jnp.stack([o[i] for o in outs])
    return (xp, xs,
            stack(outs_p, 0), stack(outs_p, 1), stack(outs_p, 2), stack(outs_p, 3),
            stack(outs_s, 0), stack(outs_s, 1), stack(outs_s, 2), stack(outs_s, 3))
```

```python
import functools

import jax
import jax.numpy as jnp
from jax import lax
from jax.experimental import pallas as pl
from jax.experimental.pallas import tpu as pltpu

D_MODEL = 1024
CHUNK = 64
CHUNK_SHIFT = 6
CONV_W = 3
N_HEADS = 16
HEAD_DIM = 64
N_KV_HEADS = 4
GROUP = N_HEADS // N_KV_HEADS
IDX_HEADS = 8
IDX_DIM = 64
TOPK_MAX = 256
NUM_BUCKETS = 32
MAX_DISTANCE = 128
D_FF = 2816
N_EXPERTS = 8
D_FF_EXPERT = 1408
LN_EPS = 1e-5

LANES = 128
SUBLANES = 8
N_PAIRS = N_HEADS // 2
KB = 512
TILES = KB // LANES
N_BIAS_TILES = 4
PAIRS_PER_LOOP = 8
LOOKAHEAD = 3
MOE_CHUNK = 256
NEG = -1e30
LOG2E = 1.4426950408889634
INT_MIN = -2 ** 31
VMEM_LIMIT = 56 * 1024 * 1024

F32 = jnp.float32
BF16 = jnp.bfloat16
I32 = jnp.int32

assert CHUNK == 1 << CHUNK_SHIFT and 2 * HEAD_DIM == LANES and 2 * IDX_DIM == LANES


def _dot(a, b):
    return jnp.dot(a, b, preferred_element_type=F32)


def _dot_nt(a, b):
    return lax.dot_general(a, b, (((1,), (1,)), ((), ())), preferred_element_type=F32)


def _layer_norm(y, g, b):
    mu = jnp.mean(y, axis=-1, keepdims=True)
    d = y - mu
    var = jnp.mean(d * d, axis=-1, keepdims=True)
    return d * lax.rsqrt(var + LN_EPS) * g + b


def _tree(op, xs):
    xs = list(xs)
    while len(xs) > 1:
        xs = [op(xs[i], xs[i + 1]) for i in range(0, len(xs) - 1, 2)] + ([xs[-1]] if len(xs) % 2 else [])
    return xs[0]


def _fold8(x):
    return _tree(jnp.add, [x[i:i + SUBLANES] for i in range(0, x.shape[0], SUBLANES)])


def _inproj_kernel(x_ref, prev_ref, wa_ref, wq_ref, wgb_ref, wkk_ref, wvv_ref, wqi_ref, wki_ref, cw_ref,
                   conv_ref, q_ref, sgb_ref, kkf_ref, kkb_ref, vvf_ref, vvb_ref, qi_ref, kiki_ref, kiwi_ref,
                   cstate_ref, uext_ref, *, tm, transpose_v):
    @pl.when(pl.program_id(1) == 0)
    def _():
        uext_ref[0:SUBLANES, :] = prev_ref[0]

    xb = x_ref[0].astype(BF16)
    cc = 256
    for c in range(0, D_MODEL, cc):
        bg = _dot(xb, wa_ref[:, c:c + cc])
        cg = _dot(xb, wa_ref[:, D_MODEL + c:D_MODEL + c + cc])
        xin = _dot(xb, wa_ref[:, 2 * D_MODEL + c:2 * D_MODEL + c + cc])
        ga = _dot(xb, wa_ref[:, 3 * D_MODEL + c:3 * D_MODEL + c + cc])
        u = cg * xin
        uext_ref[SUBLANES:SUBLANES + tm, c:c + cc] = u
        um1 = uext_ref[SUBLANES - 1:SUBLANES - 1 + tm, c:c + cc]
        um2 = uext_ref[SUBLANES - 2:SUBLANES - 2 + tm, c:c + cc]
        y = um2 * cw_ref[0:1, c:c + cc] + um1 * cw_ref[1:2, c:c + cc] + u * cw_ref[2:3, c:c + cc]
        conv_ref[0, :, c:c + cc] = jax.nn.sigmoid(ga) * (bg * y)
    tail = uext_ref[tm:tm + SUBLANES, :]
    cstate_ref[0] = tail
    uext_ref[0:SUBLANES, :] = tail

    q_ref[0] = (_dot(xb, wq_ref[...]) * (HEAD_DIM ** -0.5 * LOG2E)).astype(BF16)
    sgb_ref[0] = jax.nn.sigmoid(_dot(xb, wgb_ref[...]))
    def undup(a):
        low = lax.broadcasted_iota(I32, (tm, LANES), 1) < HEAD_DIM
        return jnp.concatenate(
            [jnp.where(low, a[:, 2 * j * LANES:(2 * j + 1) * LANES], a[:, (2 * j + 1) * LANES:(2 * j + 2) * LANES])
             for j in range(N_KV_HEADS // 2)], axis=1)

    kk = _dot(xb, wkk_ref[...])
    kkf_ref[0] = undup(kk)
    kkb_ref[0] = kk.astype(BF16)
    vv = _dot(xb, wvv_ref[...])
    vvf_ref[0] = undup(vv)
    if transpose_v:
        vvb_ref[0, 0] = vv.T.astype(BF16)
    else:
        vvb_ref[0] = vv.astype(BF16)
    qi_ref[0] = _dot(xb, wqi_ref[...]).astype(BF16)
    kw = _dot(xb, wki_ref[...])
    kiki_ref[0] = kw[:, 0:LANES].astype(BF16)
    kiwi_ref[0] = kw[:, LANES:2 * LANES]


def _inproj(x, prev8, w, conv_w, tm, transpose_v):
    b, t, _ = x.shape
    grid = (b, t // tm)
    row = lambda n: pl.BlockSpec((1, tm, n), lambda i, j: (i, j, 0))
    full = lambda a: pl.BlockSpec(a.shape, lambda i, j: (0,) * a.ndim, pipeline_mode=pl.Buffered(1))
    if transpose_v:
        assert tm == KB
        vvb_shape = jax.ShapeDtypeStruct((b, t // KB, 4 * LANES, KB), BF16)
        vvb_spec = pl.BlockSpec((1, 1, 4 * LANES, KB), lambda i, j: (i, j, 0, 0))
    else:
        vvb_shape = jax.ShapeDtypeStruct((b, t, 4 * LANES), BF16)
        vvb_spec = row(4 * LANES)
    out_shapes = (
        jax.ShapeDtypeStruct((b, t, D_MODEL), F32),
        jax.ShapeDtypeStruct((b, t, D_MODEL), BF16),
        jax.ShapeDtypeStruct((b, t, D_MODEL), F32),
        jax.ShapeDtypeStruct((b, t, N_KV_HEADS * HEAD_DIM), F32),
        jax.ShapeDtypeStruct((b, t, 4 * LANES), BF16),
        jax.ShapeDtypeStruct((b, t, N_KV_HEADS * HEAD_DIM), F32),
        vvb_shape,
        jax.ShapeDtypeStruct((b, t, IDX_HEADS * IDX_DIM), BF16),
        jax.ShapeDtypeStruct((b, t, LANES), BF16),
        jax.ShapeDtypeStruct((b, t, LANES), F32),
        jax.ShapeDtypeStruct((b, SUBLANES, D_MODEL), F32),
    )
    out_specs = (row(D_MODEL), row(D_MODEL), row(D_MODEL), row(N_KV_HEADS * HEAD_DIM), row(4 * LANES),
                 row(N_KV_HEADS * HEAD_DIM), vvb_spec, row(IDX_HEADS * IDX_DIM), row(LANES), row(LANES),
                 pl.BlockSpec((1, SUBLANES, D_MODEL), lambda i, j: (i, 0, 0)))
    in_specs = [row(D_MODEL), pl.BlockSpec((1, SUBLANES, D_MODEL), lambda i, j: (i, 0, 0)),
                full(w['a']), full(w['q']), full(w['gb']), full(w['kk']), full(w['vv']), full(w['qi']),
                full(w['ki']), full(conv_w)]
    return pl.pallas_call(
        functools.partial(_inproj_kernel, tm=tm, transpose_v=transpose_v),
        grid=grid, in_specs=in_specs, out_specs=out_specs, out_shape=out_shapes,
        scratch_shapes=[pltpu.VMEM((tm + SUBLANES, D_MODEL), F32)],
        compiler_params=pltpu.CompilerParams(dimension_semantics=("arbitrary", "arbitrary"),
                                             vmem_limit_bytes=VMEM_LIMIT),
        name="inproj_conv",
    )(x, prev8, w['a'], w['q'], w['gb'], w['kk'], w['vv'], w['qi'], w['ki'], conv_w)


def _bias_tiles_kernel(tbl_ref, out_ref, *, nq):
    d = pl.program_id(0) - 2
    krow = lax.broadcasted_iota(I32, (LANES, nq), 0)
    qcol = lax.broadcasted_iota(I32, (LANES, nq), 1)
    rel = d * LANES + krow - qcol
    n = jnp.abs(rel)
    nb = NUM_BUCKETS // 2
    max_exact = nb // 2
    n2 = n * n
    large = jnp.full((LANES, nq), max_exact, I32)
    for j in range(1, nb - max_exact):
        large = large + (n2 >= (max_exact * max_exact) * (1 << j)).astype(I32)
    bucket = jnp.where(n < max_exact, n, large) + jnp.where(rel > 0, nb, 0)
    far = pl.program_id(0) == 0
    for h in range(N_HEADS):
        base = tbl_ref[nb - 1, h]
        acc = jnp.zeros((LANES, nq), F32)
        for bk in range(NUM_BUCKETS):
            acc = jnp.where(bucket == bk, (tbl_ref[bk, h] - base) * LOG2E, acc)
        out_ref[0, h] = jnp.where(far, 0.0, acc)


def _bias_tiles(rel_bias, nq):
    assert MAX_DISTANCE == 128 and NUM_BUCKETS == 32
    return pl.pallas_call(
        functools.partial(_bias_tiles_kernel, nq=nq),
        grid=(N_BIAS_TILES,),
        in_specs=[pl.BlockSpec(memory_space=pltpu.SMEM)],
        out_specs=pl.BlockSpec((1, N_HEADS, LANES, nq), lambda i: (i, 0, 0, 0)),
        out_shape=jax.ShapeDtypeStruct((N_BIAS_TILES, N_HEADS, LANES, nq), F32),
        name="bias_tiles",
    )(rel_bias)


def _attn_kernel(q_ref, qi_ref, kiwi_ref, conv_ref, sgb_ref, kk_ref, vvt_ref, kiki_ref, bt_ref, out_ref,
                 key_ref, mask_ref, qis_ref, tri_ref, qs_ref, acc_ref, *, nq, q_off, kv_len, topk):
    q0 = q_off + pl.program_id(1) * nq
    nkb = (q0 + nq + KB - 1) // KB
    jd = q0 // LANES
    lane = lax.broadcasted_iota(I32, (nq, LANES), 1)
    low = lane < HEAD_DIM
    krow = lax.broadcasted_iota(I32, (LANES, nq), 0)
    rowlow = krow < HEAD_DIM
    qchunk = lax.shift_right_arithmetic(q0 + lax.broadcasted_iota(I32, (LANES, nq), 1), CHUNK_SHIFT)

    wit = kiwi_ref[0].T * ((IDX_DIM ** -0.5) * (IDX_HEADS ** -0.5))
    wis = [wit[IDX_DIM + h:IDX_DIM + h + 1, :] for h in range(IDX_HEADS)]
    for h in range(IDX_HEADS):
        pair = qi_ref[0, :, (h // 2) * LANES:(h // 2 + 1) * LANES].astype(F32)
        keep = low if h % 2 == 0 else jnp.logical_not(low)
        qis_ref[h] = jnp.where(keep, pair, 0.0).astype(BF16)

    def score_block(jb, carry):
        ks = pl.multiple_of(jb * KB, KB)
        ki = kiki_ref[0, pl.ds(ks, KB), :]
        sc = [jnp.zeros((LANES, nq), F32) for _ in range(TILES)]
        for h in range(IDX_HEADS):
            s = _dot_nt(ki, qis_ref[h])
            for c in range(TILES):
                sc[c] = sc[c] + jnp.maximum(s[c * LANES:(c + 1) * LANES], 0.0) * wis[h]
        for c in range(TILES):
            bits = lax.bitcast_convert_type(sc[c], I32)
            key = jnp.where(bits < 0, bits ^ 0x7FFFFFFF, bits)
            kpos = ks + c * LANES + krow
            adm = (lax.shift_right_arithmetic(kpos, CHUNK_SHIFT) <= qchunk) & (kpos < kv_len)
            key_ref[pl.ds(ks + c * LANES, LANES), :] = jnp.where(adm, key, INT_MIN)
        return carry

    lax.fori_loop(0, nkb, score_block, 0)

    def count(pred):
        def body(jb, acc):
            ks = pl.multiple_of(jb * KB, KB)
            parts = []
            for c in range(TILES):
                blk = key_ref[pl.ds(ks + c * LANES, LANES), :]
                parts.append(_fold8(pred(blk, ks + c * LANES).astype(I32)))
            return acc + _tree(jnp.add, parts)
        acc = lax.fori_loop(0, nkb, body, jnp.zeros((SUBLANES, nq), I32))
        return jnp.sum(acc.astype(F32), axis=0, keepdims=True).astype(I32)

    def thr_step(i, t):
        cand = t + lax.shift_left(jnp.int32(1), 31 - i)
        cnt = count(lambda blk, base: blk >= cand)
        return jnp.where(cnt >= topk, cand, t)

    thr = lax.fori_loop(0, 32, thr_step, jnp.full((1, nq), INT_MIN, I32))

    want = (topk - count(lambda blk, base: blk > thr)).astype(F32)
    live = thr > INT_MIN
    tri_ref[...] = (lax.broadcasted_iota(I32, (KB, KB), 0) > lax.broadcasted_iota(I32, (KB, KB), 1)
                    ).astype(F32).astype(BF16)

    def mask_block(jb, seen):
        ks = pl.multiple_of(jb * KB, KB)
        keys = [key_ref[pl.ds(ks + c * LANES, LANES), :] for c in range(TILES)]
        ties = [((k == thr) & live).astype(F32) for k in keys]
        before = _dot(tri_ref[...], jnp.concatenate(ties, axis=0).astype(BF16)) + seen
        for c in range(TILES):
            keep = (keys[c] > thr) | ((ties[c] > 0.0) & (before[c * LANES:(c + 1) * LANES] < want))
            mask_ref[pl.ds(ks + c * LANES, LANES), :] = jnp.where(keep, 0.0, NEG)
        return seen + jnp.sum(_fold8(_tree(jnp.add, ties)), axis=0, keepdims=True)

    lax.fori_loop(0, nkb, mask_block, jnp.zeros((1, nq), F32))

    def head_step(s, msks, biases, m):
        ss = [s[c * LANES:(c + 1) * LANES] + msks[c] for c in range(TILES)]
        if biases is not None:
            ss = [x + b for x, b in zip(ss, biases)]
        m_new = jnp.maximum(m, jnp.max(_tree(jnp.maximum, ss), axis=0, keepdims=True))
        return m_new, jnp.exp2(m - m_new), jnp.concatenate([jnp.exp2(x - m_new) for x in ss], axis=0).astype(BF16)

    vrow_low = lax.broadcasted_iota(I32, (LANES, KB), 0) < HEAD_DIM

    for p0 in range(0, N_PAIRS, PAIRS_PER_LOOP):
        pairs = tuple(range(p0, p0 + PAIRS_PER_LOOP))
        for p in pairs:
            qp = q_ref[0, :, p * LANES:(p + 1) * LANES].astype(F32)
            qs_ref[2 * p] = jnp.where(low, qp, 0.0).astype(BF16)
            qs_ref[2 * p + 1] = jnp.where(low, 0.0, qp).astype(BF16)

        def attn_block(jb, carry, with_bias):
            ks = pl.multiple_of(jb * KB, KB)
            msks = [mask_ref[pl.ds(ks + c * LANES, LANES), :] for c in range(TILES)]
            tiles = [jnp.clip(jb * TILES + c - jd + 2, 0, N_BIAS_TILES - 1) for c in range(TILES)]

            def logits(head):
                g = head // GROUP
                kblk = kk_ref[0, pl.ds(ks, KB), g * LANES:(g + 1) * LANES]
                return _dot_nt(kblk, qs_ref[head])

            heads = [2 * p + r for p in pairs for r in range(2)]
            pending = [logits(h) for h in heads[:LOOKAHEAD]]
            vts = {}
            out = []
            for i, p in enumerate(pairs):
                g = (2 * p) // GROUP
                if g not in vts:
                    vt = vvt_ref[0, jb, g * LANES:(g + 1) * LANES, :]
                    one = jnp.ones_like(vt)
                    vts[g] = (jnp.where(vrow_low, vt, one), jnp.where(vrow_low, one, vt))
                ma, la, mb, lb = carry[i]
                stats = []
                for r, m in enumerate((ma, mb)):
                    nxt = 2 * i + r + LOOKAHEAD
                    if nxt < len(heads):
                        pending.append(logits(heads[nxt]))
                    biases = [bt_ref[tiles[c], 2 * p + r] for c in range(TILES)] if with_bias else None
                    stats.append(head_step(pending.pop(0), msks, biases, m))
                (ma, aa, pa), (mb, ab, pb) = stats
                pva = _dot(vts[g][0], pa)
                pvb = _dot(vts[g][1], pb)
                la = aa * la + pva[HEAD_DIM:HEAD_DIM + 1]
                lb = ab * lb + pvb[0:1]
                acc_ref[p] = jnp.where(rowlow, aa, ab) * acc_ref[p] + jnp.where(rowlow, pva, pvb)
                out.append((ma, la, mb, lb))
            return tuple(out)

        for p in pairs:
            acc_ref[p] = jnp.zeros((LANES, nq), F32)
        init = (jnp.full((1, nq), NEG, F32), jnp.zeros((1, nq), F32),
                jnp.full((1, nq), NEG, F32), jnp.zeros((1, nq), F32))
        nfar = jnp.clip((jd - 1) // TILES, 0, nkb)
        res = lax.fori_loop(0, nfar, functools.partial(attn_block, with_bias=False), (init,) * len(pairs))
        res = lax.fori_loop(nfar, nkb, functools.partial(attn_block, with_bias=True), res)
        for i, p in enumerate(pairs):
            _, la, _, lb = res[i]
            o = (acc_ref[p] / jnp.where(rowlow, la, lb)).T
            sl = slice(p * LANES, (p + 1) * LANES)
            out_ref[0, :, sl] = (conv_ref[0, :, sl] + sgb_ref[0, :, sl] * o).astype(BF16)


def _attention(q, qi, kiwi, conv, sgb, kk, vvt, kiki, bias_tiles, *, nq, q_off, kv_len, topk):
    b, t, _ = q.shape
    lpad = kk.shape[1]
    assert q_off % LANES == 0 and nq % LANES == 0 and t % nq == 0
    assert lpad % KB == 0 and lpad >= q_off + t and vvt.shape == (b, lpad // KB, 4 * LANES, KB)
    grid = (b, t // nq)
    row = lambda n: pl.BlockSpec((1, nq, n), lambda i, j: (i, j, 0))
    keys = lambda n: pl.BlockSpec((1, lpad, n), lambda i, j: (i, 0, 0))
    return pl.pallas_call(
        functools.partial(_attn_kernel, nq=nq, q_off=q_off, kv_len=kv_len, topk=topk),
        grid=grid,
        in_specs=[row(D_MODEL), row(IDX_HEADS * IDX_DIM), row(LANES), row(D_MODEL), row(D_MODEL),
                  keys(4 * LANES),
                  pl.BlockSpec((1, lpad // KB, 4 * LANES, KB), lambda i, j: (i, 0, 0, 0)),
                  keys(LANES),
                  pl.BlockSpec((N_BIAS_TILES, N_HEADS, LANES, nq), lambda i, j: (0, 0, 0, 0),
                               pipeline_mode=pl.Buffered(1))],
        out_specs=row(D_MODEL),
        out_shape=jax.ShapeDtypeStruct((b, t, D_MODEL), BF16),
        scratch_shapes=[pltpu.VMEM((lpad, nq), I32), pltpu.VMEM((lpad, nq), F32),
                        pltpu.VMEM((IDX_HEADS, nq, LANES), BF16), pltpu.VMEM((KB, KB), BF16),
                        pltpu.VMEM((N_HEADS, nq, LANES), BF16), pltpu.VMEM((N_PAIRS, LANES, nq), F32)],
        compiler_params=pltpu.CompilerParams(dimension_semantics=("arbitrary", "arbitrary"),
                                             vmem_limit_bytes=VMEM_LIMIT),
        name="dsa_attention",
    )(q, qi, kiwi, conv, sgb, kk, vvt, kiki, bias_tiles)


def _outproj_kernel(m_ref, x_ref, w_ref, g_ref, b_ref, out_ref, *, alpha):
    y = alpha * x_ref[...] + _dot(m_ref[...], w_ref[...])
    out_ref[...] = _layer_norm(y, g_ref[...], b_ref[...])


def _outproj_ln(merged, x, w_out, g, b, alpha, tm):
    n = x.shape[0]
    row = pl.BlockSpec((tm, D_MODEL), lambda i: (i, 0))
    vec = pl.BlockSpec((1, D_MODEL), lambda i: (0, 0))
    return pl.pallas_call(
        functools.partial(_outproj_kernel, alpha=alpha),
        grid=(n // tm,),
        in_specs=[row, row, pl.BlockSpec((D_MODEL, D_MODEL), lambda i: (0, 0)), vec, vec],
        out_specs=row,
        out_shape=jax.ShapeDtypeStruct((n, D_MODEL), F32),
        compiler_params=pltpu.CompilerParams(dimension_semantics=("arbitrary",), vmem_limit_bytes=VMEM_LIMIT),
        name="outproj_ln",
    )(merged, x, w_out, g, b)


def _ffn_kernel(x_ref, wg_ref, wu_ref, wd_ref, g_ref, b_ref, out_ref, *, alpha):
    x = x_ref[...]
    xb = x.astype(BF16)
    hg = _dot(xb, wg_ref[...])
    h = (hg * jax.nn.sigmoid(hg)) * _dot(xb, wu_ref[...])
    f = _dot(h.astype(BF16), wd_ref[...])
    out_ref[...] = _layer_norm(alpha * x + f, g_ref[...], b_ref[...])


def _ffn_ln(x, wg, wu, wd, g, b, alpha, tm):
    n = x.shape[0]
    row = pl.BlockSpec((tm, D_MODEL), lambda i: (i, 0))
    vec = pl.BlockSpec((1, D_MODEL), lambda i: (0, 0))
    full = lambda a: pl.BlockSpec(a.shape, lambda i: (0, 0), pipeline_mode=pl.Buffered(1))
    return pl.pallas_call(
        functools.partial(_ffn_kernel, alpha=alpha),
        grid=(n // tm,),
        in_specs=[row, full(wg), full(wu), full(wd), vec, vec],
        out_specs=row,
        out_shape=jax.ShapeDtypeStruct((n, D_MODEL), F32),
        compiler_params=pltpu.CompilerParams(dimension_semantics=("arbitrary",), vmem_limit_bytes=VMEM_LIMIT),
        name="ffn_ln",
    )(x, wg, wu, wd, g, b)


def _moe_kernel(x_ref, rw_ref, rb_ref, wgu_ref, wd_ref, g_ref, b_ref, out_ref,
                rank_ref, rankt_ref, combt_ref, xb_ref, acc_ref, *, alpha, tm):
    e = pl.program_id(1)
    ch = min(MOE_CHUNK, tm)
    lane = lax.broadcasted_iota(I32, (tm, LANES), 1)

    @pl.when(e == 0)
    def _():
        logits = jnp.dot(x_ref[...], rw_ref[...], preferred_element_type=F32,
                         precision=lax.Precision.HIGHEST) + rb_ref[...]
        logits = jnp.where(lane < N_EXPERTS, logits, -jnp.inf)
        lanef = lane.astype(F32)
        v1 = jnp.max(logits, axis=1, keepdims=True)
        i1 = jnp.min(jnp.where(logits == v1, lanef, float(LANES)), axis=1, keepdims=True)
        rest = jnp.where(lanef == i1, -jnp.inf, logits)
        v2 = jnp.max(rest, axis=1, keepdims=True)
        i2 = jnp.min(jnp.where(rest == v2, lanef, float(LANES)), axis=1, keepdims=True)
        e2 = jnp.exp(v2 - v1)
        den = 1.0 + e2
        comb = jnp.where(lanef == i1, 1.0 / den, 0.0) + jnp.where(lanef == i2, e2 / den, 0.0)
        routed = (lanef == i1) | (lanef == i2)
        earlier = (lax.broadcasted_iota(I32, (tm, tm), 0) > lax.broadcasted_iota(I32, (tm, tm), 1))
        rank = _dot(earlier.astype(F32).astype(BF16), routed.astype(F32).astype(BF16))
        rank = jnp.where(routed, rank, -1.0)
        rank_ref[...] = rank.astype(I32)
        rankt_ref[...] = rank.T.astype(I32)
        combt_ref[...] = comb.T
        xb_ref[...] = x_ref[...].astype(BF16)
        acc_ref[...] = jnp.zeros((tm, D_MODEL), F32)

    rcol = jnp.max(jnp.where(lane == e, rank_ref[...], -1).astype(F32), axis=1, keepdims=True).astype(I32)
    rrow = rankt_ref[pl.ds(e, 1), :]
    crow = combt_ref[pl.ds(e, 1), :]
    n_routed = jnp.max(rcol.astype(F32)).astype(I32) + 1

    def chunk(c, carry):
        base = c * ch
        hit = rrow == lax.broadcasted_iota(I32, (ch, tm), 0) + base
        xg = _dot(jnp.where(hit, 1.0, 0.0).astype(BF16), xb_ref[...]).astype(BF16)
        hgu = _dot(xg, wgu_ref[...])
        hg = hgu[:, :D_FF_EXPERT]
        h = (hg * jax.nn.sigmoid(hg)) * hgu[:, D_FF_EXPERT:]
        f = _dot(h.astype(BF16), wd_ref[...])
        gate = jnp.sum(jnp.where(hit, crow, 0.0), axis=1, keepdims=True)
        hit_t = rcol == lax.broadcasted_iota(I32, (tm, ch), 1) + base
        acc_ref[...] += _dot(jnp.where(hit_t, 1.0, 0.0).astype(BF16), (f * gate).astype(BF16))
        return carry

    lax.fori_loop(0, (n_routed + ch - 1) // ch, chunk, 0)

    @pl.when(e == N_EXPERTS - 1)
    def _():
        out_ref[...] = _layer_norm(alpha * x_ref[...] + acc_ref[...], g_ref[...], b_ref[...])


def _moe_ln(x, rw, rb, wgu, wd, g, b, alpha, tm):
    n = x.shape[0]
    row = pl.BlockSpec((tm, D_MODEL), lambda i, e: (i, 0))
    vec = pl.BlockSpec((1, D_MODEL), lambda i, e: (0, 0))
    return pl.pallas_call(
        functools.partial(_moe_kernel, alpha=alpha, tm=tm),
        grid=(n // tm, N_EXPERTS),
        in_specs=[row, pl.BlockSpec((D_MODEL, LANES), lambda i, e: (0, 0)),
                  pl.BlockSpec((1, LANES), lambda i, e: (0, 0)),
                  pl.BlockSpec((None, D_MODEL, 2 * D_FF_EXPERT), lambda i, e: (e, 0, 0)),
                  pl.BlockSpec((None, D_FF_EXPERT, D_MODEL), lambda i, e: (e, 0, 0)),
                  vec, vec],
        out_specs=row,
        out_shape=jax.ShapeDtypeStruct((n, D_MODEL), F32),
        scratch_shapes=[pltpu.VMEM((tm, LANES), I32), pltpu.VMEM((LANES, tm), I32), pltpu.VMEM((LANES, tm), F32),
                        pltpu.VMEM((tm, D_MODEL), BF16), pltpu.VMEM((tm, D_MODEL), F32)],
        compiler_params=pltpu.CompilerParams(dimension_semantics=("arbitrary", "arbitrary"),
                                             vmem_limit_bytes=VMEM_LIMIT),
        name="moe_ln",
    )(x, rw, rb, wgu, wd, g, b)


def _split_w_in(w):
    sizes = (D_MODEL, D_MODEL, D_MODEL, N_HEADS * HEAD_DIM, N_KV_HEADS * HEAD_DIM, N_KV_HEADS * HEAD_DIM,
             IDX_HEADS * IDX_DIM, IDX_DIM, IDX_HEADS, D_MODEL, D_MODEL)
    parts, start = [], 0
    for n in sizes:
        parts.append(w[:, start:start + n])
        start += n
    bg, cg, xin, q, k, v, qi, ki, wi, ga, gb = parts

    def dup(a):
        a = a.reshape(a.shape[0], -1, HEAD_DIM)
        return jnp.concatenate([a, a], axis=-1).reshape(a.shape[0], -1)

    kiwi = jnp.concatenate([ki, wi, jnp.zeros((w.shape[0], LANES - IDX_DIM - IDX_HEADS), w.dtype)], axis=1)
    return {
        'a': jnp.concatenate([bg, cg, xin, ga], axis=1).astype(BF16),
        'q': q.astype(BF16), 'gb': gb.astype(BF16), 'kk': dup(k).astype(BF16), 'vv': dup(v).astype(BF16),
        'qi': qi.astype(BF16), 'ki': jnp.concatenate([ki, ki, kiwi], axis=1).astype(BF16),
    }


def _dup_cache(a):
    a = a.astype(BF16)
    return jnp.concatenate([a, a], axis=-1).reshape(a.shape[0], a.shape[1], -1)


def _pad_rows(a, n):
    return jnp.pad(a, ((0, 0), (0, n - a.shape[1]), (0, 0)))


def _row_tile(n, cap):
    tm = min(n, cap)
    assert n % tm == 0
    return tm


def kernel(x_prompt, x_sample, cache_k, cache_v, cache_kidx, state_conv, w_in, conv_w, w_out, rel_bias,
           ln1_g, ln1_b, ln2_g, ln2_b, ffn_w_gate, ffn_w_up, ffn_w_down, router_w, router_b,
           moe_w_gate, moe_w_up, moe_w_down):
    depth = w_in.shape[0]
    alpha = (2 * depth) ** 0.25
    bp, tp, _ = x_prompt.shape
    bs, ts, _ = x_sample.shape
    past = cache_k.shape[2]
    nq_p = 256 if tp % 256 == 0 else LANES
    nq_s = LANES
    assert tp % KB == 0 and past % LANES == 0 and ts <= nq_s and ts % SUBLANES == 0
    topk_p = min(TOPK_MAX, tp // 4)
    topk_s = min(TOPK_MAX, (past + ts) // 4)
    lpad_s = -(-(past + nq_s) // KB) * KB

    bt_p = _bias_tiles(rel_bias, nq_p)
    bt_s = _bias_tiles(rel_bias, nq_s)
    vec = lambda a: a.reshape(1, -1)

    def layer(l, x, prev8, past_kv):
        b, t, _ = x.shape
        w = _split_w_in(w_in[l])
        prompt = past_kv is None
        (conv, q, sgb, kkf, kkb, vvf, vvb, qi, kiki, kiwi, cstate) = _inproj(
            x, prev8, w, conv_w[l], _row_tile(t, KB), transpose_v=prompt)
        if prompt:
            merged = _attention(q, qi, kiwi, conv, sgb, kkb, vvb, kiki, bt_p,
                                nq=nq_p, q_off=0, kv_len=t, topk=topk_p)
        else:
            ck, cv, cki = past_kv
            kk_all = _pad_rows(jnp.concatenate([_dup_cache(ck), kkb], axis=1), lpad_s)
            vv_all = _pad_rows(jnp.concatenate([_dup_cache(cv), vvb], axis=1), lpad_s)
            vvt_all = jnp.swapaxes(vv_all.reshape(b, lpad_s // KB, KB, 4 * LANES), 2, 3)
            cki = cki.astype(BF16)
            kiki_all = _pad_rows(jnp.concatenate([jnp.concatenate([cki, cki], axis=-1), kiki], axis=1), lpad_s)
            pad = lambda a: _pad_rows(a, nq_s)
            merged = _attention(pad(q), pad(qi), pad(kiwi), pad(conv), pad(sgb), kk_all, vvt_all, kiki_all, bt_s,
                                nq=nq_s, q_off=past, kv_len=past + t, topk=topk_s)[:, :t]
        n = b * t
        x1 = _outproj_ln(merged.reshape(n, D_MODEL), x.reshape(n, D_MODEL), w_out[l].astype(BF16),
                         vec(ln1_g[l]), vec(ln1_b[l]), alpha, _row_tile(n, 512))
        j = l // 2
        if l % 2 == 0:
            x2 = _ffn_ln(x1, ffn_w_gate[j].astype(BF16), ffn_w_up[j].astype(BF16), ffn_w_down[j].astype(BF16),
                         vec(ln2_g[l]), vec(ln2_b[l]), alpha, _row_tile(n, 512))
        else:
            rw = jnp.pad(router_w[j], ((0, 0), (0, LANES - N_EXPERTS)))
            rb = jnp.pad(router_b[j], (0, LANES - N_EXPERTS)).reshape(1, LANES)
            wgu = jnp.concatenate([moe_w_gate[j], moe_w_up[j]], axis=-1).astype(BF16)
            x2 = _moe_ln(x1, rw, rb, wgu, moe_w_down[j].astype(BF16), vec(ln2_g[l]), vec(ln2_b[l]), alpha,
                         _row_tile(n, 512))
        heads = lambda a: a.reshape(b, t, N_KV_HEADS, HEAD_DIM)
        return (x2.reshape(b, t, D_MODEL), heads(kkf), heads(vvf), kiwi[..., :IDX_DIM],
                cstate[:, SUBLANES - (CONV_W - 1):, :])

    xp, xs = x_prompt, x_sample
    outs_p, outs_s = [], []
    zero_prev = jnp.zeros((bp, SUBLANES, D_MODEL), F32)
    for l in range(depth):
        xp, *rest = layer(l, xp, zero_prev, None)
        outs_p.append(rest)
        prev8 = jnp.pad(state_conv[l], ((0, 0), (SUBLANES - (CONV_W - 1), 0), (0, 0)))
        xs, *rest = layer(l, xs, prev8, (cache_k[l], cache_v[l], cache_kidx[l]))
        outs_s.append(rest)
    stack = lambda outs, i: jnp.stack([o[i] for o in outs])
    return (xp, xs,
            stack(outs_p, 0), stack(outs_p, 1), stack(outs_p, 2), stack(outs_p, 3),
            stack(outs_s, 0), stack(outs_s, 1), stack(outs_s, 2), stack(outs_s, 3))
```

```python
import functools

import jax
import jax.numpy as jnp
from jax import lax
from jax.experimental import pallas as pl
from jax.experimental.pallas import tpu as pltpu

D_MODEL = 1024
CHUNK = 64
CHUNK_SHIFT = 6
CONV_W = 3
N_HEADS = 16
HEAD_DIM = 64
N_KV_HEADS = 4
GROUP = N_HEADS // N_KV_HEADS
IDX_HEADS = 8
IDX_DIM = 64
TOPK_MAX = 256
NUM_BUCKETS = 32
MAX_DISTANCE = 128
D_FF = 2816
N_EXPERTS = 8
D_FF_EXPERT = 1408
LN_EPS = 1e-5

LANES = 128
SUBLANES = 8
N_PAIRS = N_HEADS // 2
KB = 512
TILES = KB // LANES
N_BIAS_TILES = 4
PAIRS_PER_LOOP = 8
LOOKAHEAD = 3
MOE_SUB = 512
MOE_CHUNK = 160
NEG = -1e30
LOG2E = 1.4426950408889634
INT_MIN = -2 ** 31
VMEM_LIMIT = 56 * 1024 * 1024

F32 = jnp.float32
BF16 = jnp.bfloat16
I32 = jnp.int32

assert CHUNK == 1 << CHUNK_SHIFT and 2 * HEAD_DIM == LANES and 2 * IDX_DIM == LANES


def _dot(a, b):
    return jnp.dot(a, b, preferred_element_type=F32)


def _dot_nt(a, b):
    return lax.dot_general(a, b, (((1,), (1,)), ((), ())), preferred_element_type=F32)


def _layer_norm(y, g, b):
    mu = jnp.mean(y, axis=-1, keepdims=True)
    d = y - mu
    var = jnp.mean(d * d, axis=-1, keepdims=True)
    return d * lax.rsqrt(var + LN_EPS) * g + b


def _tree(op, xs):
    xs = list(xs)
    while len(xs) > 1:
        xs = [op(xs[i], xs[i + 1]) for i in range(0, len(xs) - 1, 2)] + ([xs[-1]] if len(xs) % 2 else [])
    return xs[0]


def _fold8(x):
    return _tree(jnp.add, [x[i:i + SUBLANES] for i in range(0, x.shape[0], SUBLANES)])


def _inproj_kernel(x_ref, prev_ref, wa_ref, wq_ref, wgb_ref, wkk_ref, wvv_ref, wqi_ref, wki_ref, cw_ref,
                   conv_ref, q_ref, sgb_ref, kkf_ref, kkb_ref, vvf_ref, vvb_ref, qi_ref, kiki_ref, kiwi_ref,
                   cstate_ref, uext_ref, *, tm, transpose_v):
    @pl.when(pl.program_id(1) == 0)
    def _():
        uext_ref[0:SUBLANES, :] = prev_ref[0]

    xb = x_ref[0].astype(BF16)
    cc = 256
    for c in range(0, D_MODEL, cc):
        bg = _dot(xb, wa_ref[:, c:c + cc])
        cg = _dot(xb, wa_ref[:, D_MODEL + c:D_MODEL + c + cc])
        xin = _dot(xb, wa_ref[:, 2 * D_MODEL + c:2 * D_MODEL + c + cc])
        ga = _dot(xb, wa_ref[:, 3 * D_MODEL + c:3 * D_MODEL + c + cc])
        u = cg * xin
        uext_ref[SUBLANES:SUBLANES + tm, c:c + cc] = u
        um1 = uext_ref[SUBLANES - 1:SUBLANES - 1 + tm, c:c + cc]
        um2 = uext_ref[SUBLANES - 2:SUBLANES - 2 + tm, c:c + cc]
        y = um2 * cw_ref[0:1, c:c + cc] + um1 * cw_ref[1:2, c:c + cc] + u * cw_ref[2:3, c:c + cc]
        conv_ref[0, :, c:c + cc] = jax.nn.sigmoid(ga) * (bg * y)
    tail = uext_ref[tm:tm + SUBLANES, :]
    cstate_ref[0] = tail
    uext_ref[0:SUBLANES, :] = tail

    q_ref[0] = (_dot(xb, wq_ref[...]) * (HEAD_DIM ** -0.5 * LOG2E)).astype(BF16)
    sgb_ref[0] = jax.nn.sigmoid(_dot(xb, wgb_ref[...]))
    def undup(a):
        low = lax.broadcasted_iota(I32, (tm, LANES), 1) < HEAD_DIM
        return jnp.concatenate(
            [jnp.where(low, a[:, 2 * j * LANES:(2 * j + 1) * LANES], a[:, (2 * j + 1) * LANES:(2 * j + 2) * LANES])
             for j in range(N_KV_HEADS // 2)], axis=1)

    kk = _dot(xb, wkk_ref[...])
    kkf_ref[0] = undup(kk)
    kkb_ref[0] = kk.astype(BF16)
    vv = _dot(xb, wvv_ref[...])
    vvf_ref[0] = undup(vv)
    if transpose_v:
        vvb_ref[0, 0] = vv.T.astype(BF16)
    else:
        vvb_ref[0] = vv.astype(BF16)
    qi_ref[0] = _dot(xb, wqi_ref[...]).astype(BF16)
    kw = _dot(xb, wki_ref[...])
    kiki_ref[0] = kw[:, 0:LANES].astype(BF16)
    kiwi_ref[0] = kw[:, LANES:2 * LANES]


def _inproj(x, prev8, w, conv_w, tm, transpose_v):
    b, t, _ = x.shape
    grid = (b, t // tm)
    row = lambda n: pl.BlockSpec((1, tm, n), lambda i, j: (i, j, 0))
    full = lambda a: pl.BlockSpec(a.shape, lambda i, j: (0,) * a.ndim, pipeline_mode=pl.Buffered(1))
    if transpose_v:
        assert tm == KB
        vvb_shape = jax.ShapeDtypeStruct((b, t // KB, 4 * LANES, KB), BF16)
        vvb_spec = pl.BlockSpec((1, 1, 4 * LANES, KB), lambda i, j: (i, j, 0, 0))
    else:
        vvb_shape = jax.ShapeDtypeStruct((b, t, 4 * LANES), BF16)
        vvb_spec = row(4 * LANES)
    out_shapes = (
        jax.ShapeDtypeStruct((b, t, D_MODEL), F32),
        jax.ShapeDtypeStruct((b, t, D_MODEL), BF16),
        jax.ShapeDtypeStruct((b, t, D_MODEL), F32),
        jax.ShapeDtypeStruct((b, t, N_KV_HEADS * HEAD_DIM), F32),
        jax.ShapeDtypeStruct((b, t, 4 * LANES), BF16),
        jax.ShapeDtypeStruct((b, t, N_KV_HEADS * HEAD_DIM), F32),
        vvb_shape,
        jax.ShapeDtypeStruct((b, t, IDX_HEADS * IDX_DIM), BF16),
        jax.ShapeDtypeStruct((b, t, LANES), BF16),
        jax.ShapeDtypeStruct((b, t, LANES), F32),
        jax.ShapeDtypeStruct((b, SUBLANES, D_MODEL), F32),
    )
    out_specs = (row(D_MODEL), row(D_MODEL), row(D_MODEL), row(N_KV_HEADS * HEAD_DIM), row(4 * LANES),
                 row(N_KV_HEADS * HEAD_DIM), vvb_spec, row(IDX_HEADS * IDX_DIM), row(LANES), row(LANES),
                 pl.BlockSpec((1, SUBLANES, D_MODEL), lambda i, j: (i, 0, 0)))
    in_specs = [row(D_MODEL), pl.BlockSpec((1, SUBLANES, D_MODEL), lambda i, j: (i, 0, 0)),
                full(w['a']), full(w['q']), full(w['gb']), full(w['kk']), full(w['vv']), full(w['qi']),
                full(w['ki']), full(conv_w)]
    return pl.pallas_call(
        functools.partial(_inproj_kernel, tm=tm, transpose_v=transpose_v),
        grid=grid, in_specs=in_specs, out_specs=out_specs, out_shape=out_shapes,
        scratch_shapes=[pltpu.VMEM((tm + SUBLANES, D_MODEL), F32)],
        compiler_params=pltpu.CompilerParams(dimension_semantics=("arbitrary", "arbitrary"),
                                             vmem_limit_bytes=VMEM_LIMIT),
        name="inproj_conv",
    )(x, prev8, w['a'], w['q'], w['gb'], w['kk'], w['vv'], w['qi'], w['ki'], conv_w)


def _bias_tiles_kernel(tbl_ref, out_ref, *, nq):
    d = pl.program_id(0) - 2
    krow = lax.broadcasted_iota(I32, (LANES, nq), 0)
    qcol = lax.broadcasted_iota(I32, (LANES, nq), 1)
    rel = d * LANES + krow - qcol
    n = jnp.abs(rel)
    nb = NUM_BUCKETS // 2
    max_exact = nb // 2
    n2 = n * n
    large = jnp.full((LANES, nq), max_exact, I32)
    for j in range(1, nb - max_exact):
        large = large + (n2 >= (max_exact * max_exact) * (1 << j)).astype(I32)
    bucket = jnp.where(n < max_exact, n, large) + jnp.where(rel > 0, nb, 0)
    far = pl.program_id(0) == 0
    for h in range(N_HEADS):
        base = tbl_ref[nb - 1, h]
        acc = jnp.zeros((LANES, nq), F32)
        for bk in range(NUM_BUCKETS):
            acc = jnp.where(bucket == bk, (tbl_ref[bk, h] - base) * LOG2E, acc)
        out_ref[0, h] = jnp.where(far, 0.0, acc)


def _bias_tiles(rel_bias, nq):
    assert MAX_DISTANCE == 128 and NUM_BUCKETS == 32
    return pl.pallas_call(
        functools.partial(_bias_tiles_kernel, nq=nq),
        grid=(N_BIAS_TILES,),
        in_specs=[pl.BlockSpec(memory_space=pltpu.SMEM)],
        out_specs=pl.BlockSpec((1, N_HEADS, LANES, nq), lambda i: (i, 0, 0, 0)),
        out_shape=jax.ShapeDtypeStruct((N_BIAS_TILES, N_HEADS, LANES, nq), F32),
        name="bias_tiles",
    )(rel_bias)


def _attn_kernel(q_ref, qi_ref, kiwi_ref, conv_ref, sgb_ref, x_ref, kk_ref, vvt_ref, kiki_ref, bt_ref,
                 wout_ref, g_ref, b_ref, out_ref,
                 key_ref, mask_ref, qis_ref, tri_ref, qs_ref, acc_ref, merged_ref, *, alpha, nq, q_off, kv_len, topk):
    q0 = q_off + pl.program_id(1) * nq
    nkb = (q0 + nq + KB - 1) // KB
    jd = q0 // LANES
    lane = lax.broadcasted_iota(I32, (nq, LANES), 1)
    low = lane < HEAD_DIM
    krow = lax.broadcasted_iota(I32, (LANES, nq), 0)
    rowlow = krow < HEAD_DIM
    qchunk = lax.shift_right_arithmetic(q0 + lax.broadcasted_iota(I32, (LANES, nq), 1), CHUNK_SHIFT)

    wit = kiwi_ref[0].T * ((IDX_DIM ** -0.5) * (IDX_HEADS ** -0.5))
    wis = [wit[IDX_DIM + h:IDX_DIM + h + 1, :] for h in range(IDX_HEADS)]
    for h in range(IDX_HEADS):
        pair = qi_ref[0, :, (h // 2) * LANES:(h // 2 + 1) * LANES].astype(F32)
        keep = low if h % 2 == 0 else jnp.logical_not(low)
        qis_ref[h] = jnp.where(keep, pair, 0.0).astype(BF16)

    def score_block(jb, carry):
        ks = pl.multiple_of(jb * KB, KB)
        ki = kiki_ref[0, pl.ds(ks, KB), :]
        sc = [jnp.zeros((LANES, nq), F32) for _ in range(TILES)]
        for h in range(IDX_HEADS):
            s = _dot_nt(ki, qis_ref[h])
            for c in range(TILES):
                sc[c] = sc[c] + jnp.maximum(s[c * LANES:(c + 1) * LANES], 0.0) * wis[h]
        for c in range(TILES):
            bits = lax.bitcast_convert_type(sc[c], I32)
            key = jnp.where(bits < 0, bits ^ 0x7FFFFFFF, bits)
            kpos = ks + c * LANES + krow
            adm = (lax.shift_right_arithmetic(kpos, CHUNK_SHIFT) <= qchunk) & (kpos < kv_len)
            key_ref[pl.ds(ks + c * LANES, LANES), :] = jnp.where(adm, key, INT_MIN)
        return carry

    lax.fori_loop(0, nkb, score_block, 0)

    def count(pred):
        def body(jb, acc):
            ks = pl.multiple_of(jb * KB, KB)
            parts = []
            for c in range(TILES):
                blk = key_ref[pl.ds(ks + c * LANES, LANES), :]
                parts.append(_fold8(pred(blk, ks + c * LANES).astype(I32)))
            return acc + _tree(jnp.add, parts)
        acc = lax.fori_loop(0, nkb, body, jnp.zeros((SUBLANES, nq), I32))
        return jnp.sum(acc.astype(F32), axis=0, keepdims=True).astype(I32)

    def thr_step(i, t):
        cand = t + lax.shift_left(jnp.int32(1), 31 - i)
        cnt = count(lambda blk, base: blk >= cand)
        return jnp.where(cnt >= topk, cand, t)

    thr = lax.fori_loop(0, 32, thr_step, jnp.full((1, nq), INT_MIN, I32))

    want = (topk - count(lambda blk, base: blk > thr)).astype(F32)
    live = thr > INT_MIN
    tri_ref[...] = (lax.broadcasted_iota(I32, (KB, KB), 0) > lax.broadcasted_iota(I32, (KB, KB), 1)
                    ).astype(F32).astype(BF16)

    def mask_block(jb, seen):
        ks = pl.multiple_of(jb * KB, KB)
        keys = [key_ref[pl.ds(ks + c * LANES, LANES), :] for c in range(TILES)]
        ties = [((k == thr) & live).astype(F32) for k in keys]
        before = _dot(tri_ref[...], jnp.concatenate(ties, axis=0).astype(BF16)) + seen
        for c in range(TILES):
            keep = (keys[c] > thr) | ((ties[c] > 0.0) & (before[c * LANES:(c + 1) * LANES] < want))
            mask_ref[pl.ds(ks + c * LANES, LANES), :] = jnp.where(keep, 0.0, NEG)
        return seen + jnp.sum(_fold8(_tree(jnp.add, ties)), axis=0, keepdims=True)

    lax.fori_loop(0, nkb, mask_block, jnp.zeros((1, nq), F32))

    def head_step(s, msks, biases, m):
        ss = [s[c * LANES:(c + 1) * LANES] + msks[c] for c in range(TILES)]
        if biases is not None:
            ss = [x + b for x, b in zip(ss, biases)]
        m_new = jnp.maximum(m, jnp.max(_tree(jnp.maximum, ss), axis=0, keepdims=True))
        return m_new, jnp.exp2(m - m_new), jnp.concatenate([jnp.exp2(x - m_new) for x in ss], axis=0).astype(BF16)

    vrow_low = lax.broadcasted_iota(I32, (LANES, KB), 0) < HEAD_DIM

    for p0 in range(0, N_PAIRS, PAIRS_PER_LOOP):
        pairs = tuple(range(p0, p0 + PAIRS_PER_LOOP))
        for p in pairs:
            qp = q_ref[0, :, p * LANES:(p + 1) * LANES].astype(F32)
            qs_ref[2 * p] = jnp.where(low, qp, 0.0).astype(BF16)
            qs_ref[2 * p + 1] = jnp.where(low, 0.0, qp).astype(BF16)

        def attn_block(jb, carry, with_bias):
            ks = pl.multiple_of(jb * KB, KB)
            msks = [mask_ref[pl.ds(ks + c * LANES, LANES), :] for c in range(TILES)]
            tiles = [jnp.clip(jb * TILES + c - jd + 2, 0, N_BIAS_TILES - 1) for c in range(TILES)]

            def logits(head):
                g = head // GROUP
                kblk = kk_ref[0, pl.ds(ks, KB), g * LANES:(g + 1) * LANES]
                return _dot_nt(kblk, qs_ref[head])

            heads = [2 * p + r for p in pairs for r in range(2)]
            pending = [logits(h) for h in heads[:LOOKAHEAD]]
            vts = {}
            out = []
            for i, p in enumerate(pairs):
                g = (2 * p) // GROUP
                if g not in vts:
                    vt = vvt_ref[0, jb, g * LANES:(g + 1) * LANES, :]
                    one = jnp.ones_like(vt)
                    vts[g] = (jnp.where(vrow_low, vt, one), jnp.where(vrow_low, one, vt))
                ma, la, mb, lb = carry[i]
                stats = []
                for r, m in enumerate((ma, mb)):
                    nxt = 2 * i + r + LOOKAHEAD
                    if nxt < len(heads):
                        pending.append(logits(heads[nxt]))
                    biases = [bt_ref[tiles[c], 2 * p + r] for c in range(TILES)] if with_bias else None
                    stats.append(head_step(pending.pop(0), msks, biases, m))
                (ma, aa, pa), (mb, ab, pb) = stats
                pva = _dot(vts[g][0], pa)
                pvb = _dot(vts[g][1], pb)
                la = aa * la + pva[HEAD_DIM:HEAD_DIM + 1]
                lb = ab * lb + pvb[0:1]
                acc_ref[p] = jnp.where(rowlow, aa, ab) * acc_ref[p] + jnp.where(rowlow, pva, pvb)
                out.append((ma, la, mb, lb))
            return tuple(out)

        for p in pairs:
            acc_ref[p] = jnp.zeros((LANES, nq), F32)
        init = (jnp.full((1, nq), NEG, F32), jnp.zeros((1, nq), F32),
                jnp.full((1, nq), NEG, F32), jnp.zeros((1, nq), F32))
        nfar = jnp.clip((jd - 1) // TILES, 0, nkb)
        res = lax.fori_loop(0, nfar, functools.partial(attn_block, with_bias=False), (init,) * len(pairs))
        res = lax.fori_loop(nfar, nkb, functools.partial(attn_block, with_bias=True), res)
        for i, p in enumerate(pairs):
            _, la, _, lb = res[i]
            o = (acc_ref[p] / jnp.where(rowlow, la, lb)).T
            sl = slice(p * LANES, (p + 1) * LANES)
            merged_ref[:, sl] = (conv_ref[0, :, sl] + sgb_ref[0, :, sl] * o).astype(BF16)

    y = alpha * x_ref[0] + _dot(merged_ref[...], wout_ref[...])
    out_ref[0] = _layer_norm(y, g_ref[...], b_ref[...])


def _attention(q, qi, kiwi, conv, sgb, x, kk, vvt, kiki, bias_tiles, w_out, g, b_ln, *, alpha, nq, q_off, kv_len,
               topk):
    b, t, _ = q.shape
    lpad = kk.shape[1]
    assert q_off % LANES == 0 and nq % LANES == 0 and t % nq == 0
    assert lpad % KB == 0 and lpad >= q_off + t and vvt.shape == (b, lpad // KB, 4 * LANES, KB)
    grid = (b, t // nq)
    row = lambda n: pl.BlockSpec((1, nq, n), lambda i, j: (i, j, 0))
    keys = lambda n: pl.BlockSpec((1, lpad, n), lambda i, j: (i, 0, 0), pipeline_mode=pl.Buffered(1))
    const = lambda a: pl.BlockSpec(a.shape, lambda i, j: (0,) * a.ndim, pipeline_mode=pl.Buffered(1))
    return pl.pallas_call(
        functools.partial(_attn_kernel, alpha=alpha, nq=nq, q_off=q_off, kv_len=kv_len, topk=topk),
        grid=grid,
        in_specs=[row(D_MODEL), row(IDX_HEADS * IDX_DIM), row(LANES), row(D_MODEL), row(D_MODEL), row(D_MODEL),
                  keys(4 * LANES),
                  pl.BlockSpec((1, lpad // KB, 4 * LANES, KB), lambda i, j: (i, 0, 0, 0),
                               pipeline_mode=pl.Buffered(1)),
                  keys(LANES), const(bias_tiles), const(w_out), const(g), const(b_ln)],
        out_specs=row(D_MODEL),
        out_shape=jax.ShapeDtypeStruct((b, t, D_MODEL), F32),
        scratch_shapes=[pltpu.VMEM((lpad, nq), I32), pltpu.VMEM((lpad, nq), F32),
                        pltpu.VMEM((IDX_HEADS, nq, LANES), BF16), pltpu.VMEM((KB, KB), BF16),
                        pltpu.VMEM((N_HEADS, nq, LANES), BF16), pltpu.VMEM((N_PAIRS, LANES, nq), F32),
                        pltpu.VMEM((nq, D_MODEL), BF16)],
        compiler_params=pltpu.CompilerParams(dimension_semantics=("arbitrary", "arbitrary"),
                                             vmem_limit_bytes=VMEM_LIMIT),
        name="dsa_attention",
    )(q, qi, kiwi, conv, sgb, x, kk, vvt, kiki, bias_tiles, w_out, g, b_ln)


def _ffn_kernel(x_ref, wg_ref, wu_ref, wd_ref, g_ref, b_ref, out_ref, *, alpha):
    x = x_ref[...]
    xb = x.astype(BF16)
    hg = _dot(xb, wg_ref[...])
    h = (hg * jax.nn.sigmoid(hg)) * _dot(xb, wu_ref[...])
    f = _dot(h.astype(BF16), wd_ref[...])
    out_ref[...] = _layer_norm(alpha * x + f, g_ref[...], b_ref[...])


def _ffn_ln(x, wg, wu, wd, g, b, alpha, tm):
    n = x.shape[0]
    row = pl.BlockSpec((tm, D_MODEL), lambda i: (i, 0))
    vec = pl.BlockSpec((1, D_MODEL), lambda i: (0, 0))
    full = lambda a: pl.BlockSpec(a.shape, lambda i: (0, 0), pipeline_mode=pl.Buffered(1))
    return pl.pallas_call(
        functools.partial(_ffn_kernel, alpha=alpha),
        grid=(n // tm,),
        in_specs=[row, full(wg), full(wu), full(wd), vec, vec],
        out_specs=row,
        out_shape=jax.ShapeDtypeStruct((n, D_MODEL), F32),
        compiler_params=pltpu.CompilerParams(dimension_semantics=("arbitrary",), vmem_limit_bytes=VMEM_LIMIT),
        name="ffn_ln",
    )(x, wg, wu, wd, g, b)


def _moe_kernel(x_ref, rw_ref, rb_ref, wgu_ref, wd_ref, g_ref, b_ref, out_ref,
                rank_ref, rankt_ref, combt_ref, xb_ref, acc_ref, *, alpha, tm):
    e = pl.program_id(1)
    sub = min(MOE_SUB, tm)
    assert sub & (sub - 1) == 0 and tm % sub == 0
    ch = min(MOE_CHUNK, sub)
    lane = lax.broadcasted_iota(I32, (tm, LANES), 1)

    @pl.when(e == 0)
    def _():
        logits = jnp.dot(x_ref[...], rw_ref[...], preferred_element_type=F32,
                         precision=lax.Precision.HIGHEST) + rb_ref[...]
        logits = jnp.where(lane < N_EXPERTS, logits, -jnp.inf)
        lanef = lane.astype(F32)
        v1 = jnp.max(logits, axis=1, keepdims=True)
        i1 = jnp.min(jnp.where(logits == v1, lanef, float(LANES)), axis=1, keepdims=True)
        rest = jnp.where(lanef == i1, -jnp.inf, logits)
        v2 = jnp.max(rest, axis=1, keepdims=True)
        i2 = jnp.min(jnp.where(rest == v2, lanef, float(LANES)), axis=1, keepdims=True)
        e2 = jnp.exp(v2 - v1)
        den = 1.0 + e2
        comb = jnp.where(lanef == i1, 1.0 / den, 0.0) + jnp.where(lanef == i2, e2 / den, 0.0)
        routed = (lanef == i1) | (lanef == i2)
        tr = lax.broadcasted_iota(I32, (tm, tm), 0)
        tc = lax.broadcasted_iota(I32, (tm, tm), 1)
        earlier = (tr > tc) & (tc >= tr - (tr & (sub - 1)))
        rank = _dot(earlier.astype(F32).astype(BF16), routed.astype(F32).astype(BF16))
        rank = jnp.where(routed, rank, -1.0)
        rank_ref[...] = rank.astype(I32)
        rankt_ref[...] = rank.T.astype(I32)
        combt_ref[...] = comb.T
        xb_ref[...] = x_ref[...].astype(BF16)
        acc_ref[...] = jnp.zeros((tm, D_MODEL), F32)

    rcol = jnp.max(jnp.where(lane == e, rank_ref[...], -1).astype(F32), axis=1, keepdims=True).astype(I32)
    rrow = rankt_ref[pl.ds(e, 1), :]
    crow = combt_ref[pl.ds(e, 1), :]

    for s0 in range(0, tm, sub):
        rcol_s, rrow_s, crow_s = rcol[s0:s0 + sub], rrow[:, s0:s0 + sub], crow[:, s0:s0 + sub]
        n_routed = jnp.max(rcol_s.astype(F32)).astype(I32) + 1

        def chunk(c, carry):
            base = c * ch
            hit = rrow_s == lax.broadcasted_iota(I32, (ch, sub), 0) + base
            xg = _dot(jnp.where(hit, 1.0, 0.0).astype(BF16), xb_ref[s0:s0 + sub, :]).astype(BF16)
            hgu = _dot(xg, wgu_ref[...])
            hg = hgu[:, :D_FF_EXPERT]
            h = (hg * jax.nn.sigmoid(hg)) * hgu[:, D_FF_EXPERT:]
            f = _dot(h.astype(BF16), wd_ref[...])
            gate = jnp.sum(jnp.where(hit, crow_s, 0.0), axis=1, keepdims=True)
            hit_t = rcol_s == lax.broadcasted_iota(I32, (sub, ch), 1) + base
            acc_ref[s0:s0 + sub, :] += _dot(jnp.where(hit_t, 1.0, 0.0).astype(BF16), (f * gate).astype(BF16))
            return carry

        lax.fori_loop(0, (n_routed + ch - 1) // ch, chunk, 0)

    @pl.when(e == N_EXPERTS - 1)
    def _():
        out_ref[...] = _layer_norm(alpha * x_ref[...] + acc_ref[...], g_ref[...], b_ref[...])


def _moe_ln(x, rw, rb, wgu, wd, g, b, alpha, tm):
    n = x.shape[0]
    row = pl.BlockSpec((tm, D_MODEL), lambda i, e: (i, 0))
    vec = pl.BlockSpec((1, D_MODEL), lambda i, e: (0, 0))
    return pl.pallas_call(
        functools.partial(_moe_kernel, alpha=alpha, tm=tm),
        grid=(n // tm, N_EXPERTS),
        in_specs=[row, pl.BlockSpec((D_MODEL, LANES), lambda i, e: (0, 0)),
                  pl.BlockSpec((1, LANES), lambda i, e: (0, 0)),
                  pl.BlockSpec((None, D_MODEL, 2 * D_FF_EXPERT), lambda i, e: (e, 0, 0)),
                  pl.BlockSpec((None, D_FF_EXPERT, D_MODEL), lambda i, e: (e, 0, 0)),
                  vec, vec],
        out_specs=row,
        out_shape=jax.ShapeDtypeStruct((n, D_MODEL), F32),
        scratch_shapes=[pltpu.VMEM((tm, LANES), I32), pltpu.VMEM((LANES, tm), I32), pltpu.VMEM((LANES, tm), F32),
                        pltpu.VMEM((tm, D_MODEL), BF16), pltpu.VMEM((tm, D_MODEL), F32)],
        compiler_params=pltpu.CompilerParams(dimension_semantics=("arbitrary", "arbitrary"),
                                             vmem_limit_bytes=VMEM_LIMIT),
        name="moe_ln",
    )(x, rw, rb, wgu, wd, g, b)


def _split_w_in(w):
    sizes = (D_MODEL, D_MODEL, D_MODEL, N_HEADS * HEAD_DIM, N_KV_HEADS * HEAD_DIM, N_KV_HEADS * HEAD_DIM,
             IDX_HEADS * IDX_DIM, IDX_DIM, IDX_HEADS, D_MODEL, D_MODEL)
    parts, start = [], 0
    for n in sizes:
        parts.append(w[:, start:start + n])
        start += n
    bg, cg, xin, q, k, v, qi, ki, wi, ga, gb = parts

    def dup(a):
        a = a.reshape(a.shape[0], -1, HEAD_DIM)
        return jnp.concatenate([a, a], axis=-1).reshape(a.shape[0], -1)

    kiwi = jnp.concatenate([ki, wi, jnp.zeros((w.shape[0], LANES - IDX_DIM - IDX_HEADS), w.dtype)], axis=1)
    return {
        'a': jnp.concatenate([bg, cg, xin, ga], axis=1).astype(BF16),
        'q': q.astype(BF16), 'gb': gb.astype(BF16), 'kk': dup(k).astype(BF16), 'vv': dup(v).astype(BF16),
        'qi': qi.astype(BF16), 'ki': jnp.concatenate([ki, ki, kiwi], axis=1).astype(BF16),
    }


def _dup_cache(a):
    a = a.astype(BF16)
    return jnp.concatenate([a, a], axis=-1).reshape(a.shape[0], a.shape[1], -1)


def _pad_rows(a, n):
    return jnp.pad(a, ((0, 0), (0, n - a.shape[1]), (0, 0)))


def _row_tile(n, cap):
    tm = min(n, cap)
    assert n % tm == 0
    return tm


def kernel(x_prompt, x_sample, cache_k, cache_v, cache_kidx, state_conv, w_in, conv_w, w_out, rel_bias,
           ln1_g, ln1_b, ln2_g, ln2_b, ffn_w_gate, ffn_w_up, ffn_w_down, router_w, router_b,
           moe_w_gate, moe_w_up, moe_w_down):
    depth = w_in.shape[0]
    alpha = (2 * depth) ** 0.25
    bp, tp, _ = x_prompt.shape
    bs, ts, _ = x_sample.shape
    past = cache_k.shape[2]
    nq_p = 256 if tp % 256 == 0 else LANES
    nq_s = LANES
    assert tp % KB == 0 and past % LANES == 0 and ts <= nq_s and ts % SUBLANES == 0
    topk_p = min(TOPK_MAX, tp // 4)
    topk_s = min(TOPK_MAX, (past + ts) // 4)
    lpad_s = -(-(past + nq_s) // KB) * KB

    bt_p = _bias_tiles(rel_bias, nq_p)
    bt_s = _bias_tiles(rel_bias, nq_s)
    vec = lambda a: a.reshape(1, -1)

    def layer(l, x, prev8, past_kv):
        b, t, _ = x.shape
        w = _split_w_in(w_in[l])
        prompt = past_kv is None
        (conv, q, sgb, kkf, kkb, vvf, vvb, qi, kiki, kiwi, cstate) = _inproj(
            x, prev8, w, conv_w[l], _row_tile(t, KB), transpose_v=prompt)
        proj = (w_out[l].astype(BF16), vec(ln1_g[l]), vec(ln1_b[l]))
        if prompt:
            x1 = _attention(q, qi, kiwi, conv, sgb, x, kkb, vvb, kiki, bt_p, *proj,
                            alpha=alpha, nq=nq_p, q_off=0, kv_len=t, topk=topk_p)
        else:
            ck, cv, cki = past_kv
            kk_all = _pad_rows(jnp.concatenate([_dup_cache(ck), kkb], axis=1), lpad_s)
            vv_all = _pad_rows(jnp.concatenate([_dup_cache(cv), vvb], axis=1), lpad_s)
            vvt_all = jnp.swapaxes(vv_all.reshape(b, lpad_s // KB, KB, 4 * LANES), 2, 3)
            cki = cki.astype(BF16)
            kiki_all = _pad_rows(jnp.concatenate([jnp.concatenate([cki, cki], axis=-1), kiki], axis=1), lpad_s)
            pad = lambda a: _pad_rows(a, nq_s)
            x1 = _attention(pad(q), pad(qi), pad(kiwi), pad(conv), pad(sgb), pad(x), kk_all, vvt_all, kiki_all, bt_s,
                            *proj, alpha=alpha, nq=nq_s, q_off=past, kv_len=past + t, topk=topk_s)[:, :t]
        n = b * t
        x1 = x1.reshape(n, D_MODEL)
        j = l // 2
        if l % 2 == 0:
            x2 = _ffn_ln(x1, ffn_w_gate[j].astype(BF16), ffn_w_up[j].astype(BF16), ffn_w_down[j].astype(BF16),
                         vec(ln2_g[l]), vec(ln2_b[l]), alpha, _row_tile(n, 512))
        else:
            rw = jnp.pad(router_w[j], ((0, 0), (0, LANES - N_EXPERTS)))
            rb = jnp.pad(router_b[j], (0, LANES - N_EXPERTS)).reshape(1, LANES)
            wgu = jnp.concatenate([moe_w_gate[j], moe_w_up[j]], axis=-1).astype(BF16)
            x2 = _moe_ln(x1, rw, rb, wgu, moe_w_down[j].astype(BF16), vec(ln2_g[l]), vec(ln2_b[l]), alpha,
                         _row_tile(n, 2 * MOE_SUB))
        heads = lambda a: a.reshape(b, t, N_KV_HEADS, HEAD_DIM)
        return (x2.reshape(b, t, D_MODEL), heads(kkf), heads(vvf), kiwi[..., :IDX_DIM],
                cstate[:, SUBLANES - (CONV_W - 1):, :])

    xp, xs = x_prompt, x_sample
    outs_p, outs_s = [], []
    zero_prev = jnp.zeros((bp, SUBLANES, D_MODEL), F32)
    for l in range(depth):
        xp, *rest = layer(l, xp, zero_prev, None)
        outs_p.append(rest)
        prev8 = jnp.pad(state_conv[l], ((0, 0), (SUBLANES - (CONV_W - 1), 0), (0, 0)))
        xs, *rest = layer(l, xs, prev8, (cache_k[l], cache_v[l], cache_kidx[l]))
        outs_s.append(rest)
    stack = lambda outs, i: jnp.stack([o[i] for o in outs])
    return (xp, xs,
            stack(outs_p, 0), stack(outs_p, 1), stack(outs_p, 2), stack(outs_p, 3),
            stack(outs_s, 0), stack(outs_s, 1), stack(outs_s, 2), stack(outs_s, 3))
```

```python
import functools

import jax
import jax.numpy as jnp
from jax import lax
from jax.experimental import pallas as pl
from jax.experimental.pallas import tpu as pltpu

D_MODEL = 1024
CHUNK = 64
CHUNK_SHIFT = 6
CONV_W = 3
N_HEADS = 16
HEAD_DIM = 64
N_KV_HEADS = 4
GROUP = N_HEADS // N_KV_HEADS
IDX_HEADS = 8
IDX_DIM = 64
TOPK_MAX = 256
NUM_BUCKETS = 32
MAX_DISTANCE = 128
D_FF = 2816
N_EXPERTS = 8
D_FF_EXPERT = 1408
LN_EPS = 1e-5

LANES = 128
SUBLANES = 8
N_PAIRS = N_HEADS // 2
KB = 512
TILES = KB // LANES
N_BIAS_TILES = 4
PAIRS_PER_LOOP = 8
LOOKAHEAD = 3
ONES_ROWS = 16
MOE_SUB = 512
MOE_CHUNK = 160
NEG = -1e30
LOG2E = 1.4426950408889634
INT_MIN = -2 ** 31
VMEM_LIMIT = 56 * 1024 * 1024

F32 = jnp.float32
BF16 = jnp.bfloat16
I32 = jnp.int32

assert CHUNK == 1 << CHUNK_SHIFT and 2 * HEAD_DIM == LANES and 2 * IDX_DIM == LANES


def _dot(a, b):
    return jnp.dot(a, b, preferred_element_type=F32)


def _dot_nt(a, b):
    return lax.dot_general(a, b, (((1,), (1,)), ((), ())), preferred_element_type=F32)


def _layer_norm(y, g, b):
    mu = jnp.mean(y, axis=-1, keepdims=True)
    d = y - mu
    var = jnp.mean(d * d, axis=-1, keepdims=True)
    return d * lax.rsqrt(var + LN_EPS) * g + b


def _tree(op, xs):
    xs = list(xs)
    while len(xs) > 1:
        xs = [op(xs[i], xs[i + 1]) for i in range(0, len(xs) - 1, 2)] + ([xs[-1]] if len(xs) % 2 else [])
    return xs[0]


def _fold8(x):
    return _tree(jnp.add, [x[i:i + SUBLANES] for i in range(0, x.shape[0], SUBLANES)])


def _inproj_kernel(x_ref, prev_ref, wa_ref, wq_ref, wgb_ref, wkk_ref, wvv_ref, wqi_ref, wki_ref, cw_ref,
                   conv_ref, q_ref, sgb_ref, kkf_ref, kkb_ref, vvf_ref, vvb_ref, qi_ref, kiki_ref, kiwi_ref,
                   cstate_ref, uext_ref, *, tm, transpose_v):
    @pl.when(pl.program_id(1) == 0)
    def _():
        uext_ref[0:SUBLANES, :] = prev_ref[0]

    xb = x_ref[0].astype(BF16)
    cc = 256
    for c in range(0, D_MODEL, cc):
        bg = _dot(xb, wa_ref[:, c:c + cc])
        cg = _dot(xb, wa_ref[:, D_MODEL + c:D_MODEL + c + cc])
        xin = _dot(xb, wa_ref[:, 2 * D_MODEL + c:2 * D_MODEL + c + cc])
        ga = _dot(xb, wa_ref[:, 3 * D_MODEL + c:3 * D_MODEL + c + cc])
        u = cg * xin
        uext_ref[SUBLANES:SUBLANES + tm, c:c + cc] = u
        um1 = uext_ref[SUBLANES - 1:SUBLANES - 1 + tm, c:c + cc]
        um2 = uext_ref[SUBLANES - 2:SUBLANES - 2 + tm, c:c + cc]
        y = um2 * cw_ref[0:1, c:c + cc] + um1 * cw_ref[1:2, c:c + cc] + u * cw_ref[2:3, c:c + cc]
        conv_ref[0, :, c:c + cc] = jax.nn.sigmoid(ga) * (bg * y)
    tail = uext_ref[tm:tm + SUBLANES, :]
    cstate_ref[0] = tail
    uext_ref[0:SUBLANES, :] = tail

    q_ref[0] = (_dot(xb, wq_ref[...]) * (HEAD_DIM ** -0.5 * LOG2E)).astype(BF16)
    sgb_ref[0] = jax.nn.sigmoid(_dot(xb, wgb_ref[...]))
    def undup(a):
        low = lax.broadcasted_iota(I32, (tm, LANES), 1) < HEAD_DIM
        return jnp.concatenate(
            [jnp.where(low, a[:, 2 * j * LANES:(2 * j + 1) * LANES], a[:, (2 * j + 1) * LANES:(2 * j + 2) * LANES])
             for j in range(N_KV_HEADS // 2)], axis=1)

    kk = _dot(xb, wkk_ref[...])
    kkf_ref[0] = undup(kk)
    kkb_ref[0] = kk.astype(BF16)
    vv = _dot(xb, wvv_ref[...])
    vvf_ref[0] = vv
    if transpose_v:
        vvb_ref[0, 0] = vv.T.astype(BF16)
    else:
        vvb_ref[0] = vv.astype(BF16)
    qi_ref[0] = _dot(xb, wqi_ref[...]).astype(BF16)
    kw = _dot(xb, wki_ref[...])
    kiki_ref[0] = kw[:, 0:LANES].astype(BF16)
    kiwi_ref[0] = kw[:, LANES:2 * LANES]


def _inproj(x, prev8, w, conv_w, tm, transpose_v):
    b, t, _ = x.shape
    grid = (b, t // tm)
    row = lambda n: pl.BlockSpec((1, tm, n), lambda i, j: (i, j, 0))
    full = lambda a: pl.BlockSpec(a.shape, lambda i, j: (0,) * a.ndim, pipeline_mode=pl.Buffered(1))
    if transpose_v:
        assert tm == KB
        vvb_shape = jax.ShapeDtypeStruct((b, t // KB, N_KV_HEADS * HEAD_DIM, KB), BF16)
        vvb_spec = pl.BlockSpec((1, 1, N_KV_HEADS * HEAD_DIM, KB), lambda i, j: (i, j, 0, 0))
    else:
        vvb_shape = jax.ShapeDtypeStruct((b, t, N_KV_HEADS * HEAD_DIM), BF16)
        vvb_spec = row(N_KV_HEADS * HEAD_DIM)
    out_shapes = (
        jax.ShapeDtypeStruct((b, t, D_MODEL), F32),
        jax.ShapeDtypeStruct((b, t, D_MODEL), BF16),
        jax.ShapeDtypeStruct((b, t, D_MODEL), F32),
        jax.ShapeDtypeStruct((b, t, N_KV_HEADS * HEAD_DIM), F32),
        jax.ShapeDtypeStruct((b, t, 4 * LANES), BF16),
        jax.ShapeDtypeStruct((b, t, N_KV_HEADS * HEAD_DIM), F32),
        vvb_shape,
        jax.ShapeDtypeStruct((b, t, IDX_HEADS * IDX_DIM), BF16),
        jax.ShapeDtypeStruct((b, t, LANES), BF16),
        jax.ShapeDtypeStruct((b, t, LANES), F32),
        jax.ShapeDtypeStruct((b, SUBLANES, D_MODEL), F32),
    )
    out_specs = (row(D_MODEL), row(D_MODEL), row(D_MODEL), row(N_KV_HEADS * HEAD_DIM), row(4 * LANES),
                 row(N_KV_HEADS * HEAD_DIM), vvb_spec, row(IDX_HEADS * IDX_DIM), row(LANES), row(LANES),
                 pl.BlockSpec((1, SUBLANES, D_MODEL), lambda i, j: (i, 0, 0)))
    in_specs = [row(D_MODEL), pl.BlockSpec((1, SUBLANES, D_MODEL), lambda i, j: (i, 0, 0)),
                full(w['a']), full(w['q']), full(w['gb']), full(w['kk']), full(w['vv']), full(w['qi']),
                full(w['ki']), full(conv_w)]
    return pl.pallas_call(
        functools.partial(_inproj_kernel, tm=tm, transpose_v=transpose_v),
        grid=grid, in_specs=in_specs, out_specs=out_specs, out_shape=out_shapes,
        scratch_shapes=[pltpu.VMEM((tm + SUBLANES, D_MODEL), F32)],
        compiler_params=pltpu.CompilerParams(dimension_semantics=("arbitrary", "arbitrary"),
                                             vmem_limit_bytes=VMEM_LIMIT),
        name="inproj_conv",
    )(x, prev8, w['a'], w['q'], w['gb'], w['kk'], w['vv'], w['qi'], w['ki'], conv_w)


def _bias_tiles_kernel(tbl_ref, out_ref, *, nq):
    d = pl.program_id(0) - 2
    krow = lax.broadcasted_iota(I32, (LANES, nq), 0)
    qcol = lax.broadcasted_iota(I32, (LANES, nq), 1)
    rel = d * LANES + krow - qcol
    n = jnp.abs(rel)
    nb = NUM_BUCKETS // 2
    max_exact = nb // 2
    n2 = n * n
    large = jnp.full((LANES, nq), max_exact, I32)
    for j in range(1, nb - max_exact):
        large = large + (n2 >= (max_exact * max_exact) * (1 << j)).astype(I32)
    bucket = jnp.where(n < max_exact, n, large) + jnp.where(rel > 0, nb, 0)
    far = pl.program_id(0) == 0
    for h in range(N_HEADS):
        base = tbl_ref[nb - 1, h]
        acc = jnp.zeros((LANES, nq), F32)
        for bk in range(NUM_BUCKETS):
            acc = jnp.where(bucket == bk, (tbl_ref[bk, h] - base) * LOG2E, acc)
        out_ref[0, h] = jnp.where(far, 0.0, acc)


def _bias_tiles(rel_bias, nq):
    assert MAX_DISTANCE == 128 and NUM_BUCKETS == 32
    return pl.pallas_call(
        functools.partial(_bias_tiles_kernel, nq=nq),
        grid=(N_BIAS_TILES,),
        in_specs=[pl.BlockSpec(memory_space=pltpu.SMEM)],
        out_specs=pl.BlockSpec((1, N_HEADS, LANES, nq), lambda i: (i, 0, 0, 0)),
        out_shape=jax.ShapeDtypeStruct((N_BIAS_TILES, N_HEADS, LANES, nq), F32),
        name="bias_tiles",
    )(rel_bias)


def _attn_kernel(q_ref, qi_ref, kiwi_ref, conv_ref, sgb_ref, x_ref, kk_ref, vvt_ref, kiki_ref, bt_ref,
                 wout_ref, g_ref, b_ref, out_ref,
                 key_ref, mask_ref, qis_ref, tri_ref, qs_ref, acc_ref, merged_ref, *, alpha, nq, q_off, kv_len, topk):
    q0 = q_off + pl.program_id(1) * nq
    nkb = (q0 + nq + KB - 1) // KB
    jd = q0 // LANES
    lane = lax.broadcasted_iota(I32, (nq, LANES), 1)
    low = lane < HEAD_DIM
    krow = lax.broadcasted_iota(I32, (LANES, nq), 0)
    qchunk = lax.shift_right_arithmetic(q0 + lax.broadcasted_iota(I32, (LANES, nq), 1), CHUNK_SHIFT)

    wit = kiwi_ref[0].T * ((IDX_DIM ** -0.5) * (IDX_HEADS ** -0.5))
    wis = [wit[IDX_DIM + h:IDX_DIM + h + 1, :] for h in range(IDX_HEADS)]
    for h in range(IDX_HEADS):
        pair = qi_ref[0, :, (h // 2) * LANES:(h // 2 + 1) * LANES].astype(F32)
        keep = low if h % 2 == 0 else jnp.logical_not(low)
        qis_ref[h] = jnp.where(keep, pair, 0.0).astype(BF16)

    def score_block(jb, carry):
        ks = pl.multiple_of(jb * KB, KB)
        ki = kiki_ref[0, pl.ds(ks, KB), :]
        sc = [jnp.zeros((LANES, nq), F32) for _ in range(TILES)]
        for h in range(IDX_HEADS):
            s = _dot_nt(ki, qis_ref[h])
            for c in range(TILES):
                sc[c] = sc[c] + jnp.maximum(s[c * LANES:(c + 1) * LANES], 0.0) * wis[h]
        for c in range(TILES):
            bits = lax.bitcast_convert_type(sc[c], I32)
            key = jnp.where(bits < 0, bits ^ 0x7FFFFFFF, bits)
            kpos = ks + c * LANES + krow
            adm = (lax.shift_right_arithmetic(kpos, CHUNK_SHIFT) <= qchunk) & (kpos < kv_len)
            key_ref[pl.ds(ks + c * LANES, LANES), :] = jnp.where(adm, key, INT_MIN)
        return carry

    lax.fori_loop(0, nkb, score_block, 0)

    def count(pred):
        def body(jb, acc):
            ks = pl.multiple_of(jb * KB, KB)
            parts = []
            for c in range(TILES):
                blk = key_ref[pl.ds(ks + c * LANES, LANES), :]
                parts.append(_fold8(pred(blk, ks + c * LANES).astype(I32)))
            return acc + _tree(jnp.add, parts)
        acc = lax.fori_loop(0, nkb, body, jnp.zeros((SUBLANES, nq), I32))
        return jnp.sum(acc.astype(F32), axis=0, keepdims=True).astype(I32)

    def thr_step(i, t):
        cand = t + lax.shift_left(jnp.int32(1), 31 - i)
        cnt = count(lambda blk, base: blk >= cand)
        return jnp.where(cnt >= topk, cand, t)

    thr = lax.fori_loop(0, 32, thr_step, jnp.full((1, nq), INT_MIN, I32))

    want = (topk - count(lambda blk, base: blk > thr)).astype(F32)
    live = thr > INT_MIN
    tri_ref[...] = (lax.broadcasted_iota(I32, (KB, KB), 0) > lax.broadcasted_iota(I32, (KB, KB), 1)
                    ).astype(F32).astype(BF16)

    def mask_block(jb, seen):
        ks = pl.multiple_of(jb * KB, KB)
        keys = [key_ref[pl.ds(ks + c * LANES, LANES), :] for c in range(TILES)]
        ties = [((k == thr) & live).astype(F32) for k in keys]
        before = _dot(tri_ref[...], jnp.concatenate(ties, axis=0).astype(BF16)) + seen
        for c in range(TILES):
            keep = (keys[c] > thr) | ((ties[c] > 0.0) & (before[c * LANES:(c + 1) * LANES] < want))
            mask_ref[pl.ds(ks + c * LANES, LANES), :] = jnp.where(keep, 0.0, NEG)
        return seen + jnp.sum(_fold8(_tree(jnp.add, ties)), axis=0, keepdims=True)

    lax.fori_loop(0, nkb, mask_block, jnp.zeros((1, nq), F32))

    def head_step(s, msks, biases, m):
        ss = [s[c * LANES:(c + 1) * LANES] + msks[c] for c in range(TILES)]
        if biases is not None:
            ss = [x + b for x, b in zip(ss, biases)]
        m_new = jnp.maximum(m, jnp.max(_tree(jnp.maximum, ss), axis=0, keepdims=True))
        return m_new, jnp.exp2(m - m_new), jnp.concatenate([jnp.exp2(x - m_new) for x in ss], axis=0).astype(BF16)

    for p0 in range(0, N_PAIRS, PAIRS_PER_LOOP):
        pairs = tuple(range(p0, p0 + PAIRS_PER_LOOP))
        for p in pairs:
            qp = q_ref[0, :, p * LANES:(p + 1) * LANES].astype(F32)
            qs_ref[2 * p] = jnp.where(low, qp, 0.0).astype(BF16)
            qs_ref[2 * p + 1] = jnp.where(low, 0.0, qp).astype(BF16)

        def attn_block(jb, carry, with_bias):
            ks = pl.multiple_of(jb * KB, KB)
            msks = [mask_ref[pl.ds(ks + c * LANES, LANES), :] for c in range(TILES)]
            tiles = [jnp.clip(jb * TILES + c - jd + 2, 0, N_BIAS_TILES - 1) for c in range(TILES)]

            def logits(head):
                g = head // GROUP
                kblk = kk_ref[0, pl.ds(ks, KB), g * LANES:(g + 1) * LANES]
                return _dot_nt(kblk, qs_ref[head])

            heads = [2 * p + r for p in pairs for r in range(2)]
            pending = [logits(h) for h in heads[:LOOKAHEAD]]
            vts = {}
            out = []
            for i, p in enumerate(pairs):
                g = (2 * p) // GROUP
                if g not in vts:
                    vts[g] = jnp.concatenate([vvt_ref[0, jb, g * HEAD_DIM:(g + 1) * HEAD_DIM, :],
                                              jnp.ones((ONES_ROWS, KB), BF16)], axis=0)
                stats = []
                for r, (m, l) in enumerate(carry[i]):
                    nxt = 2 * i + r + LOOKAHEAD
                    if nxt < len(heads):
                        pending.append(logits(heads[nxt]))
                    biases = [bt_ref[tiles[c], 2 * p + r] for c in range(TILES)] if with_bias else None
                    m, a, pr = head_step(pending.pop(0), msks, biases, m)
                    pv = _dot(vts[g], pr)
                    acc_ref[2 * p + r] = a * acc_ref[2 * p + r] + pv[:HEAD_DIM]
                    stats.append((m, a * l + pv[HEAD_DIM:HEAD_DIM + 1]))
                out.append(tuple(stats))
            return tuple(out)

        for p in pairs:
            acc_ref[2 * p] = jnp.zeros((HEAD_DIM, nq), F32)
            acc_ref[2 * p + 1] = jnp.zeros((HEAD_DIM, nq), F32)
        init = ((jnp.full((1, nq), NEG, F32), jnp.zeros((1, nq), F32)),) * 2
        nfar = jnp.clip((jd - 1) // TILES, 0, nkb)
        res = lax.fori_loop(0, nfar, functools.partial(attn_block, with_bias=False), (init,) * len(pairs))
        res = lax.fori_loop(nfar, nkb, functools.partial(attn_block, with_bias=True), res)
        for i, p in enumerate(pairs):
            (_, la), (_, lb) = res[i]
            o = jnp.concatenate([acc_ref[2 * p] / la, acc_ref[2 * p + 1] / lb], axis=0).T
            sl = slice(p * LANES, (p + 1) * LANES)
            merged_ref[:, sl] = (conv_ref[0, :, sl] + sgb_ref[0, :, sl] * o).astype(BF16)

    y = alpha * x_ref[0] + _dot(merged_ref[...], wout_ref[...])
    out_ref[0] = _layer_norm(y, g_ref[...], b_ref[...])


def _attention(q, qi, kiwi, conv, sgb, x, kk, vvt, kiki, bias_tiles, w_out, g, b_ln, *, alpha, nq, q_off, kv_len,
               topk):
    b, t, _ = q.shape
    lpad = kk.shape[1]
    assert q_off % LANES == 0 and nq % LANES == 0 and t % nq == 0
    assert lpad % KB == 0 and lpad >= q_off + t and vvt.shape == (b, lpad // KB, N_KV_HEADS * HEAD_DIM, KB)
    grid = (b, t // nq)
    row = lambda n: pl.BlockSpec((1, nq, n), lambda i, j: (i, j, 0))
    keys = lambda n: pl.BlockSpec((1, lpad, n), lambda i, j: (i, 0, 0), pipeline_mode=pl.Buffered(1))
    const = lambda a: pl.BlockSpec(a.shape, lambda i, j: (0,) * a.ndim, pipeline_mode=pl.Buffered(1))
    return pl.pallas_call(
        functools.partial(_attn_kernel, alpha=alpha, nq=nq, q_off=q_off, kv_len=kv_len, topk=topk),
        grid=grid,
        in_specs=[row(D_MODEL), row(IDX_HEADS * IDX_DIM), row(LANES), row(D_MODEL), row(D_MODEL), row(D_MODEL),
                  keys(4 * LANES),
                  pl.BlockSpec((1, lpad // KB, N_KV_HEADS * HEAD_DIM, KB), lambda i, j: (i, 0, 0, 0),
                               pipeline_mode=pl.Buffered(1)),
                  keys(LANES), const(bias_tiles), const(w_out), const(g), const(b_ln)],
        out_specs=row(D_MODEL),
        out_shape=jax.ShapeDtypeStruct((b, t, D_MODEL), F32),
        scratch_shapes=[pltpu.VMEM((lpad, nq), I32), pltpu.VMEM((lpad, nq), F32),
                        pltpu.VMEM((IDX_HEADS, nq, LANES), BF16), pltpu.VMEM((KB, KB), BF16),
                        pltpu.VMEM((N_HEADS, nq, LANES), BF16), pltpu.VMEM((N_HEADS, HEAD_DIM, nq), F32),
                        pltpu.VMEM((nq, D_MODEL), BF16)],
        compiler_params=pltpu.CompilerParams(dimension_semantics=("arbitrary", "arbitrary"),
                                             vmem_limit_bytes=VMEM_LIMIT),
        name="dsa_attention",
    )(q, qi, kiwi, conv, sgb, x, kk, vvt, kiki, bias_tiles, w_out, g, b_ln)


def _ffn_kernel(x_ref, wg_ref, wu_ref, wd_ref, g_ref, b_ref, out_ref, *, alpha):
    x = x_ref[...]
    xb = x.astype(BF16)
    hg = _dot(xb, wg_ref[...])
    h = (hg * jax.nn.sigmoid(hg)) * _dot(xb, wu_ref[...])
    f = _dot(h.astype(BF16), wd_ref[...])
    out_ref[...] = _layer_norm(alpha * x + f, g_ref[...], b_ref[...])


def _ffn_ln(x, wg, wu, wd, g, b, alpha, tm):
    n = x.shape[0]
    row = pl.BlockSpec((tm, D_MODEL), lambda i: (i, 0))
    vec = pl.BlockSpec((1, D_MODEL), lambda i: (0, 0))
    full = lambda a: pl.BlockSpec(a.shape, lambda i: (0, 0), pipeline_mode=pl.Buffered(1))
    return pl.pallas_call(
        functools.partial(_ffn_kernel, alpha=alpha),
        grid=(n // tm,),
        in_specs=[row, full(wg), full(wu), full(wd), vec, vec],
        out_specs=row,
        out_shape=jax.ShapeDtypeStruct((n, D_MODEL), F32),
        compiler_params=pltpu.CompilerParams(dimension_semantics=("arbitrary",), vmem_limit_bytes=VMEM_LIMIT),
        name="ffn_ln",
    )(x, wg, wu, wd, g, b)


def _moe_kernel(x_ref, rw_ref, rb_ref, wgu_ref, wd_ref, g_ref, b_ref, out_ref,
                rank_ref, rankt_ref, combt_ref, xb_ref, acc_ref, *, alpha, tm):
    e = pl.program_id(1)
    sub = min(MOE_SUB, tm)
    assert sub & (sub - 1) == 0 and tm % sub == 0
    ch = min(MOE_CHUNK, sub)
    lane = lax.broadcasted_iota(I32, (tm, LANES), 1)

    @pl.when(e == 0)
    def _():
        logits = jnp.dot(x_ref[...], rw_ref[...], preferred_element_type=F32,
                         precision=lax.Precision.HIGHEST) + rb_ref[...]
        logits = jnp.where(lane < N_EXPERTS, logits, -jnp.inf)
        lanef = lane.astype(F32)
        v1 = jnp.max(logits, axis=1, keepdims=True)
        i1 = jnp.min(jnp.where(logits == v1, lanef, float(LANES)), axis=1, keepdims=True)
        rest = jnp.where(lanef == i1, -jnp.inf, logits)
        v2 = jnp.max(rest, axis=1, keepdims=True)
        i2 = jnp.min(jnp.where(rest == v2, lanef, float(LANES)), axis=1, keepdims=True)
        e2 = jnp.exp(v2 - v1)
        den = 1.0 + e2
        comb = jnp.where(lanef == i1, 1.0 / den, 0.0) + jnp.where(lanef == i2, e2 / den, 0.0)
        routed = (lanef == i1) | (lanef == i2)
        tr = lax.broadcasted_iota(I32, (tm, tm), 0)
        tc = lax.broadcasted_iota(I32, (tm, tm), 1)
        earlier = (tr > tc) & (tc >= tr - (tr & (sub - 1)))
        rank = _dot(earlier.astype(F32).astype(BF16), routed.astype(F32).astype(BF16))
        rank = jnp.where(routed, rank, -1.0)
        rank_ref[...] = rank.astype(I32)
        rankt_ref[...] = rank.T.astype(I32)
        combt_ref[...] = comb.T
        xb_ref[...] = x_ref[...].astype(BF16)
        acc_ref[...] = jnp.zeros((tm, D_MODEL), F32)

    rcol = jnp.max(jnp.where(lane == e, rank_ref[...], -1).astype(F32), axis=1, keepdims=True).astype(I32)
    rrow = rankt_ref[pl.ds(e, 1), :]
    crow = combt_ref[pl.ds(e, 1), :]

    for s0 in range(0, tm, sub):
        rcol_s, rrow_s, crow_s = rcol[s0:s0 + sub], rrow[:, s0:s0 + sub], crow[:, s0:s0 + sub]
        n_routed = jnp.max(rcol_s.astype(F32)).astype(I32) + 1

        def chunk(c, carry):
            base = c * ch
            hit = rrow_s == lax.broadcasted_iota(I32, (ch, sub), 0) + base
            xg = _dot(jnp.where(hit, 1.0, 0.0).astype(BF16), xb_ref[s0:s0 + sub, :]).astype(BF16)
            hgu = _dot(xg, wgu_ref[...])
            hg = hgu[:, :D_FF_EXPERT]
            h = (hg * jax.nn.sigmoid(hg)) * hgu[:, D_FF_EXPERT:]
            f = _dot(h.astype(BF16), wd_ref[...])
            gate = jnp.sum(jnp.where(hit, crow_s, 0.0), axis=1, keepdims=True)
            hit_t = rcol_s == lax.broadcasted_iota(I32, (sub, ch), 1) + base
            acc_ref[s0:s0 + sub, :] += _dot(jnp.where(hit_t, 1.0, 0.0).astype(BF16), (f * gate).astype(BF16))
            return carry

        lax.fori_loop(0, (n_routed + ch - 1) // ch, chunk, 0)

    @pl.when(e == N_EXPERTS - 1)
    def _():
        out_ref[...] = _layer_norm(alpha * x_ref[...] + acc_ref[...], g_ref[...], b_ref[...])


def _moe_ln(x, rw, rb, wgu, wd, g, b, alpha, tm):
    n = x.shape[0]
    row = pl.BlockSpec((tm, D_MODEL), lambda i, e: (i, 0))
    vec = pl.BlockSpec((1, D_MODEL), lambda i, e: (0, 0))
    return pl.pallas_call(
        functools.partial(_moe_kernel, alpha=alpha, tm=tm),
        grid=(n // tm, N_EXPERTS),
        in_specs=[row, pl.BlockSpec((D_MODEL, LANES), lambda i, e: (0, 0)),
                  pl.BlockSpec((1, LANES), lambda i, e: (0, 0)),
                  pl.BlockSpec((None, D_MODEL, 2 * D_FF_EXPERT), lambda i, e: (e, 0, 0)),
                  pl.BlockSpec((None, D_FF_EXPERT, D_MODEL), lambda i, e: (e, 0, 0)),
                  vec, vec],
        out_specs=row,
        out_shape=jax.ShapeDtypeStruct((n, D_MODEL), F32),
        scratch_shapes=[pltpu.VMEM((tm, LANES), I32), pltpu.VMEM((LANES, tm), I32), pltpu.VMEM((LANES, tm), F32),
                        pltpu.VMEM((tm, D_MODEL), BF16), pltpu.VMEM((tm, D_MODEL), F32)],
        compiler_params=pltpu.CompilerParams(dimension_semantics=("arbitrary", "arbitrary"),
                                             vmem_limit_bytes=VMEM_LIMIT),
        name="moe_ln",
    )(x, rw, rb, wgu, wd, g, b)


def _split_w_in(w):
    sizes = (D_MODEL, D_MODEL, D_MODEL, N_HEADS * HEAD_DIM, N_KV_HEADS * HEAD_DIM, N_KV_HEADS * HEAD_DIM,
             IDX_HEADS * IDX_DIM, IDX_DIM, IDX_HEADS, D_MODEL, D_MODEL)
    parts, start = [], 0
    for n in sizes:
        parts.append(w[:, start:start + n])
        start += n
    bg, cg, xin, q, k, v, qi, ki, wi, ga, gb = parts

    def dup(a):
        a = a.reshape(a.shape[0], -1, HEAD_DIM)
        return jnp.concatenate([a, a], axis=-1).reshape(a.shape[0], -1)

    kiwi = jnp.concatenate([ki, wi, jnp.zeros((w.shape[0], LANES - IDX_DIM - IDX_HEADS), w.dtype)], axis=1)
    return {
        'a': jnp.concatenate([bg, cg, xin, ga], axis=1).astype(BF16),
        'q': q.astype(BF16), 'gb': gb.astype(BF16), 'kk': dup(k).astype(BF16), 'vv': v.astype(BF16),
        'qi': qi.astype(BF16), 'ki': jnp.concatenate([ki, ki, kiwi], axis=1).astype(BF16),
    }


def _dup_cache(a):
    a = a.astype(BF16)
    return jnp.concatenate([a, a], axis=-1).reshape(a.shape[0], a.shape[1], -1)


def _pad_rows(a, n):
    return jnp.pad(a, ((0, 0), (0, n - a.shape[1]), (0, 0)))


def _row_tile(n, cap):
    tm = min(n, cap)
    assert n % tm == 0
    return tm


def kernel(x_prompt, x_sample, cache_k, cache_v, cache_kidx, state_conv, w_in, conv_w, w_out, rel_bias,
           ln1_g, ln1_b, ln2_g, ln2_b, ffn_w_gate, ffn_w_up, ffn_w_down, router_w, router_b,
           moe_w_gate, moe_w_up, moe_w_down):
    depth = w_in.shape[0]
    alpha = (2 * depth) ** 0.25
    bp, tp, _ = x_prompt.shape
    bs, ts, _ = x_sample.shape
    past = cache_k.shape[2]
    nq_p = 256 if tp % 256 == 0 else LANES
    nq_s = LANES
    assert tp % KB == 0 and past % LANES == 0 and ts <= nq_s and ts % SUBLANES == 0
    topk_p = min(TOPK_MAX, tp // 4)
    topk_s = min(TOPK_MAX, (past + ts) // 4)
    lpad_s = -(-(past + nq_s) // KB) * KB

    bt_p = _bias_tiles(rel_bias, nq_p)
    bt_s = _bias_tiles(rel_bias, nq_s)
    vec = lambda a: a.reshape(1, -1)

    def layer(l, x, prev8, past_kv):
        b, t, _ = x.shape
        w = _split_w_in(w_in[l])
        prompt = past_kv is None
        (conv, q, sgb, kkf, kkb, vvf, vvb, qi, kiki, kiwi, cstate) = _inproj(
            x, prev8, w, conv_w[l], _row_tile(t, KB), transpose_v=prompt)
        proj = (w_out[l].astype(BF16), vec(ln1_g[l]), vec(ln1_b[l]))
        if prompt:
            x1 = _attention(q, qi, kiwi, conv, sgb, x, kkb, vvb, kiki, bt_p, *proj,
                            alpha=alpha, nq=nq_p, q_off=0, kv_len=t, topk=topk_p)
        else:
            ck, cv, cki = past_kv
            kk_all = _pad_rows(jnp.concatenate([_dup_cache(ck), kkb], axis=1), lpad_s)
            cv = cv.astype(BF16).reshape(b, cv.shape[1], N_KV_HEADS * HEAD_DIM)
            vv_all = _pad_rows(jnp.concatenate([cv, vvb], axis=1), lpad_s)
            vvt_all = jnp.swapaxes(vv_all.reshape(b, lpad_s // KB, KB, N_KV_HEADS * HEAD_DIM), 2, 3)
            cki = cki.astype(BF16)
            kiki_all = _pad_rows(jnp.concatenate([jnp.concatenate([cki, cki], axis=-1), kiki], axis=1), lpad_s)
            pad = lambda a: _pad_rows(a, nq_s)
            x1 = _attention(pad(q), pad(qi), pad(kiwi), pad(conv), pad(sgb), pad(x), kk_all, vvt_all, kiki_all, bt_s,
                            *proj, alpha=alpha, nq=nq_s, q_off=past, kv_len=past + t, topk=topk_s)[:, :t]
        n = b * t
        x1 = x1.reshape(n, D_MODEL)
        j = l // 2
        if l % 2 == 0:
            x2 = _ffn_ln(x1, ffn_w_gate[j].astype(BF16), ffn_w_up[j].astype(BF16), ffn_w_down[j].astype(BF16),
                         vec(ln2_g[l]), vec(ln2_b[l]), alpha, _row_tile(n, 512))
        else:
            rw = jnp.pad(router_w[j], ((0, 0), (0, LANES - N_EXPERTS)))
            rb = jnp.pad(router_b[j], (0, LANES - N_EXPERTS)).reshape(1, LANES)
            wgu = jnp.concatenate([moe_w_gate[j], moe_w_up[j]], axis=-1).astype(BF16)
            x2 = _moe_ln(x1, rw, rb, wgu, moe_w_down[j].astype(BF16), vec(ln2_g[l]), vec(ln2_b[l]), alpha,
                         _row_tile(n, 2 * MOE_SUB))
        heads = lambda a: a.reshape(b, t, N_KV_HEADS, HEAD_DIM)
        return (x2.reshape(b, t, D_MODEL), heads(kkf), heads(vvf), kiwi[..., :IDX_DIM],
                cstate[:, SUBLANES - (CONV_W - 1):, :])

    xp, xs = x_prompt, x_sample
    outs_p, outs_s = [], []
    zero_prev = jnp.zeros((bp, SUBLANES, D_MODEL), F32)
    for l in range(depth):
        xp, *rest = layer(l, xp, zero_prev, None)
        outs_p.append(rest)
        prev8 = jnp.pad(state_conv[l], ((0, 0), (SUBLANES - (CONV_W - 1), 0), (0, 0)))
        xs, *rest = layer(l, xs, prev8, (cache_k[l], cache_v[l], cache_kidx[l]))
        outs_s.append(rest)
    stack = lambda outs, i: jnp.stack([o[i] for o in outs])
    return (xp, xs,
            stack(outs_p, 0), stack(outs_p, 1), stack(outs_p, 2), stack(outs_p, 3),
            stack(outs_s, 0), stack(outs_s, 1), stack(outs_s, 2), stack(outs_s, 3))
```

```python
import functools

import jax
import jax.numpy as jnp
from jax import lax
from jax.experimental import pallas as pl
from jax.experimental.pallas import tpu as pltpu

D_MODEL = 1024
CHUNK = 64
CHUNK_SHIFT = 6
CONV_W = 3
N_HEADS = 16
HEAD_DIM = 64
N_KV_HEADS = 4
GROUP = N_HEADS // N_KV_HEADS
IDX_HEADS = 8
IDX_DIM = 64
TOPK_MAX = 256
NUM_BUCKETS = 32
MAX_DISTANCE = 128
D_FF = 2816
N_EXPERTS = 8
D_FF_EXPERT = 1408
LN_EPS = 1e-5

LANES = 128
SUBLANES = 8
N_PAIRS = N_HEADS // 2
KB = 512
TILES = KB // LANES
N_BIAS_TILES = 4
PAIRS_PER_LOOP = 8
LOOKAHEAD = 3
ONES_ROWS = 16
MOE_SUB = 512
MOE_CHUNK = 160
NEG = -1e30
LOG2E = 1.4426950408889634
INT_MIN = -2 ** 31
VMEM_LIMIT = 56 * 1024 * 1024

F32 = jnp.float32
BF16 = jnp.bfloat16
I32 = jnp.int32
I16 = jnp.int16

assert CHUNK == 1 << CHUNK_SHIFT and 2 * HEAD_DIM == LANES and 2 * IDX_DIM == LANES


def _dot(a, b):
    return jnp.dot(a, b, preferred_element_type=F32)


def _dot_nt(a, b):
    return lax.dot_general(a, b, (((1,), (1,)), ((), ())), preferred_element_type=F32)


def _layer_norm(y, g, b):
    mu = jnp.mean(y, axis=-1, keepdims=True)
    d = y - mu
    var = jnp.mean(d * d, axis=-1, keepdims=True)
    return d * lax.rsqrt(var + LN_EPS) * g + b


def _tree(op, xs):
    xs = list(xs)
    while len(xs) > 1:
        xs = [op(xs[i], xs[i + 1]) for i in range(0, len(xs) - 1, 2)] + ([xs[-1]] if len(xs) % 2 else [])
    return xs[0]


def _fold8(x):
    return _tree(jnp.add, [x[i:i + SUBLANES] for i in range(0, x.shape[0], SUBLANES)])


def _inproj_kernel(x_ref, prev_ref, wa_ref, wq_ref, wgb_ref, wkk_ref, wvv_ref, wqi_ref, wki_ref, cw_ref,
                   conv_ref, q_ref, sgb_ref, kkf_ref, kkb_ref, vvf_ref, vvb_ref, qi_ref, kiki_ref, kiwi_ref,
                   cstate_ref, uext_ref, *, tm, transpose_v):
    @pl.when(pl.program_id(1) == 0)
    def _():
        uext_ref[0:SUBLANES, :] = prev_ref[0]

    xb = x_ref[0].astype(BF16)
    cc = 256
    for c in range(0, D_MODEL, cc):
        bg = _dot(xb, wa_ref[:, c:c + cc])
        cg = _dot(xb, wa_ref[:, D_MODEL + c:D_MODEL + c + cc])
        xin = _dot(xb, wa_ref[:, 2 * D_MODEL + c:2 * D_MODEL + c + cc])
        ga = _dot(xb, wa_ref[:, 3 * D_MODEL + c:3 * D_MODEL + c + cc])
        u = cg * xin
        uext_ref[SUBLANES:SUBLANES + tm, c:c + cc] = u
        um1 = uext_ref[SUBLANES - 1:SUBLANES - 1 + tm, c:c + cc]
        um2 = uext_ref[SUBLANES - 2:SUBLANES - 2 + tm, c:c + cc]
        y = um2 * cw_ref[0:1, c:c + cc] + um1 * cw_ref[1:2, c:c + cc] + u * cw_ref[2:3, c:c + cc]
        conv_ref[0, :, c:c + cc] = jax.nn.sigmoid(ga) * (bg * y)
    tail = uext_ref[tm:tm + SUBLANES, :]
    cstate_ref[0] = tail
    uext_ref[0:SUBLANES, :] = tail

    q_ref[0] = (_dot(xb, wq_ref[...]) * (HEAD_DIM ** -0.5 * LOG2E)).astype(BF16)
    sgb_ref[0] = jax.nn.sigmoid(_dot(xb, wgb_ref[...]))
    def undup(a):
        low = lax.broadcasted_iota(I32, (tm, LANES), 1) < HEAD_DIM
        return jnp.concatenate(
            [jnp.where(low, a[:, 2 * j * LANES:(2 * j + 1) * LANES], a[:, (2 * j + 1) * LANES:(2 * j + 2) * LANES])
             for j in range(N_KV_HEADS // 2)], axis=1)

    kk = _dot(xb, wkk_ref[...])
    kkf_ref[0] = undup(kk)
    kkb_ref[0] = kk.astype(BF16)
    vv = _dot(xb, wvv_ref[...])
    vvf_ref[0] = vv
    if transpose_v:
        vvb_ref[0, 0] = vv.T.astype(BF16)
    else:
        vvb_ref[0] = vv.astype(BF16)
    qi_ref[0] = _dot(xb, wqi_ref[...]).astype(BF16)
    kw = _dot(xb, wki_ref[...])
    kiki_ref[0] = kw[:, 0:LANES].astype(BF16)
    kiwi_ref[0] = kw[:, LANES:2 * LANES]


def _inproj(x, prev8, w, conv_w, tm, transpose_v):
    b, t, _ = x.shape
    grid = (b, t // tm)
    row = lambda n: pl.BlockSpec((1, tm, n), lambda i, j: (i, j, 0))
    full = lambda a: pl.BlockSpec(a.shape, lambda i, j: (0,) * a.ndim, pipeline_mode=pl.Buffered(1))
    if transpose_v:
        assert tm == KB
        vvb_shape = jax.ShapeDtypeStruct((b, t // KB, N_KV_HEADS * HEAD_DIM, KB), BF16)
        vvb_spec = pl.BlockSpec((1, 1, N_KV_HEADS * HEAD_DIM, KB), lambda i, j: (i, j, 0, 0))
    else:
        vvb_shape = jax.ShapeDtypeStruct((b, t, N_KV_HEADS * HEAD_DIM), BF16)
        vvb_spec = row(N_KV_HEADS * HEAD_DIM)
    out_shapes = (
        jax.ShapeDtypeStruct((b, t, D_MODEL), F32),
        jax.ShapeDtypeStruct((b, t, D_MODEL), BF16),
        jax.ShapeDtypeStruct((b, t, D_MODEL), F32),
        jax.ShapeDtypeStruct((b, t, N_KV_HEADS * HEAD_DIM), F32),
        jax.ShapeDtypeStruct((b, t, 4 * LANES), BF16),
        jax.ShapeDtypeStruct((b, t, N_KV_HEADS * HEAD_DIM), F32),
        vvb_shape,
        jax.ShapeDtypeStruct((b, t, IDX_HEADS * IDX_DIM), BF16),
        jax.ShapeDtypeStruct((b, t, LANES), BF16),
        jax.ShapeDtypeStruct((b, t, LANES), F32),
        jax.ShapeDtypeStruct((b, SUBLANES, D_MODEL), F32),
    )
    out_specs = (row(D_MODEL), row(D_MODEL), row(D_MODEL), row(N_KV_HEADS * HEAD_DIM), row(4 * LANES),
                 row(N_KV_HEADS * HEAD_DIM), vvb_spec, row(IDX_HEADS * IDX_DIM), row(LANES), row(LANES),
                 pl.BlockSpec((1, SUBLANES, D_MODEL), lambda i, j: (i, 0, 0)))
    in_specs = [row(D_MODEL), pl.BlockSpec((1, SUBLANES, D_MODEL), lambda i, j: (i, 0, 0)),
                full(w['a']), full(w['q']), full(w['gb']), full(w['kk']), full(w['vv']), full(w['qi']),
                full(w['ki']), full(conv_w)]
    return pl.pallas_call(
        functools.partial(_inproj_kernel, tm=tm, transpose_v=transpose_v),
        grid=grid, in_specs=in_specs, out_specs=out_specs, out_shape=out_shapes,
        scratch_shapes=[pltpu.VMEM((tm + SUBLANES, D_MODEL), F32)],
        compiler_params=pltpu.CompilerParams(dimension_semantics=("arbitrary", "arbitrary"),
                                             vmem_limit_bytes=VMEM_LIMIT),
        name="inproj_conv",
    )(x, prev8, w['a'], w['q'], w['gb'], w['kk'], w['vv'], w['qi'], w['ki'], conv_w)


def _bias_tiles_kernel(tbl_ref, out_ref, *, nq):
    d = pl.program_id(0) - 2
    krow = lax.broadcasted_iota(I32, (LANES, nq), 0)
    qcol = lax.broadcasted_iota(I32, (LANES, nq), 1)
    rel = d * LANES + krow - qcol
    n = jnp.abs(rel)
    nb = NUM_BUCKETS // 2
    max_exact = nb // 2
    n2 = n * n
    large = jnp.full((LANES, nq), max_exact, I32)
    for j in range(1, nb - max_exact):
        large = large + (n2 >= (max_exact * max_exact) * (1 << j)).astype(I32)
    bucket = jnp.where(n < max_exact, n, large) + jnp.where(rel > 0, nb, 0)
    far = pl.program_id(0) == 0
    for h in range(N_HEADS):
        base = tbl_ref[nb - 1, h]
        acc = jnp.zeros((LANES, nq), F32)
        for bk in range(NUM_BUCKETS):
            acc = jnp.where(bucket == bk, (tbl_ref[bk, h] - base) * LOG2E, acc)
        out_ref[0, h] = jnp.where(far, 0.0, acc)


def _bias_tiles(rel_bias, nq):
    assert MAX_DISTANCE == 128 and NUM_BUCKETS == 32
    return pl.pallas_call(
        functools.partial(_bias_tiles_kernel, nq=nq),
        grid=(N_BIAS_TILES,),
        in_specs=[pl.BlockSpec(memory_space=pltpu.SMEM)],
        out_specs=pl.BlockSpec((1, N_HEADS, LANES, nq), lambda i: (i, 0, 0, 0)),
        out_shape=jax.ShapeDtypeStruct((N_BIAS_TILES, N_HEADS, LANES, nq), F32),
        name="bias_tiles",
    )(rel_bias)


def _attn_kernel(q_ref, qi_ref, kiwi_ref, conv_ref, sgb_ref, x_ref, kk_ref, vvt_ref, kiki_ref, bt_ref,
                 wout_ref, g_ref, b_ref, out_ref,
                 key_ref, mask_ref, qis_ref, tri_ref, qs_ref, acc_ref, merged_ref, hi_ref, dig_ref,
                 *, alpha, nq, q_off, kv_len, topk):
    q0 = q_off + pl.program_id(1) * nq
    nkb = (q0 + nq + KB - 1) // KB
    jd = q0 // LANES
    lane = lax.broadcasted_iota(I32, (nq, LANES), 1)
    low = lane < HEAD_DIM
    krow = lax.broadcasted_iota(I32, (LANES, nq), 0)
    qchunk = lax.shift_right_arithmetic(q0 + lax.broadcasted_iota(I32, (LANES, nq), 1), CHUNK_SHIFT)

    wit = kiwi_ref[0].T * ((IDX_DIM ** -0.5) * (IDX_HEADS ** -0.5))
    wis = [wit[IDX_DIM + h:IDX_DIM + h + 1, :] for h in range(IDX_HEADS)]
    for h in range(IDX_HEADS):
        pair = qi_ref[0, :, (h // 2) * LANES:(h // 2 + 1) * LANES].astype(F32)
        keep = low if h % 2 == 0 else jnp.logical_not(low)
        qis_ref[h] = jnp.where(keep, pair, 0.0).astype(BF16)

    def score_block(jb, carry):
        ks = pl.multiple_of(jb * KB, KB)
        ki = kiki_ref[0, pl.ds(ks, KB), :]
        sc = [jnp.zeros((LANES, nq), F32) for _ in range(TILES)]
        for h in range(IDX_HEADS):
            s = _dot_nt(ki, qis_ref[h])
            for c in range(TILES):
                sc[c] = sc[c] + jnp.maximum(s[c * LANES:(c + 1) * LANES], 0.0) * wis[h]
        for c in range(TILES):
            bits = lax.bitcast_convert_type(sc[c], I32)
            key = jnp.where(bits < 0, bits ^ 0x7FFFFFFF, bits)
            kpos = ks + c * LANES + krow
            adm = (lax.shift_right_arithmetic(kpos, CHUNK_SHIFT) <= qchunk) & (kpos < kv_len)
            key_ref[pl.ds(ks + c * LANES, LANES), :] = jnp.where(adm, key, INT_MIN)
        return carry

    lax.fori_loop(0, nkb, score_block, 0)

    def count(pred):
        def body(jb, acc):
            ks = pl.multiple_of(jb * KB, KB)
            parts = []
            for c in range(TILES):
                blk = key_ref[pl.ds(ks + c * LANES, LANES), :]
                parts.append(_fold8(pred(blk, ks + c * LANES).astype(I32)))
            return acc + _tree(jnp.add, parts)
        acc = lax.fori_loop(0, nkb, body, jnp.zeros((SUBLANES, nq), I32))
        return jnp.sum(acc.astype(F32), axis=0, keepdims=True).astype(I32)

    pack = 2 * SUBLANES

    def count16(ref, cand):
        def body(jb, acc):
            ks = pl.multiple_of(jb * KB, KB)
            parts = []
            for c in range(TILES):
                hit = jnp.where(ref[pl.ds(ks + c * LANES, LANES), :] >= cand, jnp.ones((), I16), jnp.zeros((), I16))
                parts.append(_tree(jnp.add, [hit[i:i + pack] for i in range(0, LANES, pack)]))
            return acc + _tree(jnp.add, parts)
        acc = lax.fori_loop(0, nkb, body, jnp.zeros((pack, nq), I16))
        return jnp.sum(acc.astype(I32).astype(F32), axis=0, keepdims=True).astype(I32)

    def digit_search(ref, nbits, start, need):
        def step(i, t):
            cand = t + lax.shift_left(jnp.int32(1), nbits - 1 - i)
            return jnp.where(count16(ref, cand.astype(I16)) >= need, cand, t)
        return lax.fori_loop(0, nbits, step, start)

    def fill(ref, fn):
        def body(jb, carry):
            ks = pl.multiple_of(jb * KB, KB)
            for c in range(TILES):
                rows = pl.ds(ks + c * LANES, LANES)
                ref[rows, :] = fn(key_ref[rows, :], rows)
            return carry
        lax.fori_loop(0, nkb, body, 0)

    digit_min = -2 ** 15
    zero = jnp.zeros((1, nq), I32)
    fill(hi_ref, lambda k, rows: lax.shift_right_arithmetic(k, 16).astype(I16))
    top = digit_search(hi_ref, 16, jnp.full((1, nq), digit_min, I32), topk)
    found = top > digit_min
    above = jnp.where(top < 2 ** 15 - 1, count16(hi_ref, jnp.minimum(top + 1, 2 ** 15 - 1).astype(I16)), 0)
    need = topk - above
    top16 = top.astype(I16)
    fill(dig_ref, lambda k, rows: jnp.where(
        hi_ref[rows, :] == top16, (lax.shift_right_logical(k, 8) & 0xFF).astype(I16), jnp.full((), -1, I16)))
    mid = digit_search(dig_ref, 8, zero, need)
    need = need - count16(dig_ref, (mid + 1).astype(I16))
    mid16 = mid.astype(I16)
    fill(dig_ref, lambda k, rows: jnp.where(dig_ref[rows, :] == mid16, (k & 0xFF).astype(I16), jnp.full((), -1, I16)))
    bottom = digit_search(dig_ref, 8, zero, need)
    thr = jnp.where(found, lax.shift_left(top, 16) | lax.shift_left(mid, 8) | bottom, INT_MIN)

    want = (topk - count(lambda blk, base: blk > thr)).astype(F32)
    live = thr > INT_MIN
    tri_ref[...] = (lax.broadcasted_iota(I32, (KB, KB), 0) > lax.broadcasted_iota(I32, (KB, KB), 1)
                    ).astype(F32).astype(BF16)

    def mask_block(jb, seen):
        ks = pl.multiple_of(jb * KB, KB)
        keys = [key_ref[pl.ds(ks + c * LANES, LANES), :] for c in range(TILES)]
        ties = [((k == thr) & live).astype(F32) for k in keys]
        before = _dot(tri_ref[...], jnp.concatenate(ties, axis=0).astype(BF16)) + seen
        for c in range(TILES):
            keep = (keys[c] > thr) | ((ties[c] > 0.0) & (before[c * LANES:(c + 1) * LANES] < want))
            mask_ref[pl.ds(ks + c * LANES, LANES), :] = jnp.where(keep, 0.0, NEG)
        return seen + jnp.sum(_fold8(_tree(jnp.add, ties)), axis=0, keepdims=True)

    lax.fori_loop(0, nkb, mask_block, jnp.zeros((1, nq), F32))

    def head_step(s, msks, biases, m):
        ss = [s[c * LANES:(c + 1) * LANES] + msks[c] for c in range(TILES)]
        if biases is not None:
            ss = [x + b for x, b in zip(ss, biases)]
        m_new = jnp.maximum(m, jnp.max(_tree(jnp.maximum, ss), axis=0, keepdims=True))
        return m_new, jnp.exp2(m - m_new), jnp.concatenate([jnp.exp2(x - m_new) for x in ss], axis=0).astype(BF16)

    for p0 in range(0, N_PAIRS, PAIRS_PER_LOOP):
        pairs = tuple(range(p0, p0 + PAIRS_PER_LOOP))
        for p in pairs:
            qp = q_ref[0, :, p * LANES:(p + 1) * LANES].astype(F32)
            qs_ref[2 * p] = jnp.where(low, qp, 0.0).astype(BF16)
            qs_ref[2 * p + 1] = jnp.where(low, 0.0, qp).astype(BF16)

        def attn_block(jb, carry, with_bias):
            ks = pl.multiple_of(jb * KB, KB)
            msks = [mask_ref[pl.ds(ks + c * LANES, LANES), :] for c in range(TILES)]
            tiles = [jnp.clip(jb * TILES + c - jd + 2, 0, N_BIAS_TILES - 1) for c in range(TILES)]

            def logits(head):
                g = head // GROUP
                kblk = kk_ref[0, pl.ds(ks, KB), g * LANES:(g + 1) * LANES]
                return _dot_nt(kblk, qs_ref[head])

            heads = [2 * p + r for p in pairs for r in range(2)]
            pending = [logits(h) for h in heads[:LOOKAHEAD]]
            vts = {}
            out = []
            for i, p in enumerate(pairs):
                g = (2 * p) // GROUP
                if g not in vts:
                    vts[g] = jnp.concatenate([vvt_ref[0, jb, g * HEAD_DIM:(g + 1) * HEAD_DIM, :],
                                              jnp.ones((ONES_ROWS, KB), BF16)], axis=0)
                stats = []
                for r, (m, l) in enumerate(carry[i]):
                    nxt = 2 * i + r + LOOKAHEAD
                    if nxt < len(heads):
                        pending.append(logits(heads[nxt]))
                    biases = [bt_ref[tiles[c], 2 * p + r] for c in range(TILES)] if with_bias else None
                    m, a, pr = head_step(pending.pop(0), msks, biases, m)
                    pv = _dot(vts[g], pr)
                    acc_ref[2 * p + r] = a * acc_ref[2 * p + r] + pv[:HEAD_DIM]
                    stats.append((m, a * l + pv[HEAD_DIM:HEAD_DIM + 1]))
                out.append(tuple(stats))
            return tuple(out)

        for p in pairs:
            acc_ref[2 * p] = jnp.zeros((HEAD_DIM, nq), F32)
            acc_ref[2 * p + 1] = jnp.zeros((HEAD_DIM, nq), F32)
        init = ((jnp.full((1, nq), NEG, F32), jnp.zeros((1, nq), F32)),) * 2
        nfar = jnp.clip((jd - 1) // TILES, 0, nkb)
        res = lax.fori_loop(0, nfar, functools.partial(attn_block, with_bias=False), (init,) * len(pairs))
        res = lax.fori_loop(nfar, nkb, functools.partial(attn_block, with_bias=True), res)
        for i, p in enumerate(pairs):
            (_, la), (_, lb) = res[i]
            o = jnp.concatenate([acc_ref[2 * p] / la, acc_ref[2 * p + 1] / lb], axis=0).T
            sl = slice(p * LANES, (p + 1) * LANES)
            merged_ref[:, sl] = (conv_ref[0, :, sl] + sgb_ref[0, :, sl] * o).astype(BF16)

    y = alpha * x_ref[0] + _dot(merged_ref[...], wout_ref[...])
    out_ref[0] = _layer_norm(y, g_ref[...], b_ref[...])


def _attention(q, qi, kiwi, conv, sgb, x, kk, vvt, kiki, bias_tiles, w_out, g, b_ln, *, alpha, nq, q_off, kv_len,
               topk):
    b, t, _ = q.shape
    lpad = kk.shape[1]
    assert q_off % LANES == 0 and nq % LANES == 0 and t % nq == 0
    assert lpad % KB == 0 and lpad >= q_off + t and vvt.shape == (b, lpad // KB, N_KV_HEADS * HEAD_DIM, KB)
    grid = (b, t // nq)
    row = lambda n: pl.BlockSpec((1, nq, n), lambda i, j: (i, j, 0))
    keys = lambda n: pl.BlockSpec((1, lpad, n), lambda i, j: (i, 0, 0), pipeline_mode=pl.Buffered(1))
    const = lambda a: pl.BlockSpec(a.shape, lambda i, j: (0,) * a.ndim, pipeline_mode=pl.Buffered(1))
    return pl.pallas_call(
        functools.partial(_attn_kernel, alpha=alpha, nq=nq, q_off=q_off, kv_len=kv_len, topk=topk),
        grid=grid,
        in_specs=[row(D_MODEL), row(IDX_HEADS * IDX_DIM), row(LANES), row(D_MODEL), row(D_MODEL), row(D_MODEL),
                  keys(4 * LANES),
                  pl.BlockSpec((1, lpad // KB, N_KV_HEADS * HEAD_DIM, KB), lambda i, j: (i, 0, 0, 0),
                               pipeline_mode=pl.Buffered(1)),
                  keys(LANES), const(bias_tiles), const(w_out), const(g), const(b_ln)],
        out_specs=row(D_MODEL),
        out_shape=jax.ShapeDtypeStruct((b, t, D_MODEL), F32),
        scratch_shapes=[pltpu.VMEM((lpad, nq), I32), pltpu.VMEM((lpad, nq), F32),
                        pltpu.VMEM((IDX_HEADS, nq, LANES), BF16), pltpu.VMEM((KB, KB), BF16),
                        pltpu.VMEM((N_HEADS, nq, LANES), BF16), pltpu.VMEM((N_HEADS, HEAD_DIM, nq), F32),
                        pltpu.VMEM((nq, D_MODEL), BF16), pltpu.VMEM((lpad, nq), I16), pltpu.VMEM((lpad, nq), I16)],
        compiler_params=pltpu.CompilerParams(dimension_semantics=("arbitrary", "arbitrary"),
                                             vmem_limit_bytes=VMEM_LIMIT),
        name="dsa_attention",
    )(q, qi, kiwi, conv, sgb, x, kk, vvt, kiki, bias_tiles, w_out, g, b_ln)


def _ffn_kernel(x_ref, wg_ref, wu_ref, wd_ref, g_ref, b_ref, out_ref, *, alpha):
    x = x_ref[...]
    xb = x.astype(BF16)
    hg = _dot(xb, wg_ref[...])
    h = (hg * jax.nn.sigmoid(hg)) * _dot(xb, wu_ref[...])
    f = _dot(h.astype(BF16), wd_ref[...])
    out_ref[...] = _layer_norm(alpha * x + f, g_ref[...], b_ref[...])


def _ffn_ln(x, wg, wu, wd, g, b, alpha, tm):
    n = x.shape[0]
    row = pl.BlockSpec((tm, D_MODEL), lambda i: (i, 0))
    vec = pl.BlockSpec((1, D_MODEL), lambda i: (0, 0))
    full = lambda a: pl.BlockSpec(a.shape, lambda i: (0, 0), pipeline_mode=pl.Buffered(1))
    return pl.pallas_call(
        functools.partial(_ffn_kernel, alpha=alpha),
        grid=(n // tm,),
        in_specs=[row, full(wg), full(wu), full(wd), vec, vec],
        out_specs=row,
        out_shape=jax.ShapeDtypeStruct((n, D_MODEL), F32),
        compiler_params=pltpu.CompilerParams(dimension_semantics=("arbitrary",), vmem_limit_bytes=VMEM_LIMIT),
        name="ffn_ln",
    )(x, wg, wu, wd, g, b)


def _moe_kernel(x_ref, rw_ref, rb_ref, wgu_ref, wd_ref, g_ref, b_ref, out_ref,
                rank_ref, rankt_ref, combt_ref, xb_ref, acc_ref, *, alpha, tm):
    e = pl.program_id(1)
    sub = min(MOE_SUB, tm)
    assert sub & (sub - 1) == 0 and tm % sub == 0
    ch = min(MOE_CHUNK, sub)
    lane = lax.broadcasted_iota(I32, (tm, LANES), 1)

    @pl.when(e == 0)
    def _():
        x = x_ref[...]
        x_hi = x.astype(BF16)
        x_lo = (x - x_hi.astype(F32)).astype(BF16)
        logits = (_dot(x_hi, rw_ref[0]) + _dot(x_hi, rw_ref[1]) + _dot(x_lo, rw_ref[0])) + rb_ref[...]
        logits = jnp.where(lane < N_EXPERTS, logits, -jnp.inf)
        lanef = lane.astype(F32)
        v1 = jnp.max(logits, axis=1, keepdims=True)
        i1 = jnp.min(jnp.where(logits == v1, lanef, float(LANES)), axis=1, keepdims=True)
        rest = jnp.where(lanef == i1, -jnp.inf, logits)
        v2 = jnp.max(rest, axis=1, keepdims=True)
        i2 = jnp.min(jnp.where(rest == v2, lanef, float(LANES)), axis=1, keepdims=True)
        e2 = jnp.exp(v2 - v1)
        den = 1.0 + e2
        comb = jnp.where(lanef == i1, 1.0 / den, 0.0) + jnp.where(lanef == i2, e2 / den, 0.0)
        routed = (lanef == i1) | (lanef == i2)
        tr = lax.broadcasted_iota(I32, (tm, tm), 0)
        tc = lax.broadcasted_iota(I32, (tm, tm), 1)
        earlier = (tr > tc) & (tc >= tr - (tr & (sub - 1)))
        rank = _dot(earlier.astype(F32).astype(BF16), routed.astype(F32).astype(BF16))
        rank = jnp.where(routed, rank, -1.0)
        rank_ref[...] = rank.astype(I32)
        rankt_ref[...] = rank.T.astype(I32)
        combt_ref[...] = comb.T
        xb_ref[...] = x_ref[...].astype(BF16)
        acc_ref[...] = jnp.zeros((tm, D_MODEL), F32)

    rcol = jnp.max(jnp.where(lane == e, rank_ref[...], -1).astype(F32), axis=1, keepdims=True).astype(I32)
    rrow = rankt_ref[pl.ds(e, 1), :]
    crow = combt_ref[pl.ds(e, 1), :]

    for s0 in range(0, tm, sub):
        rcol_s, rrow_s, crow_s = rcol[s0:s0 + sub], rrow[:, s0:s0 + sub], crow[:, s0:s0 + sub]
        n_routed = jnp.max(rcol_s.astype(F32)).astype(I32) + 1

        def chunk(c, carry):
            base = c * ch
            hit = rrow_s == lax.broadcasted_iota(I32, (ch, sub), 0) + base
            xg = _dot(jnp.where(hit, 1.0, 0.0).astype(BF16), xb_ref[s0:s0 + sub, :]).astype(BF16)
            hgu = _dot(xg, wgu_ref[...])
            hg = hgu[:, :D_FF_EXPERT]
            h = (hg * jax.nn.sigmoid(hg)) * hgu[:, D_FF_EXPERT:]
            f = _dot(h.astype(BF16), wd_ref[...])
            gate = jnp.sum(jnp.where(hit, crow_s, 0.0), axis=1, keepdims=True)
            hit_t = rcol_s == lax.broadcasted_iota(I32, (sub, ch), 1) + base
            acc_ref[s0:s0 + sub, :] += _dot(jnp.where(hit_t, 1.0, 0.0).astype(BF16), (f * gate).astype(BF16))
            return carry

        lax.fori_loop(0, (n_routed + ch - 1) // ch, chunk, 0)

    @pl.when(e == N_EXPERTS - 1)
    def _():
        out_ref[...] = _layer_norm(alpha * x_ref[...] + acc_ref[...], g_ref[...], b_ref[...])


def _moe_ln(x, rw, rb, wgu, wd, g, b, alpha, tm):
    n = x.shape[0]
    row = pl.BlockSpec((tm, D_MODEL), lambda i, e: (i, 0))
    vec = pl.BlockSpec((1, D_MODEL), lambda i, e: (0, 0))
    return pl.pallas_call(
        functools.partial(_moe_kernel, alpha=alpha, tm=tm),
        grid=(n // tm, N_EXPERTS),
        in_specs=[row, pl.BlockSpec((2, D_MODEL, LANES), lambda i, e: (0, 0, 0)),
                  pl.BlockSpec((1, LANES), lambda i, e: (0, 0)),
                  pl.BlockSpec((None, D_MODEL, 2 * D_FF_EXPERT), lambda i, e: (e, 0, 0)),
                  pl.BlockSpec((None, D_FF_EXPERT, D_MODEL), lambda i, e: (e, 0, 0)),
                  vec, vec],
        out_specs=row,
        out_shape=jax.ShapeDtypeStruct((n, D_MODEL), F32),
        scratch_shapes=[pltpu.VMEM((tm, LANES), I32), pltpu.VMEM((LANES, tm), I32), pltpu.VMEM((LANES, tm), F32),
                        pltpu.VMEM((tm, D_MODEL), BF16), pltpu.VMEM((tm, D_MODEL), F32)],
        compiler_params=pltpu.CompilerParams(dimension_semantics=("arbitrary", "arbitrary"),
                                             vmem_limit_bytes=VMEM_LIMIT),
        name="moe_ln",
    )(x, rw, rb, wgu, wd, g, b)


def _split_w_in(w):
    sizes = (D_MODEL, D_MODEL, D_MODEL, N_HEADS * HEAD_DIM, N_KV_HEADS * HEAD_DIM, N_KV_HEADS * HEAD_DIM,
             IDX_HEADS * IDX_DIM, IDX_DIM, IDX_HEADS, D_MODEL, D_MODEL)
    parts, start = [], 0
    for n in sizes:
        parts.append(w[:, start:start + n])
        start += n
    bg, cg, xin, q, k, v, qi, ki, wi, ga, gb = parts

    def dup(a):
        a = a.reshape(a.shape[0], -1, HEAD_DIM)
        return jnp.concatenate([a, a], axis=-1).reshape(a.shape[0], -1)

    kiwi = jnp.concatenate([ki, wi, jnp.zeros((w.shape[0], LANES - IDX_DIM - IDX_HEADS), w.dtype)], axis=1)
    return {
        'a': jnp.concatenate([bg, cg, xin, ga], axis=1).astype(BF16),
        'q': q.astype(BF16), 'gb': gb.astype(BF16), 'kk': dup(k).astype(BF16), 'vv': v.astype(BF16),
        'qi': qi.astype(BF16), 'ki': jnp.concatenate([ki, ki, kiwi], axis=1).astype(BF16),
    }


def _dup_cache(a):
    a = a.astype(BF16)
    return jnp.concatenate([a, a], axis=-1).reshape(a.shape[0], a.shape[1], -1)


def _pad_rows(a, n):
    return jnp.pad(a, ((0, 0), (0, n - a.shape[1]), (0, 0)))


def _row_tile(n, cap):
    tm = min(n, cap)
    assert n % tm == 0
    return tm


def kernel(x_prompt, x_sample, cache_k, cache_v, cache_kidx, state_conv, w_in, conv_w, w_out, rel_bias,
           ln1_g, ln1_b, ln2_g, ln2_b, ffn_w_gate, ffn_w_up, ffn_w_down, router_w, router_b,
           moe_w_gate, moe_w_up, moe_w_down):
    depth = w_in.shape[0]
    alpha = (2 * depth) ** 0.25
    bp, tp, _ = x_prompt.shape
    bs, ts, _ = x_sample.shape
    past = cache_k.shape[2]
    nq_p = 256 if tp % 256 == 0 else LANES
    nq_s = LANES
    assert tp % KB == 0 and past % LANES == 0 and ts <= nq_s and ts % SUBLANES == 0
    topk_p = min(TOPK_MAX, tp // 4)
    topk_s = min(TOPK_MAX, (past + ts) // 4)
    lpad_s = -(-(past + nq_s) // KB) * KB

    bt_p = _bias_tiles(rel_bias, nq_p)
    bt_s = _bias_tiles(rel_bias, nq_s)
    vec = lambda a: a.reshape(1, -1)

    def layer(l, x, prev8, past_kv):
        b, t, _ = x.shape
        w = _split_w_in(w_in[l])
        prompt = past_kv is None
        (conv, q, sgb, kkf, kkb, vvf, vvb, qi, kiki, kiwi, cstate) = _inproj(
            x, prev8, w, conv_w[l], _row_tile(t, KB), transpose_v=prompt)
        proj = (w_out[l].astype(BF16), vec(ln1_g[l]), vec(ln1_b[l]))
        if prompt:
            x1 = _attention(q, qi, kiwi, conv, sgb, x, kkb, vvb, kiki, bt_p, *proj,
                            alpha=alpha, nq=nq_p, q_off=0, kv_len=t, topk=topk_p)
        else:
            ck, cv, cki = past_kv
            kk_all = _pad_rows(jnp.concatenate([_dup_cache(ck), kkb], axis=1), lpad_s)
            cv = cv.astype(BF16).reshape(b, cv.shape[1], N_KV_HEADS * HEAD_DIM)
            vv_all = _pad_rows(jnp.concatenate([cv, vvb], axis=1), lpad_s)
            vvt_all = jnp.swapaxes(vv_all.reshape(b, lpad_s // KB, KB, N_KV_HEADS * HEAD_DIM), 2, 3)
            cki = cki.astype(BF16)
            kiki_all = _pad_rows(jnp.concatenate([jnp.concatenate([cki, cki], axis=-1), kiki], axis=1), lpad_s)
            pad = lambda a: _pad_rows(a, nq_s)
            x1 = _attention(pad(q), pad(qi), pad(kiwi), pad(conv), pad(sgb), pad(x), kk_all, vvt_all, kiki_all, bt_s,
                            *proj, alpha=alpha, nq=nq_s, q_off=past, kv_len=past + t, topk=topk_s)[:, :t]
        n = b * t
        x1 = x1.reshape(n, D_MODEL)
        j = l // 2
        if l % 2 == 0:
            x2 = _ffn_ln(x1, ffn_w_gate[j].astype(BF16), ffn_w_up[j].astype(BF16), ffn_w_down[j].astype(BF16),
                         vec(ln2_g[l]), vec(ln2_b[l]), alpha, _row_tile(n, 512))
        else:
            rw = jnp.pad(router_w[j], ((0, 0), (0, LANES - N_EXPERTS)))
            rw_hi = rw.astype(BF16)
            rw = jnp.stack([rw_hi, (rw - rw_hi.astype(F32)).astype(BF16)])
            rb = jnp.pad(router_b[j], (0, LANES - N_EXPERTS)).reshape(1, LANES)
            wgu = jnp.concatenate([moe_w_gate[j], moe_w_up[j]], axis=-1).astype(BF16)
            x2 = _moe_ln(x1, rw, rb, wgu, moe_w_down[j].astype(BF16), vec(ln2_g[l]), vec(ln2_b[l]), alpha,
                         _row_tile(n, 2 * MOE_SUB))
        heads = lambda a: a.reshape(b, t, N_KV_HEADS, HEAD_DIM)
        return (x2.reshape(b, t, D_MODEL), heads(kkf), heads(vvf), kiwi[..., :IDX_DIM],
                cstate[:, SUBLANES - (CONV_W - 1):, :])

    xp, xs = x_prompt, x_sample
    outs_p, outs_s = [], []
    zero_prev = jnp.zeros((bp, SUBLANES, D_MODEL), F32)
    for l in range(depth):
        xp, *rest = layer(l, xp, zero_prev, None)
        outs_p.append(rest)
        prev8 = jnp.pad(state_conv[l], ((0, 0), (SUBLANES - (CONV_W - 1), 0), (0, 0)))
        xs, *rest = layer(l, xs, prev8, (cache_k[l], cache_v[l], cache_kidx[l]))
        outs_s.append(rest)
    stack = lambda outs, i: jnp.stack([o[i] for o in outs])
    return (xp, xs,
            stack(outs_p, 0), stack(outs_p, 1), stack(outs_p, 2), stack(outs_p, 3),
            stack(outs_s, 0), stack(outs_s, 1), stack(outs_s, 2), stack(outs_s, 3))
```

```python
import functools

import jax
import jax.numpy as jnp
from jax import lax
from jax.experimental import pallas as pl
from jax.experimental.pallas import tpu as pltpu

D_MODEL = 1024
CHUNK = 64
CHUNK_SHIFT = 6
CONV_W = 3
N_HEADS = 16
HEAD_DIM = 64
N_KV_HEADS = 4
GROUP = N_HEADS // N_KV_HEADS
IDX_HEADS = 8
IDX_DIM = 64
TOPK_MAX = 256
NUM_BUCKETS = 32
MAX_DISTANCE = 128
N_EXPERTS = 8
D_FF_EXPERT = 1408
LN_EPS = 1e-5

LANES = 128
SUBLANES = 8
N_PAIRS = N_HEADS // 2
KB = 512
TILES = KB // LANES
N_BIAS_TILES = 4
PAIRS_PER_LOOP = 8
LOOKAHEAD = 3
ONES_ROWS = 16
MOE_SUB = 512
MOE_CHUNK = 160
NEG = -1e30
LOG2E = 1.4426950408889634
INT_MIN = -2 ** 31
VMEM_LIMIT = 56 * 1024 * 1024

F32 = jnp.float32
BF16 = jnp.bfloat16
I32 = jnp.int32
I16 = jnp.int16

assert CHUNK == 1 << CHUNK_SHIFT and 2 * HEAD_DIM == LANES and 2 * IDX_DIM == LANES


def _dot(a, b):
    return jnp.dot(a, b, preferred_element_type=F32)


def _dot_nt(a, b):
    return lax.dot_general(a, b, (((1,), (1,)), ((), ())), preferred_element_type=F32)


def _layer_norm(y, g, b):
    mu = jnp.mean(y, axis=-1, keepdims=True)
    d = y - mu
    var = jnp.mean(d * d, axis=-1, keepdims=True)
    return d * lax.rsqrt(var + LN_EPS) * g + b


def _tree(op, xs):
    xs = list(xs)
    while len(xs) > 1:
        xs = [op(xs[i], xs[i + 1]) for i in range(0, len(xs) - 1, 2)] + ([xs[-1]] if len(xs) % 2 else [])
    return xs[0]


def _fold8(x):
    return _tree(jnp.add, [x[i:i + SUBLANES] for i in range(0, x.shape[0], SUBLANES)])


def _inproj_kernel(x_ref, prev_ref, wa_ref, wq_ref, wgb_ref, wkk_ref, wvv_ref, wqi_ref, wki_ref, cw_ref,
                   conv_ref, q_ref, sgb_ref, kkf_ref, kkb_ref, vvf_ref, vvb_ref, qi_ref, kiki_ref, kiwi_ref,
                   cstate_ref, uext_ref, *, tm, transpose_v):
    @pl.when(pl.program_id(1) == 0)
    def _():
        uext_ref[0:SUBLANES, :] = prev_ref[0]

    xb = x_ref[0].astype(BF16)
    cc = 256
    for c in range(0, D_MODEL, cc):
        bg = _dot(xb, wa_ref[:, c:c + cc])
        cg = _dot(xb, wa_ref[:, D_MODEL + c:D_MODEL + c + cc])
        xin = _dot(xb, wa_ref[:, 2 * D_MODEL + c:2 * D_MODEL + c + cc])
        ga = _dot(xb, wa_ref[:, 3 * D_MODEL + c:3 * D_MODEL + c + cc])
        u = cg * xin
        uext_ref[SUBLANES:SUBLANES + tm, c:c + cc] = u
        um1 = uext_ref[SUBLANES - 1:SUBLANES - 1 + tm, c:c + cc]
        um2 = uext_ref[SUBLANES - 2:SUBLANES - 2 + tm, c:c + cc]
        y = um2 * cw_ref[0:1, c:c + cc] + um1 * cw_ref[1:2, c:c + cc] + u * cw_ref[2:3, c:c + cc]
        conv_ref[0, :, c:c + cc] = jax.nn.sigmoid(ga) * (bg * y)
    tail = uext_ref[tm:tm + SUBLANES, :]
    cstate_ref[0] = tail
    uext_ref[0:SUBLANES, :] = tail

    q_ref[0] = (_dot(xb, wq_ref[...]) * (HEAD_DIM ** -0.5 * LOG2E)).astype(BF16)
    sgb_ref[0] = jax.nn.sigmoid(_dot(xb, wgb_ref[...]))
    def undup(a):
        low = lax.broadcasted_iota(I32, (tm, LANES), 1) < HEAD_DIM
        return jnp.concatenate(
            [jnp.where(low, a[:, 2 * j * LANES:(2 * j + 1) * LANES], a[:, (2 * j + 1) * LANES:(2 * j + 2) * LANES])
             for j in range(N_KV_HEADS // 2)], axis=1)

    kk = _dot(xb, wkk_ref[...])
    kkf_ref[0] = undup(kk)
    kkb_ref[0] = kk.astype(BF16)
    vv = _dot(xb, wvv_ref[...])
    vvf_ref[0] = vv
    if transpose_v:
        vvb_ref[0, 0] = vv.T.astype(BF16)
    else:
        vvb_ref[0] = vv.astype(BF16)
    qi_ref[0] = _dot(xb, wqi_ref[...]).astype(BF16)
    kw = _dot(xb, wki_ref[...])
    kiki_ref[0] = kw[:, 0:LANES].astype(BF16)
    kiwi_ref[0] = kw[:, LANES:2 * LANES]


def _inproj(x, prev8, w, conv_w, tm, transpose_v):
    b, t, _ = x.shape
    grid = (b, t // tm)
    row = lambda n: pl.BlockSpec((1, tm, n), lambda i, j: (i, j, 0))
    full = lambda a: pl.BlockSpec(a.shape, lambda i, j: (0,) * a.ndim, pipeline_mode=pl.Buffered(1))
    if transpose_v:
        assert tm == KB
        vvb_shape = jax.ShapeDtypeStruct((b, t // KB, N_KV_HEADS * HEAD_DIM, KB), BF16)
        vvb_spec = pl.BlockSpec((1, 1, N_KV_HEADS * HEAD_DIM, KB), lambda i, j: (i, j, 0, 0))
    else:
        vvb_shape = jax.ShapeDtypeStruct((b, t, N_KV_HEADS * HEAD_DIM), BF16)
        vvb_spec = row(N_KV_HEADS * HEAD_DIM)
    out_shapes = (
        jax.ShapeDtypeStruct((b, t, D_MODEL), F32),
        jax.ShapeDtypeStruct((b, t, D_MODEL), BF16),
        jax.ShapeDtypeStruct((b, t, D_MODEL), F32),
        jax.ShapeDtypeStruct((b, t, N_KV_HEADS * HEAD_DIM), F32),
        jax.ShapeDtypeStruct((b, t, 4 * LANES), BF16),
        jax.ShapeDtypeStruct((b, t, N_KV_HEADS * HEAD_DIM), F32),
        vvb_shape,
        jax.ShapeDtypeStruct((b, t, IDX_HEADS * IDX_DIM), BF16),
        jax.ShapeDtypeStruct((b, t, LANES), BF16),
        jax.ShapeDtypeStruct((b, t, LANES), F32),
        jax.ShapeDtypeStruct((b, SUBLANES, D_MODEL), F32),
    )
    out_specs = (row(D_MODEL), row(D_MODEL), row(D_MODEL), row(N_KV_HEADS * HEAD_DIM), row(4 * LANES),
                 row(N_KV_HEADS * HEAD_DIM), vvb_spec, row(IDX_HEADS * IDX_DIM), row(LANES), row(LANES),
                 pl.BlockSpec((1, SUBLANES, D_MODEL), lambda i, j: (i, 0, 0)))
    in_specs = [row(D_MODEL), pl.BlockSpec((1, SUBLANES, D_MODEL), lambda i, j: (i, 0, 0)),
                full(w['a']), full(w['q']), full(w['gb']), full(w['kk']), full(w['vv']), full(w['qi']),
                full(w['ki']), full(conv_w)]
    return pl.pallas_call(
        functools.partial(_inproj_kernel, tm=tm, transpose_v=transpose_v),
        grid=grid, in_specs=in_specs, out_specs=out_specs, out_shape=out_shapes,
        scratch_shapes=[pltpu.VMEM((tm + SUBLANES, D_MODEL), F32)],
        compiler_params=pltpu.CompilerParams(dimension_semantics=("arbitrary", "arbitrary"),
                                             vmem_limit_bytes=VMEM_LIMIT),
        name="inproj_conv",
    )(x, prev8, w['a'], w['q'], w['gb'], w['kk'], w['vv'], w['qi'], w['ki'], conv_w)


def _bias_tiles_kernel(tbl_ref, out_ref, *, nq):
    d = pl.program_id(0) - 2
    krow = lax.broadcasted_iota(I32, (LANES, nq), 0)
    qcol = lax.broadcasted_iota(I32, (LANES, nq), 1)
    rel = d * LANES + krow - qcol
    n = jnp.abs(rel)
    nb = NUM_BUCKETS // 2
    max_exact = nb // 2
    n2 = n * n
    large = jnp.full((LANES, nq), max_exact, I32)
    for j in range(1, nb - max_exact):
        large = large + (n2 >= (max_exact * max_exact) * (1 << j)).astype(I32)
    bucket = jnp.where(n < max_exact, n, large) + jnp.where(rel > 0, nb, 0)
    far = pl.program_id(0) == 0
    for h in range(N_HEADS):
        base = tbl_ref[nb - 1, h]
        acc = jnp.zeros((LANES, nq), F32)
        for bk in range(NUM_BUCKETS):
            acc = jnp.where(bucket == bk, (tbl_ref[bk, h] - base) * LOG2E, acc)
        out_ref[0, h] = jnp.where(far, 0.0, acc)


def _bias_tiles(rel_bias, nq):
    assert MAX_DISTANCE == 128 and NUM_BUCKETS == 32
    return pl.pallas_call(
        functools.partial(_bias_tiles_kernel, nq=nq),
        grid=(N_BIAS_TILES,),
        in_specs=[pl.BlockSpec(memory_space=pltpu.SMEM)],
        out_specs=pl.BlockSpec((1, N_HEADS, LANES, nq), lambda i: (i, 0, 0, 0)),
        out_shape=jax.ShapeDtypeStruct((N_BIAS_TILES, N_HEADS, LANES, nq), F32),
        name="bias_tiles",
    )(rel_bias)


def _attn_kernel(q_ref, qi_ref, kiwi_ref, conv_ref, sgb_ref, x_ref, kk_ref, vvt_ref, kiki_ref, bt_ref,
                 wout_ref, g_ref, b_ref, out_ref,
                 key_ref, mask_ref, qis_ref, tri_ref, qs_ref, acc_ref, merged_ref, hi_ref, dig_ref,
                 *, alpha, nq, q_off, kv_len, topk):
    q0 = q_off + pl.program_id(1) * nq
    nkb = (q0 + nq + KB - 1) // KB
    jd = q0 // LANES
    lane = lax.broadcasted_iota(I32, (nq, LANES), 1)
    low = lane < HEAD_DIM
    krow = lax.broadcasted_iota(I32, (LANES, nq), 0)
    qchunk = lax.shift_right_arithmetic(q0 + lax.broadcasted_iota(I32, (LANES, nq), 1), CHUNK_SHIFT)

    wit = kiwi_ref[0].T * ((IDX_DIM ** -0.5) * (IDX_HEADS ** -0.5))
    wis = [wit[IDX_DIM + h:IDX_DIM + h + 1, :] for h in range(IDX_HEADS)]
    for h in range(IDX_HEADS):
        pair = qi_ref[0, :, (h // 2) * LANES:(h // 2 + 1) * LANES].astype(F32)
        keep = low if h % 2 == 0 else jnp.logical_not(low)
        qis_ref[h] = jnp.where(keep, pair, 0.0).astype(BF16)

    def score_block(jb, carry):
        ks = pl.multiple_of(jb * KB, KB)
        ki = kiki_ref[0, pl.ds(ks, KB), :]
        sc = [jnp.zeros((LANES, nq), F32) for _ in range(TILES)]
        for h in range(IDX_HEADS):
            s = _dot_nt(ki, qis_ref[h])
            for c in range(TILES):
                sc[c] = sc[c] + jnp.maximum(s[c * LANES:(c + 1) * LANES], 0.0) * wis[h]
        for c in range(TILES):
            bits = lax.bitcast_convert_type(sc[c], I32)
            key = jnp.where(bits < 0, bits ^ 0x7FFFFFFF, bits)
            kpos = ks + c * LANES + krow
            adm = (lax.shift_right_arithmetic(kpos, CHUNK_SHIFT) <= qchunk) & (kpos < kv_len)
            key_ref[pl.ds(ks + c * LANES, LANES), :] = jnp.where(adm, key, INT_MIN)
        return carry

    lax.fori_loop(0, nkb, score_block, 0)

    def count(pred):
        def body(jb, acc):
            ks = pl.multiple_of(jb * KB, KB)
            parts = []
            for c in range(TILES):
                blk = key_ref[pl.ds(ks + c * LANES, LANES), :]
                parts.append(_fold8(pred(blk, ks + c * LANES).astype(I32)))
            return acc + _tree(jnp.add, parts)
        acc = lax.fori_loop(0, nkb, body, jnp.zeros((SUBLANES, nq), I32))
        return jnp.sum(acc.astype(F32), axis=0, keepdims=True).astype(I32)

    pack = 2 * SUBLANES

    def count16(ref, cand):
        def body(jb, acc):
            ks = pl.multiple_of(jb * KB, KB)
            parts = []
            for c in range(TILES):
                hit = jnp.where(ref[pl.ds(ks + c * LANES, LANES), :] >= cand, jnp.ones((), I16), jnp.zeros((), I16))
                parts.append(_tree(jnp.add, [hit[i:i + pack] for i in range(0, LANES, pack)]))
            return acc + _tree(jnp.add, parts)
        acc = lax.fori_loop(0, nkb, body, jnp.zeros((pack, nq), I16))
        return jnp.sum(acc.astype(I32).astype(F32), axis=0, keepdims=True).astype(I32)

    def digit_search(ref, nbits, start, need):
        def step(i, t):
            cand = t + lax.shift_left(jnp.int32(1), nbits - 1 - i)
            return jnp.where(count16(ref, cand.astype(I16)) >= need, cand, t)
        return lax.fori_loop(0, nbits, step, start)

    def fill(ref, fn):
        def body(jb, carry):
            ks = pl.multiple_of(jb * KB, KB)
            for c in range(TILES):
                rows = pl.ds(ks + c * LANES, LANES)
                ref[rows, :] = fn(key_ref[rows, :], rows)
            return carry
        lax.fori_loop(0, nkb, body, 0)

    digit_min = -2 ** 15
    zero = jnp.zeros((1, nq), I32)
    fill(hi_ref, lambda k, rows: lax.shift_right_arithmetic(k, 16).astype(I16))
    top = digit_search(hi_ref, 16, jnp.full((1, nq), digit_min, I32), topk)
    found = top > digit_min
    above = jnp.where(top < 2 ** 15 - 1, count16(hi_ref, jnp.minimum(top + 1, 2 ** 15 - 1).astype(I16)), 0)
    need = topk - above
    top16 = top.astype(I16)
    fill(dig_ref, lambda k, rows: jnp.where(
        hi_ref[rows, :] == top16, (lax.shift_right_logical(k, 8) & 0xFF).astype(I16), jnp.full((), -1, I16)))
    mid = digit_search(dig_ref, 8, zero, need)
    need = need - count16(dig_ref, (mid + 1).astype(I16))
    mid16 = mid.astype(I16)
    fill(dig_ref, lambda k, rows: jnp.where(dig_ref[rows, :] == mid16, (k & 0xFF).astype(I16), jnp.full((), -1, I16)))
    bottom = digit_search(dig_ref, 8, zero, need)
    thr = jnp.where(found, lax.shift_left(top, 16) | lax.shift_left(mid, 8) | bottom, INT_MIN)

    want = (topk - count(lambda blk, base: blk > thr)).astype(F32)
    live = thr > INT_MIN
    tri_ref[...] = (lax.broadcasted_iota(I32, (KB, KB), 0) > lax.broadcasted_iota(I32, (KB, KB), 1)
                    ).astype(F32).astype(BF16)

    def mask_block(jb, seen):
        ks = pl.multiple_of(jb * KB, KB)
        keys = [key_ref[pl.ds(ks + c * LANES, LANES), :] for c in range(TILES)]
        ties = [((k == thr) & live).astype(F32) for k in keys]
        before = _dot(tri_ref[...], jnp.concatenate(ties, axis=0).astype(BF16)) + seen
        for c in range(TILES):
            keep = (keys[c] > thr) | ((ties[c] > 0.0) & (before[c * LANES:(c + 1) * LANES] < want))
            mask_ref[pl.ds(ks + c * LANES, LANES), :] = jnp.where(keep, 0.0, NEG)
        return seen + jnp.sum(_fold8(_tree(jnp.add, ties)), axis=0, keepdims=True)

    lax.fori_loop(0, nkb, mask_block, jnp.zeros((1, nq), F32))

    def head_step(s, msks, biases, m):
        ss = [s[c * LANES:(c + 1) * LANES] + msks[c] for c in range(TILES)]
        if biases is not None:
            ss = [x + b for x, b in zip(ss, biases)]
        m_new = jnp.maximum(m, jnp.max(_tree(jnp.maximum, ss), axis=0, keepdims=True))
        return m_new, jnp.exp2(m - m_new), jnp.concatenate([jnp.exp2(x - m_new) for x in ss], axis=0).astype(BF16)

    for p0 in range(0, N_PAIRS, PAIRS_PER_LOOP):
        pairs = tuple(range(p0, p0 + PAIRS_PER_LOOP))
        for p in pairs:
            qp = q_ref[0, :, p * LANES:(p + 1) * LANES].astype(F32)
            qs_ref[2 * p] = jnp.where(low, qp, 0.0).astype(BF16)
            qs_ref[2 * p + 1] = jnp.where(low, 0.0, qp).astype(BF16)

        def attn_block(jb, carry, with_bias):
            ks = pl.multiple_of(jb * KB, KB)
            msks = [mask_ref[pl.ds(ks + c * LANES, LANES), :] for c in range(TILES)]
            tiles = [jnp.clip(jb * TILES + c - jd + 2, 0, N_BIAS_TILES - 1) for c in range(TILES)]

            def logits(head):
                g = head // GROUP
                kblk = kk_ref[0, pl.ds(ks, KB), g * LANES:(g + 1) * LANES]
                return _dot_nt(kblk, qs_ref[head])

            heads = [2 * p + r for p in pairs for r in range(2)]
            pending = [logits(h) for h in heads[:LOOKAHEAD]]
            vts = {}
            out = []
            for i, p in enumerate(pairs):
                g = (2 * p) // GROUP
                if g not in vts:
                    vts[g] = jnp.concatenate([vvt_ref[0, jb, g * HEAD_DIM:(g + 1) * HEAD_DIM, :],
                                              jnp.ones((ONES_ROWS, KB), BF16)], axis=0)
                stats = []
                for r, (m, l) in enumerate(carry[i]):
                    nxt = 2 * i + r + LOOKAHEAD
                    if nxt < len(heads):
                        pending.append(logits(heads[nxt]))
                    biases = [bt_ref[tiles[c], 2 * p + r] for c in range(TILES)] if with_bias else None
                    m, a, pr = head_step(pending.pop(0), msks, biases, m)
                    pv = _dot(vts[g], pr)
                    acc_ref[2 * p + r] = a * acc_ref[2 * p + r] + pv[:HEAD_DIM]
                    stats.append((m, a * l + pv[HEAD_DIM:HEAD_DIM + 1]))
                out.append(tuple(stats))
            return tuple(out)

        for p in pairs:
            acc_ref[2 * p] = jnp.zeros((HEAD_DIM, nq), F32)
            acc_ref[2 * p + 1] = jnp.zeros((HEAD_DIM, nq), F32)
        init = ((jnp.full((1, nq), NEG, F32), jnp.zeros((1, nq), F32)),) * 2
        nfar = jnp.clip((jd - 1) // TILES, 0, nkb)
        res = lax.fori_loop(0, nfar, functools.partial(attn_block, with_bias=False), (init,) * len(pairs))
        res = lax.fori_loop(nfar, nkb, functools.partial(attn_block, with_bias=True), res)
        for i, p in enumerate(pairs):
            (_, la), (_, lb) = res[i]
            o = jnp.concatenate([acc_ref[2 * p] / la, acc_ref[2 * p + 1] / lb], axis=0).T
            sl = slice(p * LANES, (p + 1) * LANES)
            merged_ref[:, sl] = (conv_ref[0, :, sl] + sgb_ref[0, :, sl] * o).astype(BF16)

    y = alpha * x_ref[0] + _dot(merged_ref[...], wout_ref[...])
    out_ref[0] = _layer_norm(y, g_ref[...], b_ref[...])


def _attention(q, qi, kiwi, conv, sgb, x, kk, vvt, kiki, bias_tiles, w_out, g, b_ln, *, alpha, nq, q_off, kv_len,
               topk):
    b, t, _ = q.shape
    lpad = kk.shape[1]
    assert q_off % LANES == 0 and nq % LANES == 0 and t % nq == 0
    assert lpad % KB == 0 and lpad >= q_off + t and vvt.shape == (b, lpad // KB, N_KV_HEADS * HEAD_DIM, KB)
    grid = (b, t // nq)
    row = lambda n: pl.BlockSpec((1, nq, n), lambda i, j: (i, j, 0))
    keys = lambda n: pl.BlockSpec((1, lpad, n), lambda i, j: (i, 0, 0), pipeline_mode=pl.Buffered(1))
    const = lambda a: pl.BlockSpec(a.shape, lambda i, j: (0,) * a.ndim, pipeline_mode=pl.Buffered(1))
    return pl.pallas_call(
        functools.partial(_attn_kernel, alpha=alpha, nq=nq, q_off=q_off, kv_len=kv_len, topk=topk),
        grid=grid,
        in_specs=[row(D_MODEL), row(IDX_HEADS * IDX_DIM), row(LANES), row(D_MODEL), row(D_MODEL), row(D_MODEL),
                  keys(4 * LANES),
                  pl.BlockSpec((1, lpad // KB, N_KV_HEADS * HEAD_DIM, KB), lambda i, j: (i, 0, 0, 0),
                               pipeline_mode=pl.Buffered(1)),
                  keys(LANES), const(bias_tiles), const(w_out), const(g), const(b_ln)],
        out_specs=row(D_MODEL),
        out_shape=jax.ShapeDtypeStruct((b, t, D_MODEL), F32),
        scratch_shapes=[pltpu.VMEM((lpad, nq), I32), pltpu.VMEM((lpad, nq), F32),
                        pltpu.VMEM((IDX_HEADS, nq, LANES), BF16), pltpu.VMEM((KB, KB), BF16),
                        pltpu.VMEM((N_HEADS, nq, LANES), BF16), pltpu.VMEM((N_HEADS, HEAD_DIM, nq), F32),
                        pltpu.VMEM((nq, D_MODEL), BF16), pltpu.VMEM((lpad, nq), I16), pltpu.VMEM((lpad, nq), I16)],
        compiler_params=pltpu.CompilerParams(dimension_semantics=("arbitrary", "arbitrary"),
                                             vmem_limit_bytes=VMEM_LIMIT),
        name="dsa_attention",
    )(q, qi, kiwi, conv, sgb, x, kk, vvt, kiki, bias_tiles, w_out, g, b_ln)


def _ffn_kernel(x_ref, wg_ref, wu_ref, wd_ref, g_ref, b_ref, out_ref, *, alpha):
    x = x_ref[...]
    xb = x.astype(BF16)
    hg = _dot(xb, wg_ref[...])
    h = (hg * jax.nn.sigmoid(hg)) * _dot(xb, wu_ref[...])
    f = _dot(h.astype(BF16), wd_ref[...])
    out_ref[...] = _layer_norm(alpha * x + f, g_ref[...], b_ref[...])


def _ffn_ln(x, wg, wu, wd, g, b, alpha, tm):
    n = x.shape[0]
    row = pl.BlockSpec((tm, D_MODEL), lambda i: (i, 0))
    vec = pl.BlockSpec((1, D_MODEL), lambda i: (0, 0))
    full = lambda a: pl.BlockSpec(a.shape, lambda i: (0, 0), pipeline_mode=pl.Buffered(1))
    return pl.pallas_call(
        functools.partial(_ffn_kernel, alpha=alpha),
        grid=(n // tm,),
        in_specs=[row, full(wg), full(wu), full(wd), vec, vec],
        out_specs=row,
        out_shape=jax.ShapeDtypeStruct((n, D_MODEL), F32),
        compiler_params=pltpu.CompilerParams(dimension_semantics=("arbitrary",), vmem_limit_bytes=VMEM_LIMIT),
        name="ffn_ln",
    )(x, wg, wu, wd, g, b)


def _moe_kernel(x_ref, rw_ref, rb_ref, wgu_ref, wd_ref, g_ref, b_ref, out_ref,
                rank_ref, rankt_ref, combt_ref, xb_ref, acc_ref, *, alpha, tm):
    e = pl.program_id(1)
    sub = min(MOE_SUB, tm)
    assert sub & (sub - 1) == 0 and tm % sub == 0
    ch = min(MOE_CHUNK, sub)
    lane = lax.broadcasted_iota(I32, (tm, LANES), 1)

    @pl.when(e == 0)
    def _():
        x = x_ref[...]
        x_hi = x.astype(BF16)
        x_lo = (x - x_hi.astype(F32)).astype(BF16)
        logits = (_dot(x_hi, rw_ref[0]) + _dot(x_hi, rw_ref[1]) + _dot(x_lo, rw_ref[0])) + rb_ref[...]
        logits = jnp.where(lane < N_EXPERTS, logits, -jnp.inf)
        lanef = lane.astype(F32)
        v1 = jnp.max(logits, axis=1, keepdims=True)
        i1 = jnp.min(jnp.where(logits == v1, lanef, float(LANES)), axis=1, keepdims=True)
        rest = jnp.where(lanef == i1, -jnp.inf, logits)
        v2 = jnp.max(rest, axis=1, keepdims=True)
        i2 = jnp.min(jnp.where(rest == v2, lanef, float(LANES)), axis=1, keepdims=True)
        e2 = jnp.exp(v2 - v1)
        den = 1.0 + e2
        comb = jnp.where(lanef == i1, 1.0 / den, 0.0) + jnp.where(lanef == i2, e2 / den, 0.0)
        routed = (lanef == i1) | (lanef == i2)
        tr = lax.broadcasted_iota(I32, (tm, tm), 0)
        tc = lax.broadcasted_iota(I32, (tm, tm), 1)
        earlier = (tr > tc) & (tc >= tr - (tr & (sub - 1)))
        rank = _dot(earlier.astype(F32).astype(BF16), routed.astype(F32).astype(BF16))
        rank = jnp.where(routed, rank, -1.0)
        rank_ref[...] = rank.astype(I32)
        rankt_ref[...] = rank.T.astype(I32)
        combt_ref[...] = comb.T
        xb_ref[...] = x_ref[...].astype(BF16)
        acc_ref[...] = jnp.zeros((tm, D_MODEL), F32)

    rcol = jnp.max(jnp.where(lane == e, rank_ref[...], -1).astype(F32), axis=1, keepdims=True).astype(I32)
    rrow = rankt_ref[pl.ds(e, 1), :]
    crow = combt_ref[pl.ds(e, 1), :]

    for s0 in range(0, tm, sub):
        rcol_s, rrow_s, crow_s = rcol[s0:s0 + sub], rrow[:, s0:s0 + sub], crow[:, s0:s0 + sub]
        n_routed = jnp.max(rcol_s.astype(F32)).astype(I32) + 1

        def chunk(c, carry):
            base = c * ch
            hit = rrow_s == lax.broadcasted_iota(I32, (ch, sub), 0) + base
            xg = _dot(jnp.where(hit, 1.0, 0.0).astype(BF16), xb_ref[s0:s0 + sub, :]).astype(BF16)
            hgu = _dot(xg, wgu_ref[...])
            hg = hgu[:, :D_FF_EXPERT]
            h = (hg * jax.nn.sigmoid(hg)) * hgu[:, D_FF_EXPERT:]
            f = _dot(h.astype(BF16), wd_ref[...])
            gate = jnp.sum(jnp.where(hit, crow_s, 0.0), axis=1, keepdims=True)
            hit_t = rcol_s == lax.broadcasted_iota(I32, (sub, ch), 1) + base
            acc_ref[s0:s0 + sub, :] += _dot(jnp.where(hit_t, 1.0, 0.0).astype(BF16), (f * gate).astype(BF16))
            return carry

        lax.fori_loop(0, (n_routed + ch - 1) // ch, chunk, 0)

    @pl.when(e == N_EXPERTS - 1)
    def _():
        out_ref[...] = _layer_norm(alpha * x_ref[...] + acc_ref[...], g_ref[...], b_ref[...])


def _moe_ln(x, rw, rb, wgu, wd, g, b, alpha, tm):
    n = x.shape[0]
    row = pl.BlockSpec((tm, D_MODEL), lambda i, e: (i, 0))
    vec = pl.BlockSpec((1, D_MODEL), lambda i, e: (0, 0))
    return pl.pallas_call(
        functools.partial(_moe_kernel, alpha=alpha, tm=tm),
        grid=(n // tm, N_EXPERTS),
        in_specs=[row, pl.BlockSpec((2, D_MODEL, LANES), lambda i, e: (0, 0, 0)),
                  pl.BlockSpec((1, LANES), lambda i, e: (0, 0)),
                  pl.BlockSpec((None, D_MODEL, 2 * D_FF_EXPERT), lambda i, e: (e, 0, 0)),
                  pl.BlockSpec((None, D_FF_EXPERT, D_MODEL), lambda i, e: (e, 0, 0)),
                  vec, vec],
        out_specs=row,
        out_shape=jax.ShapeDtypeStruct((n, D_MODEL), F32),
        scratch_shapes=[pltpu.VMEM((tm, LANES), I32), pltpu.VMEM((LANES, tm), I32), pltpu.VMEM((LANES, tm), F32),
                        pltpu.VMEM((tm, D_MODEL), BF16), pltpu.VMEM((tm, D_MODEL), F32)],
        compiler_params=pltpu.CompilerParams(dimension_semantics=("arbitrary", "arbitrary"),
                                             vmem_limit_bytes=VMEM_LIMIT),
        name="moe_ln",
    )(x, rw, rb, wgu, wd, g, b)


def _split_w_in(w):
    sizes = (D_MODEL, D_MODEL, D_MODEL, N_HEADS * HEAD_DIM, N_KV_HEADS * HEAD_DIM, N_KV_HEADS * HEAD_DIM,
             IDX_HEADS * IDX_DIM, IDX_DIM, IDX_HEADS, D_MODEL, D_MODEL)
    parts, start = [], 0
    for n in sizes:
        parts.append(w[:, start:start + n])
        start += n
    bg, cg, xin, q, k, v, qi, ki, wi, ga, gb = parts

    def dup(a):
        a = a.reshape(a.shape[0], -1, HEAD_DIM)
        return jnp.concatenate([a, a], axis=-1).reshape(a.shape[0], -1)

    kiwi = jnp.concatenate([ki, wi, jnp.zeros((w.shape[0], LANES - IDX_DIM - IDX_HEADS), w.dtype)], axis=1)
    return {
        'a': jnp.concatenate([bg, cg, xin, ga], axis=1).astype(BF16),
        'q': q.astype(BF16), 'gb': gb.astype(BF16), 'kk': dup(k).astype(BF16), 'vv': v.astype(BF16),
        'qi': qi.astype(BF16), 'ki': jnp.concatenate([ki, ki, kiwi], axis=1).astype(BF16),
    }


def _dup_cache(a):
    a = a.astype(BF16)
    return jnp.concatenate([a, a], axis=-1).reshape(a.shape[0], a.shape[1], -1)


def _pad_rows(a, n):
    return jnp.pad(a, ((0, 0), (0, n - a.shape[1]), (0, 0)))


def _row_tile(n, cap):
    tm = min(n, cap)
    assert n % tm == 0
    return tm


def kernel(x_prompt, x_sample, cache_k, cache_v, cache_kidx, state_conv, w_in, conv_w, w_out, rel_bias,
           ln1_g, ln1_b, ln2_g, ln2_b, ffn_w_gate, ffn_w_up, ffn_w_down, router_w, router_b,
           moe_w_gate, moe_w_up, moe_w_down):
    depth = w_in.shape[0]
    alpha = (2 * depth) ** 0.25
    bp, tp, _ = x_prompt.shape
    bs, ts, _ = x_sample.shape
    past = cache_k.shape[2]
    nq_p = 256 if tp % 256 == 0 else LANES
    nq_s = LANES
    assert tp % KB == 0 and past % LANES == 0 and ts <= nq_s and ts % SUBLANES == 0
    topk_p = min(TOPK_MAX, tp // 4)
    topk_s = min(TOPK_MAX, (past + ts) // 4)
    lpad_s = -(-(past + nq_s) // KB) * KB

    bt_p = _bias_tiles(rel_bias, nq_p)
    bt_s = _bias_tiles(rel_bias, nq_s)
    vec = lambda a: a.reshape(1, -1)

    def layer(l, x, prev8, past_kv):
        b, t, _ = x.shape
        w = _split_w_in(w_in[l])
        prompt = past_kv is None
        (conv, q, sgb, kkf, kkb, vvf, vvb, qi, kiki, kiwi, cstate) = _inproj(
            x, prev8, w, conv_w[l], _row_tile(t, KB), transpose_v=prompt)
        proj = (w_out[l].astype(BF16), vec(ln1_g[l]), vec(ln1_b[l]))
        if prompt:
            x1 = _attention(q, qi, kiwi, conv, sgb, x, kkb, vvb, kiki, bt_p, *proj,
                            alpha=alpha, nq=nq_p, q_off=0, kv_len=t, topk=topk_p)
        else:
            ck, cv, cki = past_kv
            kk_all = _pad_rows(jnp.concatenate([_dup_cache(ck), kkb], axis=1), lpad_s)
            cv = cv.astype(BF16).reshape(b, cv.shape[1], N_KV_HEADS * HEAD_DIM)
            vv_all = _pad_rows(jnp.concatenate([cv, vvb], axis=1), lpad_s)
            vvt_all = jnp.swapaxes(vv_all.reshape(b, lpad_s // KB, KB, N_KV_HEADS * HEAD_DIM), 2, 3)
            cki = cki.astype(BF16)
            kiki_all = _pad_rows(jnp.concatenate([jnp.concatenate([cki, cki], axis=-1), kiki], axis=1), lpad_s)
            pad = lambda a: _pad_rows(a, nq_s)
            x1 = _attention(pad(q), pad(qi), pad(kiwi), pad(conv), pad(sgb), pad(x), kk_all, vvt_all, kiki_all, bt_s,
                            *proj, alpha=alpha, nq=nq_s, q_off=past, kv_len=past + t, topk=topk_s)[:, :t]
        n = b * t
        x1 = x1.reshape(n, D_MODEL)
        j = l // 2
        if l % 2 == 0:
            x2 = _ffn_ln(x1, ffn_w_gate[j].astype(BF16), ffn_w_up[j].astype(BF16), ffn_w_down[j].astype(BF16),
                         vec(ln2_g[l]), vec(ln2_b[l]), alpha, _row_tile(n, 512))
        else:
            rw = jnp.pad(router_w[j], ((0, 0), (0, LANES - N_EXPERTS)))
            rw_hi = rw.astype(BF16)
            rw = jnp.stack([rw_hi, (rw - rw_hi.astype(F32)).astype(BF16)])
            rb = jnp.pad(router_b[j], (0, LANES - N_EXPERTS)).reshape(1, LANES)
            wgu = jnp.concatenate([moe_w_gate[j], moe_w_up[j]], axis=-1).astype(BF16)
            x2 = _moe_ln(x1, rw, rb, wgu, moe_w_down[j].astype(BF16), vec(ln2_g[l]), vec(ln2_b[l]), alpha,
                         _row_tile(n, 2 * MOE_SUB))
        heads = lambda a: a.reshape(b, t, N_KV_HEADS, HEAD_DIM)
        return (x2.reshape(b, t, D_MODEL), heads(kkf), heads(vvf), kiwi[..., :IDX_DIM],
                cstate[:, SUBLANES - (CONV_W - 1):, :])

    xp, xs = x_prompt, x_sample
    outs_p, outs_s = [], []
    zero_prev = jnp.zeros((bp, SUBLANES, D_MODEL), F32)
    for l in range(depth):
        xp, *rest = layer(l, xp, zero_prev, None)
        outs_p.append(rest)
        prev8 = jnp.pad(state_conv[l], ((0, 0), (SUBLANES - (CONV_W - 1), 0), (0, 0)))
        xs, *rest = layer(l, xs, prev8, (cache_k[l], cache_v[l], cache_kidx[l]))
        outs_s.append(rest)
    stack = lambda outs, i: jnp.stack([o[i] for o in outs])
    return (xp, xs,
            stack(outs_p, 0), stack(outs_p, 1), stack(outs_p, 2), stack(outs_p, 3),
            stack(outs_s, 0), stack(outs_s, 1), stack(outs_s, 2), stack(outs_s, 3))
```

```python
import functools

import jax
import jax.numpy as jnp
from jax import lax
from jax.experimental import pallas as pl
from jax.experimental.pallas import tpu as pltpu

D_MODEL = 1024
CHUNK = 64
CHUNK_SHIFT = 6
CONV_W = 3
N_HEADS = 16
HEAD_DIM = 64
N_KV_HEADS = 4
GROUP = N_HEADS // N_KV_HEADS
IDX_HEADS = 8
IDX_DIM = 64
TOPK_MAX = 256
NUM_BUCKETS = 32
MAX_DISTANCE = 128
N_EXPERTS = 8
D_FF_EXPERT = 1408
LN_EPS = 1e-5

LANES = 128
SUBLANES = 8
N_PAIRS = N_HEADS // 2
KB = 512
TILES = KB // LANES
N_BIAS_TILES = 4
PAIRS_PER_LOOP = 8
LOOKAHEAD = 3
ONES_ROWS = 16
MOE_SUB = 512
MOE_CHUNK = 160
NEG = -1e30
LOG2E = 1.4426950408889634
INT_MIN = -2 ** 31
VMEM_LIMIT = 56 * 1024 * 1024

F32 = jnp.float32
BF16 = jnp.bfloat16
I32 = jnp.int32
I16 = jnp.int16

assert CHUNK == 1 << CHUNK_SHIFT and 2 * HEAD_DIM == LANES and 2 * IDX_DIM == LANES


def _dot(a, b):
    return jnp.dot(a, b, preferred_element_type=F32)


def _dot_nt(a, b):
    return lax.dot_general(a, b, (((1,), (1,)), ((), ())), preferred_element_type=F32)


def _layer_norm(y, g, b):
    mu = jnp.mean(y, axis=-1, keepdims=True)
    d = y - mu
    var = jnp.mean(d * d, axis=-1, keepdims=True)
    return d * lax.rsqrt(var + LN_EPS) * g + b


def _tree(op, xs):
    xs = list(xs)
    while len(xs) > 1:
        xs = [op(xs[i], xs[i + 1]) for i in range(0, len(xs) - 1, 2)] + ([xs[-1]] if len(xs) % 2 else [])
    return xs[0]


def _fold8(x):
    return _tree(jnp.add, [x[i:i + SUBLANES] for i in range(0, x.shape[0], SUBLANES)])


def _inproj_kernel(x_ref, prev_ref, wa_ref, wq_ref, wgb_ref, wkk_ref, wvv_ref, wqi_ref, wki_ref, cw_ref,
                   conv_ref, q_ref, sgb_ref, kkf_ref, kkb_ref, vvf_ref, vvb_ref, qi_ref, kiki_ref, kiwi_ref,
                   cstate_ref, uext_ref, *, tm, transpose_v):
    @pl.when(pl.program_id(1) == 0)
    def _():
        uext_ref[0:SUBLANES, :] = prev_ref[0]

    xb = x_ref[0].astype(BF16)
    cc = 256
    for c in range(0, D_MODEL, cc):
        bg = _dot(xb, wa_ref[:, c:c + cc])
        cg = _dot(xb, wa_ref[:, D_MODEL + c:D_MODEL + c + cc])
        xin = _dot(xb, wa_ref[:, 2 * D_MODEL + c:2 * D_MODEL + c + cc])
        ga = _dot(xb, wa_ref[:, 3 * D_MODEL + c:3 * D_MODEL + c + cc])
        u = cg * xin
        uext_ref[SUBLANES:SUBLANES + tm, c:c + cc] = u
        um1 = uext_ref[SUBLANES - 1:SUBLANES - 1 + tm, c:c + cc]
        um2 = uext_ref[SUBLANES - 2:SUBLANES - 2 + tm, c:c + cc]
        y = um2 * cw_ref[0:1, c:c + cc] + um1 * cw_ref[1:2, c:c + cc] + u * cw_ref[2:3, c:c + cc]
        conv_ref[0, :, c:c + cc] = jax.nn.sigmoid(ga) * (bg * y)
    tail = uext_ref[tm:tm + SUBLANES, :]
    cstate_ref[0] = tail
    uext_ref[0:SUBLANES, :] = tail

    q_ref[0] = (_dot(xb, wq_ref[...]) * (HEAD_DIM ** -0.5 * LOG2E)).astype(BF16)
    sgb_ref[0] = jax.nn.sigmoid(_dot(xb, wgb_ref[...]))
    def undup(a):
        low = lax.broadcasted_iota(I32, (tm, LANES), 1) < HEAD_DIM
        return jnp.concatenate(
            [jnp.where(low, a[:, 2 * j * LANES:(2 * j + 1) * LANES], a[:, (2 * j + 1) * LANES:(2 * j + 2) * LANES])
             for j in range(N_KV_HEADS // 2)], axis=1)

    kk = _dot(xb, wkk_ref[...])
    kkf_ref[0] = undup(kk)
    kkb_ref[0] = kk.astype(BF16)
    vv = _dot(xb, wvv_ref[...])
    vvf_ref[0] = vv
    if transpose_v:
        vvb_ref[0, 0] = vv.T.astype(BF16)
    else:
        vvb_ref[0] = vv.astype(BF16)
    qi_ref[0] = _dot(xb, wqi_ref[...]).astype(BF16)
    kw = _dot(xb, wki_ref[...])
    kiki_ref[0] = kw[:, 0:LANES].astype(BF16)
    kiwi_ref[0] = kw[:, LANES:2 * LANES]


def _inproj(x, prev8, w, conv_w, tm, transpose_v):
    b, t, _ = x.shape
    grid = (b, t // tm)
    row = lambda n: pl.BlockSpec((1, tm, n), lambda i, j: (i, j, 0))
    full = lambda a: pl.BlockSpec(a.shape, lambda i, j: (0,) * a.ndim, pipeline_mode=pl.Buffered(1))
    if transpose_v:
        assert tm == KB
        vvb_shape = jax.ShapeDtypeStruct((b, t // KB, N_KV_HEADS * HEAD_DIM, KB), BF16)
        vvb_spec = pl.BlockSpec((1, 1, N_KV_HEADS * HEAD_DIM, KB), lambda i, j: (i, j, 0, 0))
    else:
        vvb_shape = jax.ShapeDtypeStruct((b, t, N_KV_HEADS * HEAD_DIM), BF16)
        vvb_spec = row(N_KV_HEADS * HEAD_DIM)
    out_shapes = (
        jax.ShapeDtypeStruct((b, t, D_MODEL), F32),
        jax.ShapeDtypeStruct((b, t, D_MODEL), BF16),
        jax.ShapeDtypeStruct((b, t, D_MODEL), F32),
        jax.ShapeDtypeStruct((b, t, N_KV_HEADS * HEAD_DIM), F32),
        jax.ShapeDtypeStruct((b, t, 4 * LANES), BF16),
        jax.ShapeDtypeStruct((b, t, N_KV_HEADS * HEAD_DIM), F32),
        vvb_shape,
        jax.ShapeDtypeStruct((b, t, IDX_HEADS * IDX_DIM), BF16),
        jax.ShapeDtypeStruct((b, t, LANES), BF16),
        jax.ShapeDtypeStruct((b, t, LANES), F32),
        jax.ShapeDtypeStruct((b, SUBLANES, D_MODEL), F32),
    )
    out_specs = (row(D_MODEL), row(D_MODEL), row(D_MODEL), row(N_KV_HEADS * HEAD_DIM), row(4 * LANES),
                 row(N_KV_HEADS * HEAD_DIM), vvb_spec, row(IDX_HEADS * IDX_DIM), row(LANES), row(LANES),
                 pl.BlockSpec((1, SUBLANES, D_MODEL), lambda i, j: (i, 0, 0)))
    in_specs = [row(D_MODEL), pl.BlockSpec((1, SUBLANES, D_MODEL), lambda i, j: (i, 0, 0)),
                full(w['a']), full(w['q']), full(w['gb']), full(w['kk']), full(w['vv']), full(w['qi']),
                full(w['ki']), full(conv_w)]
    return pl.pallas_call(
        functools.partial(_inproj_kernel, tm=tm, transpose_v=transpose_v),
        grid=grid, in_specs=in_specs, out_specs=out_specs, out_shape=out_shapes,
        scratch_shapes=[pltpu.VMEM((tm + SUBLANES, D_MODEL), F32)],
        compiler_params=pltpu.CompilerParams(dimension_semantics=("arbitrary", "arbitrary"),
                                             vmem_limit_bytes=VMEM_LIMIT),
        name="inproj_conv",
    )(x, prev8, w['a'], w['q'], w['gb'], w['kk'], w['vv'], w['qi'], w['ki'], conv_w)


def _bias_tiles_kernel(tbl_ref, out_ref, *, nq):
    d = pl.program_id(0) - 2
    krow = lax.broadcasted_iota(I32, (LANES, nq), 0)
    qcol = lax.broadcasted_iota(I32, (LANES, nq), 1)
    rel = d * LANES + krow - qcol
    n = jnp.abs(rel)
    nb = NUM_BUCKETS // 2
    max_exact = nb // 2
    n2 = n * n
    large = jnp.full((LANES, nq), max_exact, I32)
    for j in range(1, nb - max_exact):
        large = large + (n2 >= (max_exact * max_exact) * (1 << j)).astype(I32)
    bucket = jnp.where(n < max_exact, n, large) + jnp.where(rel > 0, nb, 0)
    far = pl.program_id(0) == 0
    for h in range(N_HEADS):
        base = tbl_ref[nb - 1, h]
        acc = jnp.zeros((LANES, nq), F32)
        for bk in range(NUM_BUCKETS):
            acc = jnp.where(bucket == bk, (tbl_ref[bk, h] - base) * LOG2E, acc)
        out_ref[0, h] = jnp.where(far, 0.0, acc)


def _bias_tiles(rel_bias, nq):
    assert MAX_DISTANCE == 128 and NUM_BUCKETS == 32
    return pl.pallas_call(
        functools.partial(_bias_tiles_kernel, nq=nq),
        grid=(N_BIAS_TILES,),
        in_specs=[pl.BlockSpec(memory_space=pltpu.SMEM)],
        out_specs=pl.BlockSpec((1, N_HEADS, LANES, nq), lambda i: (i, 0, 0, 0)),
        out_shape=jax.ShapeDtypeStruct((N_BIAS_TILES, N_HEADS, LANES, nq), F32),
        name="bias_tiles",
    )(rel_bias)


def _attn_kernel(q_ref, qi_ref, kiwi_ref, conv_ref, sgb_ref, x_ref, kk_ref, vvt_ref, kiki_ref, bt_ref,
                 wout_ref, g_ref, b_ref, out_ref,
                 key_ref, mask_ref, qis_ref, tri_ref, qs_ref, acc_ref, merged_ref, hi_ref, dig_ref,
                 *, alpha, nq, q_off, kv_len, topk):
    q0 = q_off + pl.program_id(1) * nq
    nkb = (q0 + nq + KB - 1) // KB
    jd = q0 // LANES
    lane = lax.broadcasted_iota(I32, (nq, LANES), 1)
    low = lane < HEAD_DIM
    krow = lax.broadcasted_iota(I32, (LANES, nq), 0)
    qchunk = lax.shift_right_arithmetic(q0 + lax.broadcasted_iota(I32, (LANES, nq), 1), CHUNK_SHIFT)

    wit = kiwi_ref[0].T * ((IDX_DIM ** -0.5) * (IDX_HEADS ** -0.5))
    wis = [wit[IDX_DIM + h:IDX_DIM + h + 1, :] for h in range(IDX_HEADS)]
    for h in range(IDX_HEADS):
        pair = qi_ref[0, :, (h // 2) * LANES:(h // 2 + 1) * LANES].astype(F32)
        keep = low if h % 2 == 0 else jnp.logical_not(low)
        qis_ref[h] = jnp.where(keep, pair, 0.0).astype(BF16)

    def score_block(jb, carry):
        ks = pl.multiple_of(jb * KB, KB)
        ki = kiki_ref[0, pl.ds(ks, KB), :]
        sc = [jnp.zeros((LANES, nq), F32) for _ in range(TILES)]
        for h in range(IDX_HEADS):
            s = _dot_nt(ki, qis_ref[h])
            for c in range(TILES):
                sc[c] = sc[c] + jnp.maximum(s[c * LANES:(c + 1) * LANES], 0.0) * wis[h]
        for c in range(TILES):
            bits = lax.bitcast_convert_type(sc[c], I32)
            key = jnp.where(bits < 0, bits ^ 0x7FFFFFFF, bits)
            kpos = ks + c * LANES + krow
            adm = (lax.shift_right_arithmetic(kpos, CHUNK_SHIFT) <= qchunk) & (kpos < kv_len)
            key_ref[pl.ds(ks + c * LANES, LANES), :] = jnp.where(adm, key, INT_MIN)
        return carry

    lax.fori_loop(0, nkb, score_block, 0)

    def count(pred):
        def body(jb, acc):
            ks = pl.multiple_of(jb * KB, KB)
            parts = []
            for c in range(TILES):
                blk = key_ref[pl.ds(ks + c * LANES, LANES), :]
                parts.append(_fold8(pred(blk, ks + c * LANES).astype(I32)))
            return acc + _tree(jnp.add, parts)
        acc = lax.fori_loop(0, nkb, body, jnp.zeros((SUBLANES, nq), I32))
        return jnp.sum(acc.astype(F32), axis=0, keepdims=True).astype(I32)

    pack = 2 * SUBLANES

    def count16(ref, cand):
        def body(jb, acc):
            ks = pl.multiple_of(jb * KB, KB)
            parts = []
            for c in range(TILES):
                hit = jnp.where(ref[pl.ds(ks + c * LANES, LANES), :] >= cand, jnp.ones((), I16), jnp.zeros((), I16))
                parts.append(_tree(jnp.add, [hit[i:i + pack] for i in range(0, LANES, pack)]))
            return acc + _tree(jnp.add, parts)
        acc = lax.fori_loop(0, nkb, body, jnp.zeros((pack, nq), I16))
        return jnp.sum(acc.astype(I32).astype(F32), axis=0, keepdims=True).astype(I32)

    def digit_search(ref, nbits, start, need):
        def step(i, t):
            cand = t + lax.shift_left(jnp.int32(1), nbits - 1 - i)
            return jnp.where(count16(ref, cand.astype(I16)) >= need, cand, t)
        return lax.fori_loop(0, nbits, step, start)

    def fill(ref, fn):
        def body(jb, carry):
            ks = pl.multiple_of(jb * KB, KB)
            for c in range(TILES):
                rows = pl.ds(ks + c * LANES, LANES)
                ref[rows, :] = fn(key_ref[rows, :], rows)
            return carry
        lax.fori_loop(0, nkb, body, 0)

    digit_min = -2 ** 15
    zero = jnp.zeros((1, nq), I32)
    fill(hi_ref, lambda k, rows: lax.shift_right_arithmetic(k, 16).astype(I16))
    top = digit_search(hi_ref, 16, jnp.full((1, nq), digit_min, I32), topk)
    found = top > digit_min
    above = jnp.where(top < 2 ** 15 - 1, count16(hi_ref, jnp.minimum(top + 1, 2 ** 15 - 1).astype(I16)), 0)
    need = topk - above
    top16 = top.astype(I16)
    fill(dig_ref, lambda k, rows: jnp.where(
        hi_ref[rows, :] == top16, (lax.shift_right_logical(k, 8) & 0xFF).astype(I16), jnp.full((), -1, I16)))
    mid = digit_search(dig_ref, 8, zero, need)
    need = need - count16(dig_ref, (mid + 1).astype(I16))
    mid16 = mid.astype(I16)
    fill(dig_ref, lambda k, rows: jnp.where(dig_ref[rows, :] == mid16, (k & 0xFF).astype(I16), jnp.full((), -1, I16)))
    bottom = digit_search(dig_ref, 8, zero, need)
    thr = jnp.where(found, lax.shift_left(top, 16) | lax.shift_left(mid, 8) | bottom, INT_MIN)

    want = (topk - count(lambda blk, base: blk > thr)).astype(F32)
    live = thr > INT_MIN
    tri_ref[...] = (lax.broadcasted_iota(I32, (LANES, LANES), 0) > lax.broadcasted_iota(I32, (LANES, LANES), 1)
                    ).astype(F32).astype(BF16)

    def mask_block(jb, seen):
        ks = pl.multiple_of(jb * KB, KB)
        for c in range(TILES):
            key = key_ref[pl.ds(ks + c * LANES, LANES), :]
            tie = ((key == thr) & live).astype(F32)
            before = _dot(tri_ref[...], tie.astype(BF16)) + seen
            keep = (key > thr) | ((tie > 0.0) & (before < want))
            mask_ref[pl.ds(ks + c * LANES, LANES), :] = jnp.where(keep, 0.0, NEG)
            seen = seen + jnp.sum(_fold8(tie), axis=0, keepdims=True)
        return seen

    lax.fori_loop(0, nkb, mask_block, jnp.zeros((1, nq), F32))

    def head_step(s, msks, biases, m):
        ss = [s[c * LANES:(c + 1) * LANES] + msks[c] for c in range(TILES)]
        if biases is not None:
            ss = [x + b for x, b in zip(ss, biases)]
        m_new = jnp.maximum(m, jnp.max(_tree(jnp.maximum, ss), axis=0, keepdims=True))
        return m_new, jnp.exp2(m - m_new), jnp.concatenate([jnp.exp2(x - m_new) for x in ss], axis=0).astype(BF16)

    for p0 in range(0, N_PAIRS, PAIRS_PER_LOOP):
        pairs = tuple(range(p0, p0 + PAIRS_PER_LOOP))
        for p in pairs:
            qp = q_ref[0, :, p * LANES:(p + 1) * LANES].astype(F32)
            qs_ref[2 * p] = jnp.where(low, qp, 0.0).astype(BF16)
            qs_ref[2 * p + 1] = jnp.where(low, 0.0, qp).astype(BF16)

        def attn_block(jb, carry, with_bias):
            ks = pl.multiple_of(jb * KB, KB)
            msks = [mask_ref[pl.ds(ks + c * LANES, LANES), :] for c in range(TILES)]
            tiles = [jnp.clip(jb * TILES + c - jd + 2, 0, N_BIAS_TILES - 1) for c in range(TILES)]

            def logits(head):
                g = head // GROUP
                kblk = kk_ref[0, pl.ds(ks, KB), g * LANES:(g + 1) * LANES]
                return _dot_nt(kblk, qs_ref[head])

            heads = [2 * p + r for p in pairs for r in range(2)]
            pending = [logits(h) for h in heads[:LOOKAHEAD]]
            vts = {}
            out = []
            for i, p in enumerate(pairs):
                g = (2 * p) // GROUP
                if g not in vts:
                    vts[g] = jnp.concatenate([vvt_ref[0, jb, g * HEAD_DIM:(g + 1) * HEAD_DIM, :],
                                              jnp.ones((ONES_ROWS, KB), BF16)], axis=0)
                stats = []
                for r, (m, l) in enumerate(carry[i]):
                    nxt = 2 * i + r + LOOKAHEAD
                    if nxt < len(heads):
                        pending.append(logits(heads[nxt]))
                    biases = [bt_ref[tiles[c], 2 * p + r] for c in range(TILES)] if with_bias else None
                    m, a, pr = head_step(pending.pop(0), msks, biases, m)
                    pv = _dot(vts[g], pr)
                    acc_ref[2 * p + r] = a * acc_ref[2 * p + r] + pv[:HEAD_DIM]
                    stats.append((m, a * l + pv[HEAD_DIM:HEAD_DIM + 1]))
                out.append(tuple(stats))
            return tuple(out)

        for p in pairs:
            acc_ref[2 * p] = jnp.zeros((HEAD_DIM, nq), F32)
            acc_ref[2 * p + 1] = jnp.zeros((HEAD_DIM, nq), F32)
        init = ((jnp.full((1, nq), NEG, F32), jnp.zeros((1, nq), F32)),) * 2
        nfar = jnp.clip((jd - 1) // TILES, 0, nkb)
        res = lax.fori_loop(0, nfar, functools.partial(attn_block, with_bias=False), (init,) * len(pairs))
        res = lax.fori_loop(nfar, nkb, functools.partial(attn_block, with_bias=True), res)
        for i, p in enumerate(pairs):
            (_, la), (_, lb) = res[i]
            o = jnp.concatenate([acc_ref[2 * p] / la, acc_ref[2 * p + 1] / lb], axis=0).T
            sl = slice(p * LANES, (p + 1) * LANES)
            merged_ref[:, sl] = (conv_ref[0, :, sl] + sgb_ref[0, :, sl] * o).astype(BF16)

    y = alpha * x_ref[0] + _dot(merged_ref[...], wout_ref[...])
    out_ref[0] = _layer_norm(y, g_ref[...], b_ref[...])


def _attention(q, qi, kiwi, conv, sgb, x, kk, vvt, kiki, bias_tiles, w_out, g, b_ln, *, alpha, nq, q_off, kv_len,
               topk):
    b, t, _ = q.shape
    lpad = kk.shape[1]
    assert q_off % LANES == 0 and nq % LANES == 0 and t % nq == 0
    assert lpad % KB == 0 and lpad >= q_off + t and vvt.shape == (b, lpad // KB, N_KV_HEADS * HEAD_DIM, KB)
    grid = (b, t // nq)
    row = lambda n: pl.BlockSpec((1, nq, n), lambda i, j: (i, j, 0))
    keys = lambda n: pl.BlockSpec((1, lpad, n), lambda i, j: (i, 0, 0), pipeline_mode=pl.Buffered(1))
    const = lambda a: pl.BlockSpec(a.shape, lambda i, j: (0,) * a.ndim, pipeline_mode=pl.Buffered(1))
    return pl.pallas_call(
        functools.partial(_attn_kernel, alpha=alpha, nq=nq, q_off=q_off, kv_len=kv_len, topk=topk),
        grid=grid,
        in_specs=[row(D_MODEL), row(IDX_HEADS * IDX_DIM), row(LANES), row(D_MODEL), row(D_MODEL), row(D_MODEL),
                  keys(4 * LANES),
                  pl.BlockSpec((1, lpad // KB, N_KV_HEADS * HEAD_DIM, KB), lambda i, j: (i, 0, 0, 0),
                               pipeline_mode=pl.Buffered(1)),
                  keys(LANES), const(bias_tiles), const(w_out), const(g), const(b_ln)],
        out_specs=row(D_MODEL),
        out_shape=jax.ShapeDtypeStruct((b, t, D_MODEL), F32),
        scratch_shapes=[pltpu.VMEM((lpad, nq), I32), pltpu.VMEM((lpad, nq), F32),
                        pltpu.VMEM((IDX_HEADS, nq, LANES), BF16), pltpu.VMEM((LANES, LANES), BF16),
                        pltpu.VMEM((N_HEADS, nq, LANES), BF16), pltpu.VMEM((N_HEADS, HEAD_DIM, nq), F32),
                        pltpu.VMEM((nq, D_MODEL), BF16), pltpu.VMEM((lpad, nq), I16), pltpu.VMEM((lpad, nq), I16)],
        compiler_params=pltpu.CompilerParams(dimension_semantics=("arbitrary", "arbitrary"),
                                             vmem_limit_bytes=VMEM_LIMIT),
        name="dsa_attention",
    )(q, qi, kiwi, conv, sgb, x, kk, vvt, kiki, bias_tiles, w_out, g, b_ln)


def _ffn_kernel(x_ref, wg_ref, wu_ref, wd_ref, g_ref, b_ref, out_ref, *, alpha):
    x = x_ref[...]
    xb = x.astype(BF16)
    hg = _dot(xb, wg_ref[...])
    h = (hg * jax.nn.sigmoid(hg)) * _dot(xb, wu_ref[...])
    f = _dot(h.astype(BF16), wd_ref[...])
    out_ref[...] = _layer_norm(alpha * x + f, g_ref[...], b_ref[...])


def _ffn_ln(x, wg, wu, wd, g, b, alpha, tm):
    n = x.shape[0]
    row = pl.BlockSpec((tm, D_MODEL), lambda i: (i, 0))
    vec = pl.BlockSpec((1, D_MODEL), lambda i: (0, 0))
    full = lambda a: pl.BlockSpec(a.shape, lambda i: (0, 0), pipeline_mode=pl.Buffered(1))
    return pl.pallas_call(
        functools.partial(_ffn_kernel, alpha=alpha),
        grid=(n // tm,),
        in_specs=[row, full(wg), full(wu), full(wd), vec, vec],
        out_specs=row,
        out_shape=jax.ShapeDtypeStruct((n, D_MODEL), F32),
        compiler_params=pltpu.CompilerParams(dimension_semantics=("arbitrary",), vmem_limit_bytes=VMEM_LIMIT),
        name="ffn_ln",
    )(x, wg, wu, wd, g, b)


def _moe_kernel(x_ref, rw_ref, rb_ref, wgu_ref, wd_ref, g_ref, b_ref, out_ref,
                rank_ref, rankt_ref, combt_ref, xb_ref, acc_ref, *, alpha, tm):
    e = pl.program_id(1)
    sub = min(MOE_SUB, tm)
    assert sub & (sub - 1) == 0 and tm % sub == 0
    ch = min(MOE_CHUNK, sub)
    lane = lax.broadcasted_iota(I32, (tm, LANES), 1)

    @pl.when(e == 0)
    def _():
        x = x_ref[...]
        x_hi = x.astype(BF16)
        x_lo = (x - x_hi.astype(F32)).astype(BF16)
        logits = (_dot(x_hi, rw_ref[0]) + _dot(x_hi, rw_ref[1]) + _dot(x_lo, rw_ref[0])) + rb_ref[...]
        logits = jnp.where(lane < N_EXPERTS, logits, -jnp.inf)
        lanef = lane.astype(F32)
        v1 = jnp.max(logits, axis=1, keepdims=True)
        i1 = jnp.min(jnp.where(logits == v1, lanef, float(LANES)), axis=1, keepdims=True)
        rest = jnp.where(lanef == i1, -jnp.inf, logits)
        v2 = jnp.max(rest, axis=1, keepdims=True)
        i2 = jnp.min(jnp.where(rest == v2, lanef, float(LANES)), axis=1, keepdims=True)
        e2 = jnp.exp(v2 - v1)
        den = 1.0 + e2
        comb = jnp.where(lanef == i1, 1.0 / den, 0.0) + jnp.where(lanef == i2, e2 / den, 0.0)
        routed = (lanef == i1) | (lanef == i2)
        tr = lax.broadcasted_iota(I32, (tm, tm), 0)
        tc = lax.broadcasted_iota(I32, (tm, tm), 1)
        earlier = (tr > tc) & (tc >= tr - (tr & (sub - 1)))
        rank = _dot(earlier.astype(F32).astype(BF16), routed.astype(F32).astype(BF16))
        rank = jnp.where(routed, rank, -1.0)
        rank_ref[...] = rank.astype(I32)
        rankt_ref[...] = rank.T.astype(I32)
        combt_ref[...] = comb.T
        xb_ref[...] = x_ref[...].astype(BF16)
        acc_ref[...] = jnp.zeros((tm, D_MODEL), F32)

    rcol = jnp.max(jnp.where(lane == e, rank_ref[...], -1).astype(F32), axis=1, keepdims=True).astype(I32)
    rrow = rankt_ref[pl.ds(e, 1), :]
    crow = combt_ref[pl.ds(e, 1), :]

    for s0 in range(0, tm, sub):
        rcol_s, rrow_s, crow_s = rcol[s0:s0 + sub], rrow[:, s0:s0 + sub], crow[:, s0:s0 + sub]
        n_routed = jnp.max(rcol_s.astype(F32)).astype(I32) + 1

        def chunk(c, carry):
            base = c * ch
            hit = rrow_s == lax.broadcasted_iota(I32, (ch, sub), 0) + base
            xg = _dot(jnp.where(hit, 1.0, 0.0).astype(BF16), xb_ref[s0:s0 + sub, :]).astype(BF16)
            hgu = _dot(xg, wgu_ref[...])
            hg = hgu[:, :D_FF_EXPERT]
            h = (hg * jax.nn.sigmoid(hg)) * hgu[:, D_FF_EXPERT:]
            f = _dot(h.astype(BF16), wd_ref[...])
            gate = jnp.sum(jnp.where(hit, crow_s, 0.0), axis=1, keepdims=True)
            hit_t = rcol_s == lax.broadcasted_iota(I32, (sub, ch), 1) + base
            acc_ref[s0:s0 + sub, :] += _dot(jnp.where(hit_t, 1.0, 0.0).astype(BF16), (f * gate).astype(BF16))
            return carry

        lax.fori_loop(0, (n_routed + ch - 1) // ch, chunk, 0)

    @pl.when(e == N_EXPERTS - 1)
    def _():
        out_ref[...] = _layer_norm(alpha * x_ref[...] + acc_ref[...], g_ref[...], b_ref[...])


def _moe_ln(x, rw, rb, wgu, wd, g, b, alpha, tm):
    n = x.shape[0]
    row = pl.BlockSpec((tm, D_MODEL), lambda i, e: (i, 0))
    vec = pl.BlockSpec((1, D_MODEL), lambda i, e: (0, 0))
    return pl.pallas_call(
        functools.partial(_moe_kernel, alpha=alpha, tm=tm),
        grid=(n // tm, N_EXPERTS),
        in_specs=[row, pl.BlockSpec((2, D_MODEL, LANES), lambda i, e: (0, 0, 0)),
                  pl.BlockSpec((1, LANES), lambda i, e: (0, 0)),
                  pl.BlockSpec((None, D_MODEL, 2 * D_FF_EXPERT), lambda i, e: (e, 0, 0)),
                  pl.BlockSpec((None, D_FF_EXPERT, D_MODEL), lambda i, e: (e, 0, 0)),
                  vec, vec],
        out_specs=row,
        out_shape=jax.ShapeDtypeStruct((n, D_MODEL), F32),
        scratch_shapes=[pltpu.VMEM((tm, LANES), I32), pltpu.VMEM((LANES, tm), I32), pltpu.VMEM((LANES, tm), F32),
                        pltpu.VMEM((tm, D_MODEL), BF16), pltpu.VMEM((tm, D_MODEL), F32)],
        compiler_params=pltpu.CompilerParams(dimension_semantics=("arbitrary", "arbitrary"),
                                             vmem_limit_bytes=VMEM_LIMIT),
        name="moe_ln",
    )(x, rw, rb, wgu, wd, g, b)


def _split_w_in(w):
    sizes = (D_MODEL, D_MODEL, D_MODEL, N_HEADS * HEAD_DIM, N_KV_HEADS * HEAD_DIM, N_KV_HEADS * HEAD_DIM,
             IDX_HEADS * IDX_DIM, IDX_DIM, IDX_HEADS, D_MODEL, D_MODEL)
    parts, start = [], 0
    for n in sizes:
        parts.append(w[:, start:start + n])
        start += n
    bg, cg, xin, q, k, v, qi, ki, wi, ga, gb = parts

    def dup(a):
        a = a.reshape(a.shape[0], -1, HEAD_DIM)
        return jnp.concatenate([a, a], axis=-1).reshape(a.shape[0], -1)

    kiwi = jnp.concatenate([ki, wi, jnp.zeros((w.shape[0], LANES - IDX_DIM - IDX_HEADS), w.dtype)], axis=1)
    return {
        'a': jnp.concatenate([bg, cg, xin, ga], axis=1).astype(BF16),
        'q': q.astype(BF16), 'gb': gb.astype(BF16), 'kk': dup(k).astype(BF16), 'vv': v.astype(BF16),
        'qi': qi.astype(BF16), 'ki': jnp.concatenate([ki, ki, kiwi], axis=1).astype(BF16),
    }


def _dup_cache(a):
    a = a.astype(BF16)
    return jnp.concatenate([a, a], axis=-1).reshape(a.shape[0], a.shape[1], -1)


def _pad_rows(a, n):
    return jnp.pad(a, ((0, 0), (0, n - a.shape[1]), (0, 0)))


def _row_tile(n, cap):
    tm = min(n, cap)
    assert n % tm == 0
    return tm


def kernel(x_prompt, x_sample, cache_k, cache_v, cache_kidx, state_conv, w_in, conv_w, w_out, rel_bias,
           ln1_g, ln1_b, ln2_g, ln2_b, ffn_w_gate, ffn_w_up, ffn_w_down, router_w, router_b,
           moe_w_gate, moe_w_up, moe_w_down):
    depth = w_in.shape[0]
    alpha = (2 * depth) ** 0.25
    bp, tp, _ = x_prompt.shape
    bs, ts, _ = x_sample.shape
    past = cache_k.shape[2]
    nq_p = 256 if tp % 256 == 0 else LANES
    nq_s = LANES
    assert tp % KB == 0 and past % LANES == 0 and ts <= nq_s and ts % SUBLANES == 0
    topk_p = min(TOPK_MAX, tp // 4)
    topk_s = min(TOPK_MAX, (past + ts) // 4)
    lpad_s = -(-(past + nq_s) // KB) * KB

    bt_p = _bias_tiles(rel_bias, nq_p)
    bt_s = _bias_tiles(rel_bias, nq_s)
    vec = lambda a: a.reshape(1, -1)

    def layer(l, x, prev8, past_kv):
        b, t, _ = x.shape
        w = _split_w_in(w_in[l])
        prompt = past_kv is None
        (conv, q, sgb, kkf, kkb, vvf, vvb, qi, kiki, kiwi, cstate) = _inproj(
            x, prev8, w, conv_w[l], _row_tile(t, KB), transpose_v=prompt)
        proj = (w_out[l].astype(BF16), vec(ln1_g[l]), vec(ln1_b[l]))
        if prompt:
            x1 = _attention(q, qi, kiwi, conv, sgb, x, kkb, vvb, kiki, bt_p, *proj,
                            alpha=alpha, nq=nq_p, q_off=0, kv_len=t, topk=topk_p)
        else:
            ck, cv, cki = past_kv
            kk_all = _pad_rows(jnp.concatenate([_dup_cache(ck), kkb], axis=1), lpad_s)
            cv = cv.astype(BF16).reshape(b, cv.shape[1], N_KV_HEADS * HEAD_DIM)
            vv_all = _pad_rows(jnp.concatenate([cv, vvb], axis=1), lpad_s)
            vvt_all = jnp.swapaxes(vv_all.reshape(b, lpad_s // KB, KB, N_KV_HEADS * HEAD_DIM), 2, 3)
            cki = cki.astype(BF16)
            kiki_all = _pad_rows(jnp.concatenate([jnp.concatenate([cki, cki], axis=-1), kiki], axis=1), lpad_s)
            pad = lambda a: _pad_rows(a, nq_s)
            x1 = _attention(pad(q), pad(qi), pad(kiwi), pad(conv), pad(sgb), pad(x), kk_all, vvt_all, kiki_all, bt_s,
                            *proj, alpha=alpha, nq=nq_s, q_off=past, kv_len=past + t, topk=topk_s)[:, :t]
        n = b * t
        x1 = x1.reshape(n, D_MODEL)
        j = l // 2
        if l % 2 == 0:
            x2 = _ffn_ln(x1, ffn_w_gate[j].astype(BF16), ffn_w_up[j].astype(BF16), ffn_w_down[j].astype(BF16),
                         vec(ln2_g[l]), vec(ln2_b[l]), alpha, _row_tile(n, 512))
        else:
            rw = jnp.pad(router_w[j], ((0, 0), (0, LANES - N_EXPERTS)))
            rw_hi = rw.astype(BF16)
            rw = jnp.stack([rw_hi, (rw - rw_hi.astype(F32)).astype(BF16)])
            rb = jnp.pad(router_b[j], (0, LANES - N_EXPERTS)).reshape(1, LANES)
            wgu = jnp.concatenate([moe_w_gate[j], moe_w_up[j]], axis=-1).astype(BF16)
            x2 = _moe_ln(x1, rw, rb, wgu, moe_w_down[j].astype(BF16), vec(ln2_g[l]), vec(ln2_b[l]), alpha,
                         _row_tile(n, 2 * MOE_SUB))
        heads = lambda a: a.reshape(b, t, N_KV_HEADS, HEAD_DIM)
        return (x2.reshape(b, t, D_MODEL), heads(kkf), heads(vvf), kiwi[..., :IDX_DIM],
                cstate[:, SUBLANES - (CONV_W - 1):, :])

    xp, xs = x_prompt, x_sample
    outs_p, outs_s = [], []
    zero_prev = jnp.zeros((bp, SUBLANES, D_MODEL), F32)
    for l in range(depth):
        xp, *rest = layer(l, xp, zero_prev, None)
        outs_p.append(rest)
        prev8 = jnp.pad(state_conv[l], ((0, 0), (SUBLANES - (CONV_W - 1), 0), (0, 0)))
        xs, *rest = layer(l, xs, prev8, (cache_k[l], cache_v[l], cache_kidx[l]))
        outs_s.append(rest)
    stack = lambda outs, i: jnp.stack([o[i] for o in outs])
    return (xp, xs,
            stack(outs_p, 0), stack(outs_p, 1), stack(outs_p, 2), stack(outs_p, 3),
            stack(outs_s, 0), stack(outs_s, 1), stack(outs_s, 2), stack(outs_s, 3))
```

```python
import functools

import jax
import jax.numpy as jnp
from jax import lax
from jax.experimental import pallas as pl
from jax.experimental.pallas import tpu as pltpu

D_MODEL = 1024
CHUNK = 64
CHUNK_SHIFT = 6
CONV_W = 3
N_HEADS = 16
HEAD_DIM = 64
N_KV_HEADS = 4
GROUP = N_HEADS // N_KV_HEADS
IDX_HEADS = 8
IDX_DIM = 64
TOPK_MAX = 256
NUM_BUCKETS = 32
MAX_DISTANCE = 128
N_EXPERTS = 8
D_FF_EXPERT = 1408
LN_EPS = 1e-5

LANES = 128
SUBLANES = 8
N_PAIRS = N_HEADS // 2
KB = 512
TILES = KB // LANES
N_BIAS_TILES = 4
PAIRS_PER_LOOP = 8
LOOKAHEAD = 3
ONES_ROWS = 16
MOE_SUB = 512
MOE_CHUNK = 160
NEG = -1e30
LOG2E = 1.4426950408889634
INT_MIN = -2 ** 31
VMEM_LIMIT = 56 * 1024 * 1024

F32 = jnp.float32
BF16 = jnp.bfloat16
I32 = jnp.int32
I16 = jnp.int16

assert CHUNK == 1 << CHUNK_SHIFT and 2 * HEAD_DIM == LANES and 2 * IDX_DIM == LANES


def _dot(a, b):
    return jnp.dot(a, b, preferred_element_type=F32)


def _dot_nt(a, b):
    return lax.dot_general(a, b, (((1,), (1,)), ((), ())), preferred_element_type=F32)


def _layer_norm(y, g, b):
    mu = jnp.mean(y, axis=-1, keepdims=True)
    d = y - mu
    var = jnp.mean(d * d, axis=-1, keepdims=True)
    return d * lax.rsqrt(var + LN_EPS) * g + b


def _tree(op, xs):
    xs = list(xs)
    while len(xs) > 1:
        xs = [op(xs[i], xs[i + 1]) for i in range(0, len(xs) - 1, 2)] + ([xs[-1]] if len(xs) % 2 else [])
    return xs[0]


def _fold8(x):
    return _tree(jnp.add, [x[i:i + SUBLANES] for i in range(0, x.shape[0], SUBLANES)])


def _inproj_kernel(x_ref, prev_ref, wa_ref, wq_ref, wgb_ref, wkk_ref, wvv_ref, wqi_ref, wki_ref, cw_ref,
                   conv_ref, q_ref, sgb_ref, kkf_ref, kkb_ref, vvf_ref, vvb_ref, qi_ref, kiki_ref, kiwi_ref,
                   cstate_ref, uext_ref, *, tm, transpose_v):
    @pl.when(pl.program_id(1) == 0)
    def _():
        uext_ref[0:SUBLANES, :] = prev_ref[0]

    xb = x_ref[0].astype(BF16)
    cc = 256
    for c in range(0, D_MODEL, cc):
        bg = _dot(xb, wa_ref[:, c:c + cc])
        cg = _dot(xb, wa_ref[:, D_MODEL + c:D_MODEL + c + cc])
        xin = _dot(xb, wa_ref[:, 2 * D_MODEL + c:2 * D_MODEL + c + cc])
        ga = _dot(xb, wa_ref[:, 3 * D_MODEL + c:3 * D_MODEL + c + cc])
        u = cg * xin
        uext_ref[SUBLANES:SUBLANES + tm, c:c + cc] = u
        um1 = uext_ref[SUBLANES - 1:SUBLANES - 1 + tm, c:c + cc]
        um2 = uext_ref[SUBLANES - 2:SUBLANES - 2 + tm, c:c + cc]
        y = um2 * cw_ref[0:1, c:c + cc] + um1 * cw_ref[1:2, c:c + cc] + u * cw_ref[2:3, c:c + cc]
        conv_ref[0, :, c:c + cc] = jax.nn.sigmoid(ga) * (bg * y)
    tail = uext_ref[tm:tm + SUBLANES, :]
    cstate_ref[0] = tail
    uext_ref[0:SUBLANES, :] = tail

    q_ref[0] = (_dot(xb, wq_ref[...]) * (HEAD_DIM ** -0.5 * LOG2E)).astype(BF16)
    sgb_ref[0] = jax.nn.sigmoid(_dot(xb, wgb_ref[...]))
    def undup(a):
        low = lax.broadcasted_iota(I32, (tm, LANES), 1) < HEAD_DIM
        return jnp.concatenate(
            [jnp.where(low, a[:, 2 * j * LANES:(2 * j + 1) * LANES], a[:, (2 * j + 1) * LANES:(2 * j + 2) * LANES])
             for j in range(N_KV_HEADS // 2)], axis=1)

    kk = _dot(xb, wkk_ref[...])
    kkf_ref[0] = undup(kk)
    kkb_ref[0] = kk.astype(BF16)
    vv = _dot(xb, wvv_ref[...])
    vvf_ref[0] = vv
    if transpose_v:
        vvb_ref[0, 0] = vv.T.astype(BF16)
    else:
        vvb_ref[0] = vv.astype(BF16)
    qi_ref[0] = _dot(xb, wqi_ref[...]).astype(BF16)
    kw = _dot(xb, wki_ref[...])
    kiki_ref[0] = kw[:, 0:LANES].astype(BF16)
    kiwi_ref[0] = kw[:, LANES:2 * LANES]


def _inproj(x, prev8, w, conv_w, tm, transpose_v):
    b, t, _ = x.shape
    grid = (b, t // tm)
    row = lambda n: pl.BlockSpec((1, tm, n), lambda i, j: (i, j, 0))
    full = lambda a: pl.BlockSpec(a.shape, lambda i, j: (0,) * a.ndim, pipeline_mode=pl.Buffered(1))
    if transpose_v:
        assert tm == KB
        vvb_shape = jax.ShapeDtypeStruct((b, t // KB, N_KV_HEADS * HEAD_DIM, KB), BF16)
        vvb_spec = pl.BlockSpec((1, 1, N_KV_HEADS * HEAD_DIM, KB), lambda i, j: (i, j, 0, 0))
    else:
        vvb_shape = jax.ShapeDtypeStruct((b, t, N_KV_HEADS * HEAD_DIM), BF16)
        vvb_spec = row(N_KV_HEADS * HEAD_DIM)
    out_shapes = (
        jax.ShapeDtypeStruct((b, t, D_MODEL), F32),
        jax.ShapeDtypeStruct((b, t, D_MODEL), BF16),
        jax.ShapeDtypeStruct((b, t, D_MODEL), F32),
        jax.ShapeDtypeStruct((b, t, N_KV_HEADS * HEAD_DIM), F32),
        jax.ShapeDtypeStruct((b, t, 4 * LANES), BF16),
        jax.ShapeDtypeStruct((b, t, N_KV_HEADS * HEAD_DIM), F32),
        vvb_shape,
        jax.ShapeDtypeStruct((b, t, IDX_HEADS * IDX_DIM), BF16),
        jax.ShapeDtypeStruct((b, t, LANES), BF16),
        jax.ShapeDtypeStruct((b, t, LANES), F32),
        jax.ShapeDtypeStruct((b, SUBLANES, D_MODEL), F32),
    )
    out_specs = (row(D_MODEL), row(D_MODEL), row(D_MODEL), row(N_KV_HEADS * HEAD_DIM), row(4 * LANES),
                 row(N_KV_HEADS * HEAD_DIM), vvb_spec, row(IDX_HEADS * IDX_DIM), row(LANES), row(LANES),
                 pl.BlockSpec((1, SUBLANES, D_MODEL), lambda i, j: (i, 0, 0)))
    in_specs = [row(D_MODEL), pl.BlockSpec((1, SUBLANES, D_MODEL), lambda i, j: (i, 0, 0)),
                full(w['a']), full(w['q']), full(w['gb']), full(w['kk']), full(w['vv']), full(w['qi']),
                full(w['ki']), full(conv_w)]
    return pl.pallas_call(
        functools.partial(_inproj_kernel, tm=tm, transpose_v=transpose_v),
        grid=grid, in_specs=in_specs, out_specs=out_specs, out_shape=out_shapes,
        scratch_shapes=[pltpu.VMEM((tm + SUBLANES, D_MODEL), F32)],
        compiler_params=pltpu.CompilerParams(dimension_semantics=("arbitrary", "arbitrary"),
                                             vmem_limit_bytes=VMEM_LIMIT),
        name="inproj_conv",
    )(x, prev8, w['a'], w['q'], w['gb'], w['kk'], w['vv'], w['qi'], w['ki'], conv_w)


def _bias_tiles_kernel(tbl_ref, out_ref, *, nq):
    d = pl.program_id(0) - 2
    krow = lax.broadcasted_iota(I32, (LANES, nq), 0)
    qcol = lax.broadcasted_iota(I32, (LANES, nq), 1)
    rel = d * LANES + krow - qcol
    n = jnp.abs(rel)
    nb = NUM_BUCKETS // 2
    max_exact = nb // 2
    n2 = n * n
    large = jnp.full((LANES, nq), max_exact, I32)
    for j in range(1, nb - max_exact):
        large = large + (n2 >= (max_exact * max_exact) * (1 << j)).astype(I32)
    bucket = jnp.where(n < max_exact, n, large) + jnp.where(rel > 0, nb, 0)
    far = pl.program_id(0) == 0
    for h in range(N_HEADS):
        base = tbl_ref[nb - 1, h]
        acc = jnp.zeros((LANES, nq), F32)
        for bk in range(NUM_BUCKETS):
            acc = jnp.where(bucket == bk, (tbl_ref[bk, h] - base) * LOG2E, acc)
        out_ref[0, h] = jnp.where(far, 0.0, acc)


def _bias_tiles(rel_bias, nq):
    assert MAX_DISTANCE == 128 and NUM_BUCKETS == 32
    return pl.pallas_call(
        functools.partial(_bias_tiles_kernel, nq=nq),
        grid=(N_BIAS_TILES,),
        in_specs=[pl.BlockSpec(memory_space=pltpu.SMEM)],
        out_specs=pl.BlockSpec((1, N_HEADS, LANES, nq), lambda i: (i, 0, 0, 0)),
        out_shape=jax.ShapeDtypeStruct((N_BIAS_TILES, N_HEADS, LANES, nq), F32),
        name="bias_tiles",
    )(rel_bias)


def _attn_kernel(q_ref, qi_ref, kiwi_ref, conv_ref, sgb_ref, x_ref, kk_ref, vvt_ref, kiki_ref, bt_ref,
                 wout_ref, g_ref, b_ref, out_ref,
                 key_ref, mask_ref, qis_ref, tri_ref, qs_ref, acc_ref, merged_ref, hi_ref, dig_ref,
                 *, alpha, nq, q_off, kv_len, topk):
    q0 = q_off + pl.program_id(1) * nq
    nkb = (q0 + nq + KB - 1) // KB
    jd = q0 // LANES
    lane = lax.broadcasted_iota(I32, (nq, LANES), 1)
    low = lane < HEAD_DIM
    krow = lax.broadcasted_iota(I32, (LANES, nq), 0)
    qchunk = lax.shift_right_arithmetic(q0 + lax.broadcasted_iota(I32, (LANES, nq), 1), CHUNK_SHIFT)

    wit = kiwi_ref[0].T * ((IDX_DIM ** -0.5) * (IDX_HEADS ** -0.5))
    wis = [wit[IDX_DIM + h:IDX_DIM + h + 1, :] for h in range(IDX_HEADS)]
    for h in range(IDX_HEADS):
        pair = qi_ref[0, :, (h // 2) * LANES:(h // 2 + 1) * LANES].astype(F32)
        keep = low if h % 2 == 0 else jnp.logical_not(low)
        qis_ref[h] = jnp.where(keep, pair, 0.0).astype(BF16)

    def score_block(jb, carry):
        ks = pl.multiple_of(jb * KB, KB)
        ki = kiki_ref[0, pl.ds(ks, KB), :]
        sc = [jnp.zeros((LANES, nq), F32) for _ in range(TILES)]
        for h in range(IDX_HEADS):
            s = _dot_nt(ki, qis_ref[h])
            for c in range(TILES):
                sc[c] = sc[c] + jnp.maximum(s[c * LANES:(c + 1) * LANES], 0.0) * wis[h]
        for c in range(TILES):
            bits = lax.bitcast_convert_type(sc[c], I32)
            key = jnp.where(bits < 0, bits ^ 0x7FFFFFFF, bits)
            kpos = ks + c * LANES + krow
            adm = (lax.shift_right_arithmetic(kpos, CHUNK_SHIFT) <= qchunk) & (kpos < kv_len)
            key_ref[pl.ds(ks + c * LANES, LANES), :] = jnp.where(adm, key, INT_MIN)
        return carry

    lax.fori_loop(0, nkb, score_block, 0)

    def count(pred):
        def body(jb, acc):
            ks = pl.multiple_of(jb * KB, KB)
            parts = []
            for c in range(TILES):
                blk = key_ref[pl.ds(ks + c * LANES, LANES), :]
                parts.append(_fold8(pred(blk, ks + c * LANES).astype(I32)))
            return acc + _tree(jnp.add, parts)
        acc = lax.fori_loop(0, nkb, body, jnp.zeros((SUBLANES, nq), I32))
        return jnp.sum(acc.astype(F32), axis=0, keepdims=True).astype(I32)

    pack = 2 * SUBLANES

    def count16(ref, cand):
        def body(jb, acc):
            ks = pl.multiple_of(jb * KB, KB)
            parts = []
            for c in range(TILES):
                hit = jnp.where(ref[pl.ds(ks + c * LANES, LANES), :] >= cand, jnp.ones((), I16), jnp.zeros((), I16))
                parts.append(_tree(jnp.add, [hit[i:i + pack] for i in range(0, LANES, pack)]))
            return acc + _tree(jnp.add, parts)
        acc = lax.fori_loop(0, nkb, body, jnp.zeros((pack, nq), I16))
        return jnp.sum(acc.astype(I32).astype(F32), axis=0, keepdims=True).astype(I32)

    def digit_search(ref, nbits, start, need):
        def step(i, t):
            cand = t + lax.shift_left(jnp.int32(1), nbits - 1 - i)
            return jnp.where(count16(ref, cand.astype(I16)) >= need, cand, t)
        return lax.fori_loop(0, nbits, step, start)

    def fill(ref, fn):
        def body(jb, carry):
            ks = pl.multiple_of(jb * KB, KB)
            for c in range(TILES):
                rows = pl.ds(ks + c * LANES, LANES)
                ref[rows, :] = fn(key_ref[rows, :], rows)
            return carry
        lax.fori_loop(0, nkb, body, 0)

    digit_min = -2 ** 15
    zero = jnp.zeros((1, nq), I32)
    fill(hi_ref, lambda k, rows: lax.shift_right_arithmetic(k, 16).astype(I16))
    top = digit_search(hi_ref, 16, jnp.full((1, nq), digit_min, I32), topk)
    found = top > digit_min
    above = jnp.where(top < 2 ** 15 - 1, count16(hi_ref, jnp.minimum(top + 1, 2 ** 15 - 1).astype(I16)), 0)
    need = topk - above
    top16 = top.astype(I16)
    fill(dig_ref, lambda k, rows: jnp.where(
        hi_ref[rows, :] == top16, (lax.shift_right_logical(k, 8) & 0xFF).astype(I16), jnp.full((), -1, I16)))
    mid = digit_search(dig_ref, 8, zero, need)
    need = need - count16(dig_ref, (mid + 1).astype(I16))
    mid16 = mid.astype(I16)
    fill(dig_ref, lambda k, rows: jnp.where(dig_ref[rows, :] == mid16, (k & 0xFF).astype(I16), jnp.full((), -1, I16)))
    bottom = digit_search(dig_ref, 8, zero, need)
    thr = jnp.where(found, lax.shift_left(top, 16) | lax.shift_left(mid, 8) | bottom, INT_MIN)

    want = (topk - count(lambda blk, base: blk > thr)).astype(F32)
    live = thr > INT_MIN
    tri_ref[...] = (lax.broadcasted_iota(I32, (LANES, LANES), 0) > lax.broadcasted_iota(I32, (LANES, LANES), 1)
                    ).astype(F32).astype(BF16)

    def mask_block(jb, seen):
        ks = pl.multiple_of(jb * KB, KB)
        for c in range(TILES):
            key = key_ref[pl.ds(ks + c * LANES, LANES), :]
            tie = ((key == thr) & live).astype(F32)
            before = _dot(tri_ref[...], tie.astype(BF16)) + seen
            keep = (key > thr) | ((tie > 0.0) & (before < want))
            mask_ref[pl.ds(ks + c * LANES, LANES), :] = jnp.where(keep, 0.0, NEG)
            seen = seen + jnp.sum(_fold8(tie), axis=0, keepdims=True)
        return seen

    lax.fori_loop(0, nkb, mask_block, jnp.zeros((1, nq), F32))

    def head_step(s, msks, biases, m):
        m_halves, p_halves = [], []
        for h0 in range(0, nq, LANES):
            qs_ = slice(h0, h0 + LANES)
            ss = [s[c * LANES:(c + 1) * LANES, qs_] + msks[c][:, qs_] for c in range(TILES)]
            if biases is not None:
                ss = [x + b[:, qs_] for x, b in zip(ss, biases)]
            mh = jnp.maximum(m[:, qs_], jnp.max(_tree(jnp.maximum, ss), axis=0, keepdims=True))
            m_halves.append(mh)
            p_halves.append(jnp.concatenate([jnp.exp2(x - mh) for x in ss], axis=0).astype(BF16))
        m_new = jnp.concatenate(m_halves, axis=1)
        return m_new, jnp.exp2(m - m_new), jnp.concatenate(p_halves, axis=1)

    for p0 in range(0, N_PAIRS, PAIRS_PER_LOOP):
        pairs = tuple(range(p0, p0 + PAIRS_PER_LOOP))
        for p in pairs:
            qp = q_ref[0, :, p * LANES:(p + 1) * LANES].astype(F32)
            qs_ref[2 * p] = jnp.where(low, qp, 0.0).astype(BF16)
            qs_ref[2 * p + 1] = jnp.where(low, 0.0, qp).astype(BF16)

        def attn_block(jb, carry, with_bias):
            ks = pl.multiple_of(jb * KB, KB)
            msks = [mask_ref[pl.ds(ks + c * LANES, LANES), :] for c in range(TILES)]
            tiles = [jnp.clip(jb * TILES + c - jd + 2, 0, N_BIAS_TILES - 1) for c in range(TILES)]

            def logits(head):
                g = head // GROUP
                kblk = kk_ref[0, pl.ds(ks, KB), g * LANES:(g + 1) * LANES]
                return _dot_nt(kblk, qs_ref[head])

            heads = [2 * p + r for p in pairs for r in range(2)]
            pending = [logits(h) for h in heads[:LOOKAHEAD]]
            vts = {}
            out = []
            for i, p in enumerate(pairs):
                g = (2 * p) // GROUP
                if g not in vts:
                    vts[g] = jnp.concatenate([vvt_ref[0, jb, g * HEAD_DIM:(g + 1) * HEAD_DIM, :],
                                              jnp.ones((ONES_ROWS, KB), BF16)], axis=0)
                stats = []
                for r, (m, l) in enumerate(carry[i]):
                    nxt = 2 * i + r + LOOKAHEAD
                    if nxt < len(heads):
                        pending.append(logits(heads[nxt]))
                    biases = [bt_ref[tiles[c], 2 * p + r] for c in range(TILES)] if with_bias else None
                    m, a, pr = head_step(pending.pop(0), msks, biases, m)
                    pv = _dot(vts[g], pr)
                    acc_ref[2 * p + r] = a * acc_ref[2 * p + r] + pv[:HEAD_DIM]
                    stats.append((m, a * l + pv[HEAD_DIM:HEAD_DIM + 1]))
                out.append(tuple(stats))
            return tuple(out)

        for p in pairs:
            acc_ref[2 * p] = jnp.zeros((HEAD_DIM, nq), F32)
            acc_ref[2 * p + 1] = jnp.zeros((HEAD_DIM, nq), F32)
        init = ((jnp.full((1, nq), NEG, F32), jnp.zeros((1, nq), F32)),) * 2
        nfar = jnp.clip((jd - 1) // TILES, 0, nkb)
        res = lax.fori_loop(0, nfar, functools.partial(attn_block, with_bias=False), (init,) * len(pairs))
        res = lax.fori_loop(nfar, nkb, functools.partial(attn_block, with_bias=True), res)
        for i, p in enumerate(pairs):
            (_, la), (_, lb) = res[i]
            o = jnp.concatenate([acc_ref[2 * p] / la, acc_ref[2 * p + 1] / lb], axis=0).T
            sl = slice(p * LANES, (p + 1) * LANES)
            merged_ref[:, sl] = (conv_ref[0, :, sl] + sgb_ref[0, :, sl] * o).astype(BF16)

    y = alpha * x_ref[0] + _dot(merged_ref[...], wout_ref[...])
    out_ref[0] = _layer_norm(y, g_ref[...], b_ref[...])


def _attention(q, qi, kiwi, conv, sgb, x, kk, vvt, kiki, bias_tiles, w_out, g, b_ln, *, alpha, nq, q_off, kv_len,
               topk):
    b, t, _ = q.shape
    lpad = kk.shape[1]
    assert q_off % LANES == 0 and nq % LANES == 0 and t % nq == 0
    assert lpad % KB == 0 and lpad >= q_off + t and vvt.shape == (b, lpad // KB, N_KV_HEADS * HEAD_DIM, KB)
    grid = (b, t // nq)
    row = lambda n: pl.BlockSpec((1, nq, n), lambda i, j: (i, j, 0))
    keys = lambda n: pl.BlockSpec((1, lpad, n), lambda i, j: (i, 0, 0), pipeline_mode=pl.Buffered(1))
    const = lambda a: pl.BlockSpec(a.shape, lambda i, j: (0,) * a.ndim, pipeline_mode=pl.Buffered(1))
    return pl.pallas_call(
        functools.partial(_attn_kernel, alpha=alpha, nq=nq, q_off=q_off, kv_len=kv_len, topk=topk),
        grid=grid,
        in_specs=[row(D_MODEL), row(IDX_HEADS * IDX_DIM), row(LANES), row(D_MODEL), row(D_MODEL), row(D_MODEL),
                  keys(4 * LANES),
                  pl.BlockSpec((1, lpad // KB, N_KV_HEADS * HEAD_DIM, KB), lambda i, j: (i, 0, 0, 0),
                               pipeline_mode=pl.Buffered(1)),
                  keys(LANES), const(bias_tiles), const(w_out), const(g), const(b_ln)],
        out_specs=row(D_MODEL),
        out_shape=jax.ShapeDtypeStruct((b, t, D_MODEL), F32),
        scratch_shapes=[pltpu.VMEM((lpad, nq), I32), pltpu.VMEM((lpad, nq), F32),
                        pltpu.VMEM((IDX_HEADS, nq, LANES), BF16), pltpu.VMEM((LANES, LANES), BF16),
                        pltpu.VMEM((N_HEADS, nq, LANES), BF16), pltpu.VMEM((N_HEADS, HEAD_DIM, nq), F32),
                        pltpu.VMEM((nq, D_MODEL), BF16), pltpu.VMEM((lpad, nq), I16), pltpu.VMEM((lpad, nq), I16)],
        compiler_params=pltpu.CompilerParams(dimension_semantics=("arbitrary", "arbitrary"),
                                             vmem_limit_bytes=VMEM_LIMIT),
        name="dsa_attention",
    )(q, qi, kiwi, conv, sgb, x, kk, vvt, kiki, bias_tiles, w_out, g, b_ln)


def _ffn_kernel(x_ref, wg_ref, wu_ref, wd_ref, g_ref, b_ref, out_ref, *, alpha):
    x = x_ref[...]
    xb = x.astype(BF16)
    hg = _dot(xb, wg_ref[...])
    h = (hg * jax.nn.sigmoid(hg)) * _dot(xb, wu_ref[...])
    f = _dot(h.astype(BF16), wd_ref[...])
    out_ref[...] = _layer_norm(alpha * x + f, g_ref[...], b_ref[...])


def _ffn_ln(x, wg, wu, wd, g, b, alpha, tm):
    n = x.shape[0]
    row = pl.BlockSpec((tm, D_MODEL), lambda i: (i, 0))
    vec = pl.BlockSpec((1, D_MODEL), lambda i: (0, 0))
    full = lambda a: pl.BlockSpec(a.shape, lambda i: (0, 0), pipeline_mode=pl.Buffered(1))
    return pl.pallas_call(
        functools.partial(_ffn_kernel, alpha=alpha),
        grid=(n // tm,),
        in_specs=[row, full(wg), full(wu), full(wd), vec, vec],
        out_specs=row,
        out_shape=jax.ShapeDtypeStruct((n, D_MODEL), F32),
        compiler_params=pltpu.CompilerParams(dimension_semantics=("arbitrary",), vmem_limit_bytes=VMEM_LIMIT),
        name="ffn_ln",
    )(x, wg, wu, wd, g, b)


def _moe_kernel(x_ref, rw_ref, rb_ref, wgu_ref, wd_ref, g_ref, b_ref, out_ref,
                rank_ref, rankt_ref, combt_ref, xb_ref, acc_ref, *, alpha, tm):
    e = pl.program_id(1)
    sub = min(MOE_SUB, tm)
    assert sub & (sub - 1) == 0 and tm % sub == 0
    ch = min(MOE_CHUNK, sub)
    lane = lax.broadcasted_iota(I32, (tm, LANES), 1)

    @pl.when(e == 0)
    def _():
        x = x_ref[...]
        x_hi = x.astype(BF16)
        x_lo = (x - x_hi.astype(F32)).astype(BF16)
        logits = (_dot(x_hi, rw_ref[0]) + _dot(x_hi, rw_ref[1]) + _dot(x_lo, rw_ref[0])) + rb_ref[...]
        logits = jnp.where(lane < N_EXPERTS, logits, -jnp.inf)
        lanef = lane.astype(F32)
        v1 = jnp.max(logits, axis=1, keepdims=True)
        i1 = jnp.min(jnp.where(logits == v1, lanef, float(LANES)), axis=1, keepdims=True)
        rest = jnp.where(lanef == i1, -jnp.inf, logits)
        v2 = jnp.max(rest, axis=1, keepdims=True)
        i2 = jnp.min(jnp.where(rest == v2, lanef, float(LANES)), axis=1, keepdims=True)
        e2 = jnp.exp(v2 - v1)
        den = 1.0 + e2
        comb = jnp.where(lanef == i1, 1.0 / den, 0.0) + jnp.where(lanef == i2, e2 / den, 0.0)
        routed = (lanef == i1) | (lanef == i2)
        tr = lax.broadcasted_iota(I32, (tm, tm), 0)
        tc = lax.broadcasted_iota(I32, (tm, tm), 1)
        earlier = (tr > tc) & (tc >= tr - (tr & (sub - 1)))
        rank = _dot(earlier.astype(F32).astype(BF16), routed.astype(F32).astype(BF16))
        rank = jnp.where(routed, rank, -1.0)
        rank_ref[...] = rank.astype(I32)
        rankt_ref[...] = rank.T.astype(I32)
        combt_ref[...] = comb.T
        xb_ref[...] = x_ref[...].astype(BF16)
        acc_ref[...] = jnp.zeros((tm, D_MODEL), F32)

    rcol = jnp.max(jnp.where(lane == e, rank_ref[...], -1).astype(F32), axis=1, keepdims=True).astype(I32)
    rrow = rankt_ref[pl.ds(e, 1), :]
    crow = combt_ref[pl.ds(e, 1), :]

    for s0 in range(0, tm, sub):
        rcol_s, rrow_s, crow_s = rcol[s0:s0 + sub], rrow[:, s0:s0 + sub], crow[:, s0:s0 + sub]
        n_routed = jnp.max(rcol_s.astype(F32)).astype(I32) + 1

        def chunk(c, carry):
            base = c * ch
            hit = rrow_s == lax.broadcasted_iota(I32, (ch, sub), 0) + base
            xg = _dot(jnp.where(hit, 1.0, 0.0).astype(BF16), xb_ref[s0:s0 + sub, :]).astype(BF16)
            hgu = _dot(xg, wgu_ref[...])
            hg = hgu[:, :D_FF_EXPERT]
            h = (hg * jax.nn.sigmoid(hg)) * hgu[:, D_FF_EXPERT:]
            f = _dot(h.astype(BF16), wd_ref[...])
            gate = jnp.sum(jnp.where(hit, crow_s, 0.0), axis=1, keepdims=True)
            hit_t = rcol_s == lax.broadcasted_iota(I32, (sub, ch), 1) + base
            acc_ref[s0:s0 + sub, :] += _dot(jnp.where(hit_t, 1.0, 0.0).astype(BF16), (f * gate).astype(BF16))
            return carry

        lax.fori_loop(0, (n_routed + ch - 1) // ch, chunk, 0)

    @pl.when(e == N_EXPERTS - 1)
    def _():
        out_ref[...] = _layer_norm(alpha * x_ref[...] + acc_ref[...], g_ref[...], b_ref[...])


def _moe_ln(x, rw, rb, wgu, wd, g, b, alpha, tm):
    n = x.shape[0]
    row = pl.BlockSpec((tm, D_MODEL), lambda i, e: (i, 0))
    vec = pl.BlockSpec((1, D_MODEL), lambda i, e: (0, 0))
    return pl.pallas_call(
        functools.partial(_moe_kernel, alpha=alpha, tm=tm),
        grid=(n // tm, N_EXPERTS),
        in_specs=[row, pl.BlockSpec((2, D_MODEL, LANES), lambda i, e: (0, 0, 0)),
                  pl.BlockSpec((1, LANES), lambda i, e: (0, 0)),
                  pl.BlockSpec((None, D_MODEL, 2 * D_FF_EXPERT), lambda i, e: (e, 0, 0)),
                  pl.BlockSpec((None, D_FF_EXPERT, D_MODEL), lambda i, e: (e, 0, 0)),
                  vec, vec],
        out_specs=row,
        out_shape=jax.ShapeDtypeStruct((n, D_MODEL), F32),
        scratch_shapes=[pltpu.VMEM((tm, LANES), I32), pltpu.VMEM((LANES, tm), I32), pltpu.VMEM((LANES, tm), F32),
                        pltpu.VMEM((tm, D_MODEL), BF16), pltpu.VMEM((tm, D_MODEL), F32)],
        compiler_params=pltpu.CompilerParams(dimension_semantics=("arbitrary", "arbitrary"),
                                             vmem_limit_bytes=VMEM_LIMIT),
        name="moe_ln",
    )(x, rw, rb, wgu, wd, g, b)


def _split_w_in(w):
    sizes = (D_MODEL, D_MODEL, D_MODEL, N_HEADS * HEAD_DIM, N_KV_HEADS * HEAD_DIM, N_KV_HEADS * HEAD_DIM,
             IDX_HEADS * IDX_DIM, IDX_DIM, IDX_HEADS, D_MODEL, D_MODEL)
    parts, start = [], 0
    for n in sizes:
        parts.append(w[:, start:start + n])
        start += n
    bg, cg, xin, q, k, v, qi, ki, wi, ga, gb = parts

    def dup(a):
        a = a.reshape(a.shape[0], -1, HEAD_DIM)
        return jnp.concatenate([a, a], axis=-1).reshape(a.shape[0], -1)

    kiwi = jnp.concatenate([ki, wi, jnp.zeros((w.shape[0], LANES - IDX_DIM - IDX_HEADS), w.dtype)], axis=1)
    return {
        'a': jnp.concatenate([bg, cg, xin, ga], axis=1).astype(BF16),
        'q': q.astype(BF16), 'gb': gb.astype(BF16), 'kk': dup(k).astype(BF16), 'vv': v.astype(BF16),
        'qi': qi.astype(BF16), 'ki': jnp.concatenate([ki, ki, kiwi], axis=1).astype(BF16),
    }


def _dup_cache(a):
    a = a.astype(BF16)
    return jnp.concatenate([a, a], axis=-1).reshape(a.shape[0], a.shape[1], -1)


def _pad_rows(a, n):
    return jnp.pad(a, ((0, 0), (0, n - a.shape[1]), (0, 0)))


def _row_tile(n, cap):
    tm = min(n, cap)
    assert n % tm == 0
    return tm


def kernel(x_prompt, x_sample, cache_k, cache_v, cache_kidx, state_conv, w_in, conv_w, w_out, rel_bias,
           ln1_g, ln1_b, ln2_g, ln2_b, ffn_w_gate, ffn_w_up, ffn_w_down, router_w, router_b,
           moe_w_gate, moe_w_up, moe_w_down):
    depth = w_in.shape[0]
    alpha = (2 * depth) ** 0.25
    bp, tp, _ = x_prompt.shape
    bs, ts, _ = x_sample.shape
    past = cache_k.shape[2]
    nq_p = 256 if tp % 256 == 0 else LANES
    nq_s = LANES
    assert tp % KB == 0 and past % LANES == 0 and ts <= nq_s and ts % SUBLANES == 0
    topk_p = min(TOPK_MAX, tp // 4)
    topk_s = min(TOPK_MAX, (past + ts) // 4)
    lpad_s = -(-(past + nq_s) // KB) * KB

    bt_p = _bias_tiles(rel_bias, nq_p)
    bt_s = _bias_tiles(rel_bias, nq_s)
    vec = lambda a: a.reshape(1, -1)

    def layer(l, x, prev8, past_kv):
        b, t, _ = x.shape
        w = _split_w_in(w_in[l])
        prompt = past_kv is None
        (conv, q, sgb, kkf, kkb, vvf, vvb, qi, kiki, kiwi, cstate) = _inproj(
            x, prev8, w, conv_w[l], _row_tile(t, KB), transpose_v=prompt)
        proj = (w_out[l].astype(BF16), vec(ln1_g[l]), vec(ln1_b[l]))
        if prompt:
            x1 = _attention(q, qi, kiwi, conv, sgb, x, kkb, vvb, kiki, bt_p, *proj,
                            alpha=alpha, nq=nq_p, q_off=0, kv_len=t, topk=topk_p)
        else:
            ck, cv, cki = past_kv
            kk_all = _pad_rows(jnp.concatenate([_dup_cache(ck), kkb], axis=1), lpad_s)
            cv = cv.astype(BF16).reshape(b, cv.shape[1], N_KV_HEADS * HEAD_DIM)
            vv_all = _pad_rows(jnp.concatenate([cv, vvb], axis=1), lpad_s)
            vvt_all = jnp.swapaxes(vv_all.reshape(b, lpad_s // KB, KB, N_KV_HEADS * HEAD_DIM), 2, 3)
            cki = cki.astype(BF16)
            kiki_all = _pad_rows(jnp.concatenate([jnp.concatenate([cki, cki], axis=-1), kiki], axis=1), lpad_s)
            pad = lambda a: _pad_rows(a, nq_s)
            x1 = _attention(pad(q), pad(qi), pad(kiwi), pad(conv), pad(sgb), pad(x), kk_all, vvt_all, kiki_all, bt_s,
                            *proj, alpha=alpha, nq=nq_s, q_off=past, kv_len=past + t, topk=topk_s)[:, :t]
        n = b * t
        x1 = x1.reshape(n, D_MODEL)
        j = l // 2
        if l % 2 == 0:
            x2 = _ffn_ln(x1, ffn_w_gate[j].astype(BF16), ffn_w_up[j].astype(BF16), ffn_w_down[j].astype(BF16),
                         vec(ln2_g[l]), vec(ln2_b[l]), alpha, _row_tile(n, 512))
        else:
            rw = jnp.pad(router_w[j], ((0, 0), (0, LANES - N_EXPERTS)))
            rw_hi = rw.astype(BF16)
            rw = jnp.stack([rw_hi, (rw - rw_hi.astype(F32)).astype(BF16)])
            rb = jnp.pad(router_b[j], (0, LANES - N_EXPERTS)).reshape(1, LANES)
            wgu = jnp.concatenate([moe_w_gate[j], moe_w_up[j]], axis=-1).astype(BF16)
            x2 = _moe_ln(x1, rw, rb, wgu, moe_w_down[j].astype(BF16), vec(ln2_g[l]), vec(ln2_b[l]), alpha,
                         _row_tile(n, 2 * MOE_SUB))
        heads = lambda a: a.reshape(b, t, N_KV_HEADS, HEAD_DIM)
        return (x2.reshape(b, t, D_MODEL), heads(kkf), heads(vvf), kiwi[..., :IDX_DIM],
                cstate[:, SUBLANES - (CONV_W - 1):, :])

    xp, xs = x_prompt, x_sample
    outs_p, outs_s = [], []
    zero_prev = jnp.zeros((bp, SUBLANES, D_MODEL), F32)
    for l in range(depth):
        xp, *rest = layer(l, xp, zero_prev, None)
        outs_p.append(rest)
        prev8 = jnp.pad(state_conv[l], ((0, 0), (SUBLANES - (CONV_W - 1), 0), (0, 0)))
        xs, *rest = layer(l, xs, prev8, (cache_k[l], cache_v[l], cache_kidx[l]))
        outs_s.append(rest)
    stack = lambda outs, i: jnp.stack([o[i] for o in outs])
    return (xp, xs,
            stack(outs_p, 0), stack(outs_p, 1), stack(outs_p, 2), stack(outs_p, 3),
            stack(outs_s, 0), stack(outs_s, 1), stack(outs_s, 2), stack(outs_s, 3))
```
